```python
import jax, jax.numpy as jnp
from jax import lax
import numpy as np

D_MODEL = 1024
BATCH = 2
SEQ = 8192
DEPTH = 4

GRID_W = 64
CTX_LEN = 256
HEAD_DIM = 64
ROPE_PAIRS = HEAD_DIM // 4
ROPE_BASE = 10000.0
RET_HEADS = 4
RET_CHUNK = 128
RET_W = RET_HEADS * HEAD_DIM
RET_SCALE = HEAD_DIM ** -0.5
SGU_GROUPS = 4
SGU_CHUNK = 128
SGU_W = SGU_GROUPS * HEAD_DIM
ATT_Q_HEADS = 8
ATT_KV_HEADS = 2
GQA_GROUP = ATT_Q_HEADS // ATT_KV_HEADS
WINDOW = 128
ATT_QW = ATT_Q_HEADS * HEAD_DIM
ATT_KVW = ATT_KV_HEADS * HEAD_DIM
ATT_SCALE = HEAD_DIM ** -0.5
IN_WIDTHS = (RET_W, RET_W, RET_W, RET_W, SGU_W, SGU_W, ATT_QW, ATT_KVW, ATT_KVW)
D_IN = sum(IN_WIDTHS)
D_MIX = RET_W + SGU_W + ATT_QW
PEER_HEADS = 8
PEER_NKEYS = 128
PEER_N = PEER_NKEYS * PEER_NKEYS
PEER_QDIM = 256
PEER_TOPK = 16
PEER_CHUNK = 128
LN_EPS = 1e-5
DEEPNORM_ALPHA = (2 * DEPTH) ** 0.25
DEEPNORM_BETA = (8 * DEPTH) ** -0.25

kernel_name = 'hybrid_dit_retention_sgu_swa_peer'


def layer_norm(x, g=None, b=None):
    xf = x.astype(jnp.float32)
    mu = jnp.mean(xf, -1, keepdims=True)
    var = jnp.mean(jnp.square(xf - mu), -1, keepdims=True)
    y = (xf - mu) * lax.rsqrt(var + LN_EPS)
    if g is not None:
        y = y * g.astype(jnp.float32) + b.astype(jnp.float32)
    return y.astype(x.dtype)


def modulate(x, shift, scale):
    return (layer_norm(x) * (1 + scale) + shift).astype(x.dtype)


def _heads(t, n):
    return t.reshape(t.shape[:-1] + (n, HEAD_DIM))


def _gqa(t):
    return t.reshape(t.shape[:-1] + (ATT_KV_HEADS, GQA_GROUP, HEAD_DIM))


def axial_rope_tables(n_tokens):
    rows = n_tokens // GRID_W
    row_id = jnp.repeat(jnp.arange(rows), GRID_W).astype(jnp.float32)
    col_id = jnp.tile(jnp.arange(GRID_W), rows).astype(jnp.float32)
    inv = jnp.power(ROPE_BASE, -jnp.arange(ROPE_PAIRS, dtype=jnp.float32) / ROPE_PAIRS)
    ang_r = row_id[:, None] * inv
    ang_c = col_id[:, None] * inv
    return (jnp.cos(ang_r), jnp.sin(ang_r), jnp.cos(ang_c), jnp.sin(ang_c))


def _rot(x, cos, sin):
    x1, x2 = jnp.split(x, 2, axis=-1)
    return jnp.concatenate([x1 * cos - x2 * sin, x2 * cos + x1 * sin], -1)


def apply_axial_rope(x, tabs):
    shape = (x.shape[1],) + (1,) * (x.ndim - 3) + (ROPE_PAIRS,)
    cr, sr, cc, sc = [t.reshape(shape) for t in tabs]
    half = HEAD_DIM // 2
    out = jnp.concatenate([_rot(x[..., :half], cr, sr), _rot(x[..., half:], cc, sc)], -1)
    return out.astype(x.dtype)


def retention_scan(q, k, v, log_g, s0):
    B, L, H, _ = q.shape
    dv = v.shape[-1]
    C = RET_CHUNK
    n = L // C

    def to_chunks(t):
        return t.astype(jnp.float32).reshape(B, n, C, H, t.shape[-1]).transpose(1, 0, 3, 2, 4)

    i = jnp.arange(C, dtype=jnp.float32)
    rel = i[:, None] - i[None, :]
    intra = jnp.exp(jnp.maximum(rel, 0.0)[None] * log_g[:, None, None]) * (rel >= 0)[None]
    q_dec = jnp.exp((i + 1)[None, :] * log_g[:, None])
    k_dec = jnp.exp((C - 1 - i)[None, :] * log_g[:, None])
    c_dec = jnp.exp(C * log_g)

    def step(state, qkv):
        qc, kc, vc = qkv
        att = jnp.einsum('bhid,bhjd->bhij', qc, kc) * intra
        o = (jnp.einsum('bhij,bhjv->bhiv', att, vc)
             + jnp.einsum('bhid,bhdv->bhiv', qc * q_dec[..., None], state))
        state = state * c_dec[:, None, None] + jnp.einsum('bhjd,bhjv->bhdv', kc * k_dec[..., None], vc)
        return state, o

    state, o = lax.scan(step, s0, (to_chunks(q), to_chunks(k), to_chunks(v)))
    o = o.transpose(1, 0, 3, 2, 4).reshape(B, L, H, dv)
    return o, state


def retention_state(k, v, log_g):
    L = k.shape[1]
    w = jnp.exp(jnp.arange(L - 1, -1, -1, dtype=jnp.float32)[None, :] * log_g[:, None])
    return jnp.einsum('blhd,hl,blhv->bhdv', k.astype(jnp.float32), w, v.astype(jnp.float32))


def bidir_retention(q_lat, k_lat, v_lat, q_ctx, k_ctx, v_ctx, logit_f, logit_b, ctx_out):
    lg_f = jax.nn.log_sigmoid(logit_f.astype(jnp.float32))
    lg_b = jax.nn.log_sigmoid(logit_b.astype(jnp.float32))
    B = q_lat.shape[0]
    s0 = jnp.zeros((B, RET_HEADS, HEAD_DIM, HEAD_DIM), jnp.float32)
    flip = lambda t: t[:, ::-1]
    if ctx_out:
        oc_f, s_f = retention_scan(q_ctx, k_ctx, v_ctx, lg_f, s0)
        oc_b, s_b = retention_scan(flip(q_ctx), flip(k_ctx), flip(v_ctx), lg_b, s0)
        o_ctx = oc_f + flip(oc_b)
    else:
        s_f = retention_state(k_ctx, v_ctx, lg_f)
        s_b = retention_state(flip(k_ctx), flip(v_ctx), lg_b)
        o_ctx = None
    ol_f, _ = retention_scan(q_lat, k_lat, v_lat, lg_f, s_f)
    ol_b, _ = retention_scan(flip(q_lat), flip(k_lat), flip(v_lat), lg_b, s_b)
    return o_ctx, ol_f + flip(ol_b)


def retention_output(o, gate):
    B, L = o.shape[:2]
    y = layer_norm(o) * jax.nn.silu(_heads(gate, RET_HEADS).astype(jnp.float32))
    return y.reshape(B, L, RET_W).astype(gate.dtype)


def chunk_token_mlp(u, v, w_s, b_s):
    B, L = u.shape[:2]
    n = L // SGU_CHUNK
    u = jax.nn.gelu(u).reshape(B, n, SGU_CHUNK, SGU_GROUPS, HEAD_DIM)
    v = layer_norm(_heads(jax.nn.gelu(v), SGU_GROUPS)).reshape(B, n, SGU_CHUNK, SGU_GROUPS, HEAD_DIM)
    mix = jnp.einsum('gij,bnjgd->bnigd', w_s, v) + b_s.T[:, :, None]
    return (u * mix).reshape(B, L, SGU_W).astype(u.dtype)


def window_attention(q, k, v, k_ctx, v_ctx, sink):
    B, S = q.shape[:2]
    W = WINDOW
    n = S // W
    qb = q.reshape(B, n, W, ATT_KV_HEADS, GQA_GROUP, HEAD_DIM)

    def banded(t):
        tp = jnp.pad(t, ((0, 0), (W, W), (0, 0), (0, 0))).reshape(B, n + 2, W, ATT_KV_HEADS, HEAD_DIM)
        return jnp.concatenate([tp[:, :-2], tp[:, 1:-1], tp[:, 2:]], axis=2)

    kb, vb = banded(k), banded(v)
    qi = jnp.arange(W)
    kj = jnp.arange(3 * W)
    blk = jnp.arange(n)
    rel = kj[None, :] - W - qi[:, None]
    kpos = blk[:, None] * W - W + kj[None, :]
    mask = (jnp.abs(rel) <= W)[None] & ((kpos >= 0) & (kpos < S))[:, None, :]
    s_loc = jnp.einsum('bnqhgd,bnkhd->bnhgqk', qb, kb).astype(jnp.float32)
    s_loc = jnp.where(mask[None, :, None, None], s_loc, -jnp.inf)
    s_ctx = jnp.einsum('bnqhgd,bchd->bnhgqc', qb, k_ctx).astype(jnp.float32)
    s_sink = jnp.broadcast_to(sink.astype(jnp.float32)[None, None, :, :, None, None], s_loc.shape[:-1] + (1,))
    p = jax.nn.softmax(jnp.concatenate([s_loc, s_ctx, s_sink], -1), axis=-1).astype(v.dtype)
    n_loc = 3 * W
    n_ctx = k_ctx.shape[1]
    o = (jnp.einsum('bnhgqk,bnkhd->bnqhgd', p[..., :n_loc], vb)
         + jnp.einsum('bnhgqc,bchd->bnqhgd', p[..., n_loc:n_loc + n_ctx], v_ctx))
    return o.reshape(B, S, ATT_QW)


def context_attention(q, k, v, sink):
    s = jnp.einsum('bqhgd,bkhd->bhgqk', q, k).astype(jnp.float32)
    s_sink = jnp.broadcast_to(sink.astype(jnp.float32)[None, :, :, None, None], s.shape[:-1] + (1,))
    p = jax.nn.softmax(jnp.concatenate([s, s_sink], -1), axis=-1)[..., :-1].astype(v.dtype)
    o = jnp.einsum('bhgqk,bkhd->bqhgd', p, v)
    return o.reshape(o.shape[:2] + (ATT_QW,))


def mix_sublayer(h_lat, h_ctx, w_in, w_out, ret_logit_f, ret_logit_b, sgu_w, sgu_b, sink, rope, ctx_out):
    points = np.cumsum(IN_WIDTHS)[:-1].tolist()
    rq, rk, rv, rg, su, sv, aq, ak, av = jnp.split(h_lat @ w_in, points, axis=-1)
    cq, ck, cv, cg, csu, csv, caq, cak, cav = jnp.split(h_ctx @ w_in, points, axis=-1)
    o_ctx_a, o_lat_a = bidir_retention(
        apply_axial_rope(_heads(rq, RET_HEADS), rope),
        apply_axial_rope(_heads(rk, RET_HEADS), rope) * RET_SCALE,
        _heads(rv, RET_HEADS),
        _heads(cq, RET_HEADS), _heads(ck, RET_HEADS) * RET_SCALE, _heads(cv, RET_HEADS),
        ret_logit_f, ret_logit_b, ctx_out)
    a_lat = retention_output(o_lat_a, rg)
    b_lat = chunk_token_mlp(su, sv, sgu_w, sgu_b)
    ctx_k = _heads(cak, ATT_KV_HEADS)
    ctx_v = _heads(cav, ATT_KV_HEADS)
    sink_g = sink.reshape(ATT_KV_HEADS, GQA_GROUP)
    c_lat = window_attention(apply_axial_rope(_gqa(aq), rope) * ATT_SCALE,
                             apply_axial_rope(_heads(ak, ATT_KV_HEADS), rope),
                             _heads(av, ATT_KV_HEADS), ctx_k, ctx_v, sink_g)
    y_lat = jnp.concatenate([a_lat, b_lat, c_lat], -1) @ w_out
    if not ctx_out:
        return None, y_lat
    a_ctx = retention_output(o_ctx_a, cg)
    b_ctx = chunk_token_mlp(csu, csv, sgu_w, sgu_b)
    att_ctx = context_attention(_gqa(caq) * ATT_SCALE, ctx_k, ctx_v, sink_g)
    y_ctx = jnp.concatenate([a_ctx, b_ctx, att_ctx], -1) @ w_out
    return y_ctx, y_lat


def peer_ffn(h, wq, k1, k2, u_tab, v_tab):
    shp = h.shape
    t = h.reshape(-1, D_MODEL)
    T = t.shape[0]
    q = (t @ wq).reshape(T, PEER_HEADS, PEER_QDIM).astype(jnp.float32)
    q1, q2 = jnp.split(q, 2, axis=-1)
    s1 = jnp.einsum('thd,kd->thk', q1, k1.astype(jnp.float32))
    s2 = jnp.einsum('thd,kd->thk', q2, k2.astype(jnp.float32))
    v1, i1 = lax.top_k(s1, PEER_TOPK)
    v2, i2 = lax.top_k(s2, PEER_TOPK)
    cand = (v1[..., :, None] + v2[..., None, :]).reshape(T, PEER_HEADS, PEER_TOPK * PEER_TOPK)
    cidx = (i1[..., :, None] * PEER_NKEYS + i2[..., None, :]).reshape(T, PEER_HEADS, PEER_TOPK * PEER_TOPK)
    top, sel = lax.top_k(cand, PEER_TOPK)
    eidx = jnp.take_along_axis(cidx, sel, axis=-1)
    gate = jax.nn.softmax(top, axis=-1).astype(h.dtype)
    n = T // PEER_CHUNK
    n_sel = PEER_HEADS * PEER_TOPK

    def expert_block(args):
        xb, eb, gb = args
        act = jax.nn.gelu(jnp.einsum('td,tkd->tk', xb, u_tab[eb]))
        return jnp.einsum('tk,tkd->td', gb * act, v_tab[eb])

    out = lax.map(expert_block, (t.reshape(n, PEER_CHUNK, D_MODEL),
                                 eidx.reshape(n, PEER_CHUNK, n_sel),
                                 gate.reshape(n, PEER_CHUNK, n_sel)))
    return out.reshape(shp)


def setup_inputs(seed: int = 0) -> dict:
    key = jax.random.key(seed)
    ks = jax.random.split(key, 24)
    f32 = jnp.float32

    def nrm(k, shape, scale):
        return jax.random.normal(k, shape, f32) * scale

    ret_base = jnp.asarray(np.log(2.0 ** (5 + np.arange(RET_HEADS)) - 1.0), f32)
    return {
        'x': nrm(ks[0], (BATCH, SEQ, D_MODEL), 1.0),
        'c': nrm(ks[1], (BATCH, D_MODEL), 1.0),
        'ctx': nrm(ks[2], (BATCH, CTX_LEN, D_MODEL), 1.0),
        'c_ctx': nrm(ks[3], (D_MODEL,), 1.0),
        'w_mod': nrm(ks[4], (DEPTH, D_MODEL, 6 * D_MODEL), 0.5 * D_MODEL ** -0.5),
        'b_mod': nrm(ks[5], (DEPTH, 6 * D_MODEL), 0.02),
        'w_in': nrm(ks[6], (DEPTH, D_MODEL, D_IN), D_MODEL ** -0.5),
        'w_out': nrm(ks[7], (DEPTH, D_MIX, D_MODEL), DEEPNORM_BETA * D_MIX ** -0.5),
        'ret_decay_fwd': ret_base + nrm(ks[8], (DEPTH, RET_HEADS), 0.05),
        'ret_decay_bwd': ret_base + nrm(ks[9], (DEPTH, RET_HEADS), 0.05),
        'sgu_w': nrm(ks[10], (DEPTH, SGU_GROUPS, SGU_CHUNK, SGU_CHUNK), SGU_CHUNK ** -0.5),
        'sgu_b': 1.0 + nrm(ks[11], (DEPTH, SGU_GROUPS, SGU_CHUNK), 0.01),
        'attn_sink': nrm(ks[12], (DEPTH, ATT_Q_HEADS), 0.5),
        'ln_mix_g': 1.0 + nrm(ks[13], (DEPTH, D_MODEL), 0.02),
        'ln_mix_b': nrm(ks[14], (DEPTH, D_MODEL), 0.02),
        'peer_wq': nrm(ks[15], (DEPTH, D_MODEL, PEER_HEADS * PEER_QDIM), D_MODEL ** -0.5),
        'peer_k1': nrm(ks[16], (DEPTH, PEER_NKEYS, PEER_QDIM // 2), (PEER_QDIM // 2) ** -0.5),
        'peer_k2': nrm(ks[17], (DEPTH, PEER_NKEYS, PEER_QDIM // 2), (PEER_QDIM // 2) ** -0.5),
        'peer_u': nrm(ks[18], (DEPTH, PEER_N, D_MODEL), D_MODEL ** -0.5),
        'peer_v': nrm(ks[19], (DEPTH, PEER_N, D_MODEL), DEEPNORM_BETA),
        'ln_ffn_g': 1.0 + nrm(ks[20], (DEPTH, D_MODEL), 0.02),
        'ln_ffn_b': nrm(ks[21], (DEPTH, D_MODEL), 0.02),
    }


def reference(x, c, ctx, c_ctx, w_mod, b_mod, w_in, w_out, ret_decay_fwd, ret_decay_bwd,
              sgu_w, sgu_b, attn_sink, ln_mix_g, ln_mix_b, peer_wq, peer_k1, peer_k2,
              peer_u, peer_v, ln_ffn_g, ln_ffn_b):
    rope = axial_rope_tables(x.shape[1])
    x_lat, x_ctx = x, ctx
    for l in range(DEPTH):
        ctx_out = l < DEPTH - 1
        mod_l = (jax.nn.silu(c) @ w_mod[l] + b_mod[l])[:, None, :]
        mod_c = jax.nn.silu(c_ctx) @ w_mod[l] + b_mod[l]
        sh1, sc1, g1, sh2, sc2, g2 = jnp.split(mod_l, 6, axis=-1)
        csh1, csc1, cg1, csh2, csc2, cg2 = jnp.split(mod_c, 6, axis=-1)
        y_ctx, y_lat = mix_sublayer(modulate(x_lat, sh1, sc1), modulate(x_ctx, csh1, csc1),
                                    w_in[l], w_out[l], ret_decay_fwd[l], ret_decay_bwd[l],
                                    sgu_w[l], sgu_b[l], attn_sink[l], rope, ctx_out)
        x_lat = layer_norm(DEEPNORM_ALPHA * x_lat + g1 * y_lat, ln_mix_g[l], ln_mix_b[l])
        f_lat = peer_ffn(modulate(x_lat, sh2, sc2), peer_wq[l], peer_k1[l], peer_k2[l], peer_u[l], peer_v[l])
        x_lat = layer_norm(DEEPNORM_ALPHA * x_lat + g2 * f_lat, ln_ffn_g[l], ln_ffn_b[l])
        if ctx_out:
            x_ctx = layer_norm(DEEPNORM_ALPHA * x_ctx + cg1 * y_ctx, ln_mix_g[l], ln_mix_b[l])
            f_ctx = peer_ffn(modulate(x_ctx, csh2, csc2), peer_wq[l], peer_k1[l], peer_k2[l], peer_u[l], peer_v[l])
            x_ctx = layer_norm(DEEPNORM_ALPHA * x_ctx + cg2 * f_ctx, ln_ffn_g[l], ln_ffn_b[l])
    return x_lat
```

```python
import functools

import numpy as np
import jax
import jax.numpy as jnp
from jax import lax
from jax.experimental import pallas as pl
from jax.experimental.pallas import tpu as pltpu

F32 = jnp.float32
BF16 = jnp.bfloat16

D_MODEL = 1024
DEPTH = 4
GRID_W = 64
HEAD_DIM = 64
ROPE_PAIRS = HEAD_DIM // 4
ROPE_BASE = 10000.0
RET_HEADS = 4
CHUNK = 128
RET_W = RET_HEADS * HEAD_DIM
RET_SCALE = HEAD_DIM ** -0.5
SGU_GROUPS = 4
SGU_W = SGU_GROUPS * HEAD_DIM
ATT_Q_HEADS = 8
ATT_KV_HEADS = 2
GQA_GROUP = ATT_Q_HEADS // ATT_KV_HEADS
ATT_QW = ATT_Q_HEADS * HEAD_DIM
ATT_KVW = ATT_KV_HEADS * HEAD_DIM
ATT_SCALE = HEAD_DIM ** -0.5
D_IN = 4 * RET_W + 2 * SGU_W + ATT_QW + 2 * ATT_KVW
D_ROT = 2 * RET_W + ATT_QW + ATT_KVW
PEER_HEADS = 8
PEER_NKEYS = 128
PEER_N = PEER_NKEYS * PEER_NKEYS
PEER_QDIM = 256
PEER_TOPK = 16
LN_EPS = 1e-5
DEEPNORM_ALPHA = (2 * DEPTH) ** 0.25

LANES = 128
TOK_TILE = 256
PEER_TOK = 512
PEER_EB = 1024
PEER_SUB = 256
VMEM_LIMIT = 56 * 1024 * 1024

NEG_INF = float("-inf")


def _ln(x):
    mu = jnp.mean(x, axis=-1, keepdims=True)
    xc = x - mu
    var = jnp.mean(xc * xc, axis=-1, keepdims=True)
    return xc * lax.rsqrt(var + LN_EPS)


def _gelu(x):
    return 0.5 * x * (1.0 + jnp.tanh(0.7978845608028654 * (x + 0.044715 * (x * x * x))))


def _silu(x):
    return x * (1.0 / (1.0 + jnp.exp(-x)))


def _dot(a, b):
    return jnp.dot(a, b, preferred_element_type=F32)


def _dot_nt(a, b):
    return lax.dot_general(a, b, (((1,), (1,)), ((), ())), preferred_element_type=F32)


def _dot_tn(a, b):
    return lax.dot_general(a, b, (((0,), (0,)), ((), ())), preferred_element_type=F32)


def _group_mean(z, avg):
    hi = z.astype(BF16)
    lo = (z - hi.astype(F32)).astype(BF16)
    return _dot(hi, avg) + _dot(lo, avg)


def _group_ln(x, avg):
    mu = _group_mean(x, avg)
    xc = x - mu
    var = _group_mean(xc * xc, avg)
    return xc * lax.rsqrt(var + LN_EPS)


def _group_avg_matrix(width):
    r = lax.broadcasted_iota(jnp.int32, (width, width), 0) // HEAD_DIM
    c = lax.broadcasted_iota(jnp.int32, (width, width), 1) // HEAD_DIM
    return jnp.where(r == c, 1.0 / HEAD_DIM, 0.0).astype(BF16)


def _head_mask_stack(n_heads, rows, width):
    r = lax.broadcasted_iota(jnp.int32, (n_heads * rows, width), 0) // rows
    c = lax.broadcasted_iota(jnp.int32, (n_heads * rows, width), 1) // HEAD_DIM
    return r == c


def _mod_kernel(c_ref, w_ref, b_ref, o_ref):
    s = _silu(c_ref[...])
    hi = s.astype(BF16)
    lo = (s - hi.astype(F32)).astype(BF16)
    w = w_ref[0]
    whi = w.astype(BF16)
    wlo = (w - whi.astype(F32)).astype(BF16)
    o_ref[0] = _dot(hi, whi) + _dot(lo, whi) + _dot(hi, wlo) + b_ref[0]


def _modulations(cond_rows, w_mod, b_mod):
    depth = w_mod.shape[0]
    n_rows = cond_rows.shape[0]
    col = 1024
    n_col = w_mod.shape[2] // col
    return pl.pallas_call(
        _mod_kernel,
        grid=(depth, n_col),
        in_specs=[
            pl.BlockSpec((n_rows, D_MODEL), lambda l, j: (0, 0)),
            pl.BlockSpec((1, D_MODEL, col), lambda l, j: (l, 0, j)),
            pl.BlockSpec((1, 1, col), lambda l, j: (l, 0, j)),
        ],
        out_specs=pl.BlockSpec((1, n_rows, col), lambda l, j: (l, 0, j)),
        out_shape=jax.ShapeDtypeStruct((depth, n_rows, w_mod.shape[2]), F32),
        name="adaln_mod",
    )(cond_rows, w_mod, b_mod.reshape(depth, 1, -1))


def _inproj_kernel(x_ref, mod_ref, w_ref, cos_ref, sin_ref,
                   rq_ref, rk_ref, rv_ref, rg_ref, su_ref, sv_ref, aq_ref, ak_ref, av_ref,
                   *, tiles_per_batch, n_batch):
    i = pl.program_id(0)
    grp = jnp.minimum(i // tiles_per_batch, n_batch)
    shift = mod_ref[pl.ds(grp, 1), 0:D_MODEL]
    scale = mod_ref[pl.ds(grp, 1), D_MODEL:2 * D_MODEL]
    h = (_ln(x_ref[...]) * (1.0 + scale) + shift).astype(BF16)
    cos = cos_ref[...]
    sin = sin_ref[...]

    def proj(c0, w):
        return _dot(h, w_ref[:, c0:c0 + w])

    rot0 = D_IN
    rq_ref[...] = (proj(0, RET_W) * cos + proj(rot0, RET_W) * sin).astype(BF16)
    rk_ref[...] = ((proj(RET_W, RET_W) * cos + proj(rot0 + RET_W, RET_W) * sin) * RET_SCALE).astype(BF16)
    rv_ref[...] = proj(2 * RET_W, RET_W).astype(BF16)
    rg_ref[...] = proj(3 * RET_W, RET_W)
    su_ref[...] = proj(4 * RET_W, SGU_W)
    sv_ref[...] = proj(4 * RET_W + SGU_W, SGU_W)
    aq0 = 4 * RET_W + 2 * SGU_W
    aqr = rot0 + 2 * RET_W
    for half in range(ATT_QW // RET_W):
        o = half * RET_W
        aq_ref[:, o:o + RET_W] = ((proj(aq0 + o, RET_W) * cos + proj(aqr + o, RET_W) * sin) * ATT_SCALE).astype(BF16)
    ak0 = aq0 + ATT_QW
    akr = aqr + ATT_QW
    ak_ref[...] = (proj(ak0, ATT_KVW) * cos[:, 0:ATT_KVW] + proj(akr, ATT_KVW) * sin[:, 0:ATT_KVW]).astype(BF16)
    av_ref[...] = proj(ak0 + ATT_KVW, ATT_KVW).astype(BF16)


def _inproj(x_all, mod_l, w_ext, cos_tab, sin_tab, *, n_batch, seq):
    n_tok = x_all.shape[0]
    n_tiles = n_tok // TOK_TILE
    tiles_per_batch = seq // TOK_TILE
    n_lat_tiles = n_batch * tiles_per_batch

    def tab_map(i):
        return (jnp.where(i < n_lat_tiles, i % tiles_per_batch, tiles_per_batch), 0)

    tok = lambda w: pl.BlockSpec((TOK_TILE, w), lambda i: (i, 0))
    full = lambda a: pl.BlockSpec(a.shape, lambda i: (0,) * a.ndim)
    out_w = [(RET_W, BF16), (RET_W, BF16), (RET_W, BF16), (RET_W, F32), (SGU_W, F32), (SGU_W, F32),
             (ATT_QW, BF16), (ATT_KVW, BF16), (ATT_KVW, BF16)]
    return pl.pallas_call(
        functools.partial(_inproj_kernel, tiles_per_batch=tiles_per_batch, n_batch=n_batch),
        grid=(n_tiles,),
        in_specs=[tok(D_MODEL), full(mod_l), full(w_ext),
                  pl.BlockSpec((TOK_TILE, RET_W), tab_map), pl.BlockSpec((TOK_TILE, RET_W), tab_map)],
        out_specs=[tok(w) for w, _ in out_w],
        out_shape=[jax.ShapeDtypeStruct((n_tok, w), dt) for w, dt in out_w],
        compiler_params=pltpu.CompilerParams(dimension_semantics=("arbitrary",), vmem_limit_bytes=VMEM_LIMIT),
        name="inproj",
    )(x_all, mod_l, w_ext, cos_tab, sin_tab)


def _ret_kernel(dec_ref, qf_ref, kf_ref, vf_ref, qb_ref, kb_ref, vb_ref, of_ref, ob_ref,
                sf_ref, sb_ref, intra_ref, qd_ref, kd_ref, cd_ref):
    b = pl.program_id(0)
    s = pl.program_id(1)
    C = CHUNK
    W = RET_W

    @pl.when((b == 0) & (s == 0))
    def _tables():
        dec = dec_ref[...]
        lg = jnp.minimum(dec, 0.0) - jnp.log(1.0 + jnp.exp(-jnp.abs(dec)))
        lane_head = lax.broadcasted_iota(jnp.int32, (1, W), 1) // HEAD_DIM
        ii = lax.broadcasted_iota(jnp.int32, (C, C), 0)
        jj = lax.broadcasted_iota(jnp.int32, (C, C), 1)
        ri = lax.broadcasted_iota(jnp.int32, (C, W), 0).astype(F32)
        rb = lax.broadcasted_iota(jnp.int32, (W, W), 0) // HEAD_DIM
        cb = lax.broadcasted_iota(jnp.int32, (W, W), 1) // HEAD_DIM
        for d in range(2):
            lgl = jnp.zeros((1, W), F32)
            for hh in range(RET_HEADS):
                lg_h = lg[d:d + 1, hh:hh + 1]
                lgl = lgl + jnp.where(lane_head == hh, lg_h, 0.0)
                rel = (ii - jj) if d == 0 else (jj - ii)
                m = jnp.exp(jnp.maximum(rel, 0).astype(F32) * lg_h)
                intra_ref[d, hh * C:(hh + 1) * C, :] = jnp.where(rel >= 0, m, 0.0)
            if d == 0:
                qd_ref[d] = jnp.exp((ri + 1.0) * lgl)
                kd_ref[d] = jnp.exp((C - 1.0 - ri) * lgl)
            else:
                qd_ref[d] = jnp.exp((C - ri) * lgl)
                kd_ref[d] = jnp.exp(ri * lgl)
            cd_ref[d] = jnp.where(rb == cb, jnp.exp(C * lgl), 0.0)

    @pl.when(s == 0)
    def _zero():
        sf_ref[...] = jnp.zeros_like(sf_ref)
        sb_ref[...] = jnp.zeros_like(sb_ref)

    hm = _head_mask_stack(RET_HEADS, C, W)
    rb = lax.broadcasted_iota(jnp.int32, (W, W), 0) // HEAD_DIM
    cb = lax.broadcasted_iota(jnp.int32, (W, W), 1) // HEAD_DIM
    bd = rb == cb
    zero_b = jnp.zeros((), BF16)

    def direction(d, q_ref, k_ref, v_ref, st_ref, o_ref):
        q = q_ref[...]
        k = k_ref[...]
        v = v_ref[...]
        qs = jnp.where(hm, jnp.concatenate([q] * RET_HEADS, axis=0), zero_b)
        att = (_dot_nt(qs, k) * intra_ref[d]).astype(BF16)
        att = jnp.concatenate([att[hh * C:(hh + 1) * C] for hh in range(RET_HEADS)], axis=1)
        vs = jnp.where(hm, jnp.concatenate([v] * RET_HEADS, axis=0), zero_b)
        st = st_ref[...]
        o = _dot(att, vs) + _dot(q, st.astype(BF16)) * qd_ref[d]
        o_ref[...] = o
        kdec = (k.astype(F32) * kd_ref[d]).astype(BF16)
        st_ref[...] = st * cd_ref[d] + jnp.where(bd, _dot_tn(kdec, v), 0.0)

    direction(0, qf_ref, kf_ref, vf_ref, sf_ref, of_ref)
    direction(1, qb_ref, kb_ref, vb_ref, sb_ref, ob_ref)


def _retention(decays, rq, rk, rv, *, n_batch, seq, ctx_len):
    n_tok = rq.shape[0]
    nl = seq // CHUNK
    nc = ctx_len // CHUNK
    ctx0 = n_batch * nl

    def fwd_map(b, s):
        return (jnp.where(s < nc, ctx0 + b * nc + s, b * nl + (s - nc)), 0)

    def bwd_map(b, s):
        return (jnp.where(s < nc, ctx0 + b * nc + (nc - 1 - s), b * nl + (nl - 1 - (s - nc))), 0)

    fspec = pl.BlockSpec((CHUNK, RET_W), fwd_map)
    bspec = pl.BlockSpec((CHUNK, RET_W), bwd_map)
    return pl.pallas_call(
        _ret_kernel,
        grid=(n_batch, nc + nl),
        in_specs=[pl.BlockSpec(decays.shape, lambda b, s: (0, 0)), fspec, fspec, fspec, bspec, bspec, bspec],
        out_specs=[fspec, bspec],
        out_shape=[jax.ShapeDtypeStruct((n_tok, RET_W), F32)] * 2,
        scratch_shapes=[
            pltpu.VMEM((RET_W, RET_W), F32), pltpu.VMEM((RET_W, RET_W), F32),
            pltpu.VMEM((2, RET_HEADS * CHUNK, CHUNK), F32),
            pltpu.VMEM((2, CHUNK, RET_W), F32), pltpu.VMEM((2, CHUNK, RET_W), F32),
            pltpu.VMEM((2, RET_W, RET_W), F32),
        ],
        compiler_params=pltpu.CompilerParams(dimension_semantics=("arbitrary", "arbitrary")),
        name="retention",
    )(decays, rq, rk, rv, rq, rk, rv)


def _attn_kernel(sink_ref, q_ref, kp_ref, kc_ref, kn_ref, vp_ref, vc_ref, vn_ref, kx_ref, vx_ref, o_ref,
                 *, n_lat_blocks, seq):
    n = pl.program_id(1)
    W = CHUNK
    qi = lax.broadcasted_iota(jnp.int32, (W, 3 * W), 0)
    kj = lax.broadcasted_iota(jnp.int32, (W, 3 * W), 1)
    rel = kj - W - qi
    kpos = n * W - W + kj
    mask = (jnp.abs(rel) <= W) & (kpos >= 0) & (kpos < seq) & (n < n_lat_blocks)
    keys = jnp.concatenate([kp_ref[...], kc_ref[...], kn_ref[...], kx_ref[...]], axis=0)
    vals = jnp.concatenate([vp_ref[...], vc_ref[...], vn_ref[...], vx_ref[...]], axis=0)
    lane_head = lax.broadcasted_iota(jnp.int32, (1, LANES), 1) // HEAD_DIM
    zero_b = jnp.zeros((), BF16)
    for r in range(GQA_GROUP):
        slab = q_ref[:, r * LANES:(r + 1) * LANES]
        out = jnp.zeros((W, LANES), F32)
        for g in range(ATT_KV_HEADS):
            mg = lane_head == g
            sc = _dot_nt(jnp.where(mg, slab, zero_b), keys)
            s_loc = jnp.where(mask, sc[:, 0:3 * W], NEG_INF)
            s_ctx = sc[:, 3 * W:]
            sink = sink_ref[g * GQA_GROUP + r]
            m = jnp.maximum(jnp.maximum(jnp.max(s_loc, axis=-1, keepdims=True),
                                        jnp.max(s_ctx, axis=-1, keepdims=True)), sink)
            p_loc = jnp.exp(s_loc - m)
            p_ctx = jnp.exp(s_ctx - m)
            den = (jnp.sum(p_loc, axis=-1, keepdims=True) + jnp.sum(p_ctx, axis=-1, keepdims=True)
                   + jnp.exp(sink - m))
            p = jnp.concatenate([p_loc, p_ctx], axis=1).astype(BF16)
            o = _dot(p, vals) * (1.0 / den)
            out = out + jnp.where(mg, o, 0.0)
        o_ref[:, r * LANES:(r + 1) * LANES] = out.astype(BF16)


def _attention(sink, aq, ak, av, *, n_batch, seq, ctx_len):
    n_tok = aq.shape[0]
    nl = seq // CHUNK
    nc = ctx_len // CHUNK
    ctx0 = n_batch * nl

    def q_map(b, n):
        return (jnp.where(n < nl, b * nl + n, ctx0 + b * nc + (n - nl)), 0)

    def k_map(off):
        def f(b, n):
            return (b * nl + jnp.clip(n + off, 0, nl - 1), 0)
        return f

    x_map = lambda b, n: (n_batch * seq // ctx_len + b, 0)
    kv = lambda off: pl.BlockSpec((CHUNK, ATT_KVW), k_map(off))
    xspec = pl.BlockSpec((ctx_len, ATT_KVW), x_map)
    return pl.pallas_call(
        functools.partial(_attn_kernel, n_lat_blocks=nl, seq=seq),
        grid=(n_batch, nl + nc),
        in_specs=[pl.BlockSpec(memory_space=pltpu.SMEM), pl.BlockSpec((CHUNK, ATT_QW), q_map),
                  kv(-1), kv(0), kv(1), kv(-1), kv(0), kv(1), xspec, xspec],
        out_specs=pl.BlockSpec((CHUNK, ATT_QW), q_map),
        out_shape=jax.ShapeDtypeStruct((n_tok, ATT_QW), BF16),
        compiler_params=pltpu.CompilerParams(dimension_semantics=("arbitrary", "arbitrary")),
        name="window_attn",
    )(sink, aq, ak, ak, ak, av, av, av, ak, av)


def _mixout_kernel(x_ref, of_ref, ob_ref, rg_ref, su_ref, sv_ref, ao_ref, mod_ref, wout_ref, wcat_ref, bs_ref,
                   lng_ref, lnb_ref, wq_ref, x1_ref, h2_ref, qp_ref, *, tiles_per_batch, n_batch):
    i = pl.program_id(0)
    grp = jnp.minimum(i // tiles_per_batch, n_batch)
    avg = _group_avg_matrix(RET_W)
    a = _group_ln(of_ref[...] + ob_ref[...], avg) * _silu(rg_ref[...])
    u = _gelu(su_ref[...])
    vn = _group_ln(_gelu(sv_ref[...]), avg)
    hm = _head_mask_stack(SGU_GROUPS, CHUNK, SGU_W)
    zero_b = jnp.zeros((), BF16)
    parts = []
    for c in range(TOK_TILE // CHUNK):
        vc = vn[c * CHUNK:(c + 1) * CHUNK].astype(BF16)
        vs = jnp.where(hm, jnp.concatenate([vc] * SGU_GROUPS, axis=0), zero_b)
        mix = _dot(wcat_ref[...], vs) + bs_ref[...]
        parts.append(u[c * CHUNK:(c + 1) * CHUNK] * mix)
    bmix = jnp.concatenate(parts, axis=0)
    y = (_dot(a.astype(BF16), wout_ref[0:RET_W, :])
         + _dot(bmix.astype(BF16), wout_ref[RET_W:RET_W + SGU_W, :])
         + _dot(ao_ref[...], wout_ref[RET_W + SGU_W:, :]))
    gate1 = mod_ref[pl.ds(grp, 1), 2 * D_MODEL:3 * D_MODEL]
    x1 = _ln(DEEPNORM_ALPHA * x_ref[...] + gate1 * y) * lng_ref[...] + lnb_ref[...]
    x1_ref[...] = x1
    shift2 = mod_ref[pl.ds(grp, 1), 3 * D_MODEL:4 * D_MODEL]
    scale2 = mod_ref[pl.ds(grp, 1), 4 * D_MODEL:5 * D_MODEL]
    h2 = (_ln(x1) * (1.0 + scale2) + shift2).astype(BF16)
    h2_ref[...] = h2
    qp_ref[...] = _dot(h2, wq_ref[...]).astype(BF16)


def _mixout(x_all, o_f, o_b, rg, su, sv, ao, mod_l, w_out, w_cat, b_tab, ln_g, ln_b, wq, *, n_batch, seq):
    n_tok = x_all.shape[0]
    tiles_per_batch = seq // TOK_TILE
    tok = lambda w: pl.BlockSpec((TOK_TILE, w), lambda i: (i, 0))
    full = lambda a: pl.BlockSpec(a.shape, lambda i: (0,) * a.ndim)
    n_q = wq.shape[1]
    return pl.pallas_call(
        functools.partial(_mixout_kernel, tiles_per_batch=tiles_per_batch, n_batch=n_batch),
        grid=(n_tok // TOK_TILE,),
        in_specs=[tok(D_MODEL), tok(RET_W), tok(RET_W), tok(RET_W), tok(SGU_W), tok(SGU_W), tok(ATT_QW),
                  full(mod_l), full(w_out), full(w_cat), full(b_tab), full(ln_g), full(ln_b), full(wq)],
        out_specs=[tok(D_MODEL), tok(D_MODEL), tok(n_q)],
        out_shape=[jax.ShapeDtypeStruct((n_tok, D_MODEL), F32), jax.ShapeDtypeStruct((n_tok, D_MODEL), BF16),
                   jax.ShapeDtypeStruct((n_tok, n_q), BF16)],
        compiler_params=pltpu.CompilerParams(dimension_semantics=("arbitrary",), vmem_limit_bytes=VMEM_LIMIT),
        name="mix_out",
    )(x_all, o_f, o_b, rg, su, sv, ao, mod_l, w_out, w_cat, b_tab, ln_g, ln_b, wq)


_CAND_SEGS = [(a, PEER_TOPK // (a + 1)) for a in range(1, 8)]


def _peer_kernel(h2_ref, qp_ref, x1_ref, mod_ref, k1_ref, k2_ref, u_ref, vt_ref, lng_ref, lnb_ref, out_ref,
                 s_sc, a_sc, b_sc, th_sc, v_sc, hs_sc, acc_sc, *, tiles_per_batch, n_batch):
    t = pl.program_id(0)
    e = pl.program_id(1)
    n_e = pl.num_programs(1)
    NG = PEER_TOK // LANES
    K = PEER_TOPK
    half = PEER_QDIM // 2

    @pl.when(e == 0)
    def _prologue():
        acc_sc[...] = jnp.zeros_like(acc_sc)
        for h in range(PEER_HEADS):
            for p, kref in ((0, k1_ref), (1, k2_ref)):
                c0 = h * PEER_QDIM + p * half
                sT = _dot_nt(kref[...], qp_ref[:, c0:c0 + half])
                for g in range(NG):
                    s_sc[2 * h + p, g] = sT[:, g * LANES:(g + 1) * LANES]

        def extract(it, carry):
            idx = it // NG
            g = it % NG
            cur = s_sc[idx, g]
            for r in range(K):
                m = jnp.max(cur, axis=0, keepdims=True)
                v_sc[idx, g, r:r + 1, :] = m
                cur = jnp.where(cur >= m, NEG_INF, cur)
            return carry

        lax.fori_loop(0, 2 * PEER_HEADS * NG, extract, 0)

        row8 = lax.broadcasted_iota(jnp.int32, (8, LANES), 0)

        def stats(it, carry):
            h = it // NG
            g = it % NG
            v1 = v_sc[2 * h, g]
            v2 = v_sc[2 * h + 1, g]
            segs = [v1[0:1] + v2]
            for a, n_a in _CAND_SEGS:
                segs.append(jnp.where(row8 < n_a, v1[a:a + 1] + v2[0:8], NEG_INF))
            segs.append(v1[8:16] + v2[0:1])
            cand = jnp.concatenate(segs, axis=0)
            cur = cand
            tau = None
            for r in range(K):
                tau = jnp.max(cur, axis=0, keepdims=True)
                cur = jnp.where(cur >= tau, NEG_INF, cur)
            cmax = v1[0:1] + v2[0:1]
            z = jnp.sum(jnp.where(cand >= tau, jnp.exp(cand - cmax), 0.0), axis=0, keepdims=True)
            s1 = s_sc[2 * h, g]
            s2 = s_sc[2 * h + 1, g]
            theta = jnp.full(s1.shape, jnp.inf, F32)
            for bb in range(K):
                vb = v2[bb:bb + 1]
                theta = jnp.where((s1 + vb) >= tau, vb, theta)
            th_sc[h, g] = theta
            a_sc[h, g] = jnp.exp(s1 - v1[0:1]) * (1.0 / z)
            b_sc[h, g] = jnp.exp(s2 - v2[0:1])
            return carry

        lax.fori_loop(0, PEER_HEADS * NG, stats, 0)

    h2 = h2_ref[...]
    rows_per_sub = PEER_SUB // PEER_NKEYS
    for sb in range(PEER_EB // PEER_SUB):
        act = _dot_nt(u_ref[sb * PEER_SUB:(sb + 1) * PEER_SUB, :], h2)
        i0 = e * (PEER_EB // PEER_NKEYS) + sb * rows_per_sub
        for g in range(NG):
            rows = []
            for r in range(rows_per_sub):
                i = i0 + r
                gate = jnp.zeros((PEER_NKEYS, LANES), F32)
                for h in range(PEER_HEADS):
                    th_row = th_sc[h, g, pl.ds(i, 1), :]
                    a_row = a_sc[h, g, pl.ds(i, 1), :]
                    gate = gate + jnp.where(s_sc[2 * h + 1, g] >= th_row, a_row * b_sc[h, g], 0.0)
                rows.append(gate)
            gmat = jnp.concatenate(rows, axis=0)
            hs_sc[:, g * LANES:(g + 1) * LANES] = (gmat * _gelu(act[:, g * LANES:(g + 1) * LANES])).astype(BF16)
        acc_sc[...] += _dot(vt_ref[sb], hs_sc[...])

    @pl.when(e == n_e - 1)
    def _epilogue():
        grp = jnp.minimum(t // tiles_per_batch, n_batch)
        gate2 = mod_ref[pl.ds(grp, 1), 5 * D_MODEL:6 * D_MODEL]
        f = acc_sc[...].T
        out_ref[...] = _ln(DEEPNORM_ALPHA * x1_ref[...] + gate2 * f) * lng_ref[...] + lnb_ref[...]


def _peer(h2, qp, x1, mod_l, k1, k2, u_bf, vt3, ln_g, ln_b, *, n_batch, seq, n_tiles):
    tiles_per_batch = seq // PEER_TOK
    n_e = PEER_N // PEER_EB
    NG = PEER_TOK // LANES
    tok = lambda w: pl.BlockSpec((PEER_TOK, w), lambda t, e: (t, 0))
    full = lambda a: pl.BlockSpec(a.shape, lambda t, e: (0,) * a.ndim)
    stat = lambda n: pltpu.VMEM((n, NG, PEER_NKEYS, LANES), F32)
    return pl.pallas_call(
        functools.partial(_peer_kernel, tiles_per_batch=tiles_per_batch, n_batch=n_batch),
        grid=(n_tiles, n_e),
        in_specs=[tok(D_MODEL), tok(qp.shape[1]), tok(D_MODEL), full(mod_l), full(k1), full(k2),
                  pl.BlockSpec((PEER_EB, D_MODEL), lambda t, e: (e, 0)),
                  pl.BlockSpec((PEER_EB // PEER_SUB, D_MODEL, PEER_SUB), lambda t, e: (e, 0, 0)),
                  full(ln_g), full(ln_b)],
        out_specs=tok(D_MODEL),
        out_shape=jax.ShapeDtypeStruct((n_tiles * PEER_TOK, D_MODEL), F32),
        scratch_shapes=[stat(2 * PEER_HEADS), stat(PEER_HEADS), stat(PEER_HEADS), stat(PEER_HEADS),
                        pltpu.VMEM((2 * PEER_HEADS, NG, PEER_TOPK, LANES), F32),
                        pltpu.VMEM((PEER_SUB, PEER_TOK), BF16),
                        pltpu.VMEM((D_MODEL, PEER_TOK), F32)],
        compiler_params=pltpu.CompilerParams(dimension_semantics=("arbitrary", "arbitrary"),
                                             vmem_limit_bytes=VMEM_LIMIT),
        name="peer",
    )(h2, qp, x1, mod_l, k1, k2, u_bf, vt3, ln_g, ln_b)


def _rope_tables(seq, ctx_len):
    rows = seq // GRID_W
    row_id = jnp.repeat(jnp.arange(rows), GRID_W).astype(F32)
    col_id = jnp.tile(jnp.arange(GRID_W), rows).astype(F32)
    inv = jnp.power(ROPE_BASE, -jnp.arange(ROPE_PAIRS, dtype=F32) / ROPE_PAIRS)
    ang_r = row_id[:, None] * inv
    ang_c = col_id[:, None] * inv
    cos64 = jnp.concatenate([jnp.cos(ang_r)] * 2 + [jnp.cos(ang_c)] * 2, axis=-1)
    sin64 = jnp.concatenate([jnp.sin(ang_r)] * 2 + [jnp.sin(ang_c)] * 2, axis=-1)
    n_rep = RET_W // HEAD_DIM
    cos_tab = jnp.concatenate([jnp.tile(cos64, (1, n_rep)), jnp.ones((ctx_len, RET_W), F32)], axis=0)
    sin_tab = jnp.concatenate([jnp.tile(sin64, (1, n_rep)), jnp.zeros((ctx_len, RET_W), F32)], axis=0)
    return cos_tab, sin_tab


def _rot_partner(width):
    l = np.arange(width)
    lo = (l % 32) < 16
    partner = np.where(lo, l + 16, l - 16)
    sign = np.where(lo, -1.0, 1.0).astype(np.float32)
    return partner, sign


def _slab_perm():
    new = np.arange(ATT_QW)
    r, rem = new // LANES, new % LANES
    g, d = rem // HEAD_DIM, rem % HEAD_DIM
    return (g * GQA_GROUP + r) * HEAD_DIM + d


def _prep_w_in(w_in):
    o_rq, o_rk, o_aq, o_ak = 0, RET_W, 4 * RET_W + 2 * SGU_W, 4 * RET_W + 2 * SGU_W + ATT_QW
    slab = _slab_perm()
    aq = w_in[..., o_aq:o_aq + ATT_QW]

    def partner(block):
        p, sg = _rot_partner(block.shape[-1])
        return block[..., p] * sg

    base = jnp.concatenate([w_in[..., :o_aq], aq[..., slab], w_in[..., o_ak:]], axis=-1)
    rot = jnp.concatenate([partner(w_in[..., o_rq:o_rq + RET_W]), partner(w_in[..., o_rk:o_rk + RET_W]),
                           partner(aq)[..., slab], partner(w_in[..., o_ak:o_ak + ATT_KVW])], axis=-1)
    return jnp.concatenate([base, rot], axis=-1).astype(BF16)


def kernel(x, c, ctx, c_ctx, w_mod, b_mod, w_in, w_out, ret_decay_fwd, ret_decay_bwd, sgu_w, sgu_b, attn_sink,
           ln_mix_g, ln_mix_b, peer_wq, peer_k1, peer_k2, peer_u, peer_v, ln_ffn_g, ln_ffn_b):
    n_batch, seq, _ = x.shape
    ctx_len = ctx.shape[1]
    depth = w_in.shape[0]
    assert seq % PEER_TOK == 0 and ctx_len == TOK_TILE and (n_batch * ctx_len) % PEER_TOK == 0
    assert n_batch + 1 <= 8
    n_lat = n_batch * seq
    kw = dict(n_batch=n_batch, seq=seq)

    x_all = jnp.concatenate([x.reshape(n_lat, D_MODEL), ctx.reshape(n_batch * ctx_len, D_MODEL)], axis=0)
    cond = jnp.concatenate([c, c_ctx[None, :], jnp.zeros((8 - n_batch - 1, D_MODEL), F32)], axis=0)
    mods = _modulations(cond, w_mod, b_mod)

    cos_tab, sin_tab = _rope_tables(seq, ctx_len)
    w_ext = _prep_w_in(w_in)
    slab = _slab_perm()
    w_out_p = jnp.concatenate([w_out[:, :RET_W + SGU_W], w_out[:, RET_W + SGU_W:][:, slab]], axis=1).astype(BF16)
    w_cat = jnp.transpose(sgu_w, (0, 2, 1, 3)).reshape(depth, CHUNK, SGU_GROUPS * CHUNK).astype(BF16)
    b_tab = jnp.repeat(jnp.transpose(sgu_b, (0, 2, 1)), HEAD_DIM, axis=2)
    decays = jnp.stack([ret_decay_fwd, ret_decay_bwd], axis=1)
    wq = peer_wq.astype(BF16)
    k1 = peer_k1.astype(BF16)
    k2 = peer_k2.astype(BF16)
    u_bf = peer_u.astype(BF16)
    vt3 = jnp.transpose(peer_v.reshape(depth, PEER_N // PEER_SUB, PEER_SUB, D_MODEL), (0, 1, 3, 2)).astype(BF16)
    row = lambda a: a.reshape(depth, 1, D_MODEL)
    lmg, lmb, lfg, lfb = row(ln_mix_g), row(ln_mix_b), row(ln_ffn_g), row(ln_ffn_b)

    for l in range(depth):
        last = l == depth - 1
        rq, rk, rv, rg, su, sv, aq, ak, av = _inproj(x_all, mods[l], w_ext[l], cos_tab, sin_tab, **kw)
        o_f, o_b = _retention(decays[l], rq, rk, rv, ctx_len=ctx_len, **kw)
        ao = _attention(attn_sink[l], aq, ak, av, ctx_len=ctx_len, **kw)
        x1, h2, qp = _mixout(x_all, o_f, o_b, rg, su, sv, ao, mods[l], w_out_p[l], w_cat[l], b_tab[l],
                             lmg[l], lmb[l], wq[l], **kw)
        n_tiles = (n_lat if last else x_all.shape[0]) // PEER_TOK
        x_all = _peer(h2, qp, x1, mods[l], k1[l], k2[l], u_bf[l], vt3[l], lfg[l], lfb[l], n_tiles=n_tiles, **kw)
    return x_all[:n_lat].reshape(n_batch, seq, D_MODEL)
```

```python
import functools

import numpy as np
import jax
import jax.numpy as jnp
from jax import lax
from jax.experimental import pallas as pl
from jax.experimental.pallas import tpu as pltpu

F32 = jnp.float32
BF16 = jnp.bfloat16

D_MODEL = 1024
DEPTH = 4
GRID_W = 64
HEAD_DIM = 64
ROPE_PAIRS = HEAD_DIM // 4
ROPE_BASE = 10000.0
RET_HEADS = 4
CHUNK = 128
RET_W = RET_HEADS * HEAD_DIM
RET_SCALE = HEAD_DIM ** -0.5
SGU_GROUPS = 4
SGU_W = SGU_GROUPS * HEAD_DIM
ATT_Q_HEADS = 8
ATT_KV_HEADS = 2
GQA_GROUP = ATT_Q_HEADS // ATT_KV_HEADS
ATT_QW = ATT_Q_HEADS * HEAD_DIM
ATT_KVW = ATT_KV_HEADS * HEAD_DIM
ATT_SCALE = HEAD_DIM ** -0.5
D_IN = 4 * RET_W + 2 * SGU_W + ATT_QW + 2 * ATT_KVW
D_ROT = 2 * RET_W + ATT_QW + ATT_KVW
PEER_HEADS = 8
PEER_NKEYS = 128
PEER_N = PEER_NKEYS * PEER_NKEYS
PEER_QDIM = 256
PEER_TOPK = 16
LN_EPS = 1e-5
DEEPNORM_ALPHA = (2 * DEPTH) ** 0.25

LANES = 128
TOK_TILE = 256
PEER_TOK = 512
PEER_EB = 512
VMEM_LIMIT = 56 * 1024 * 1024

NEG_INF = float("-inf")


def _ln(x):
    mu = jnp.mean(x, axis=-1, keepdims=True)
    xc = x - mu
    var = jnp.mean(xc * xc, axis=-1, keepdims=True)
    return xc * lax.rsqrt(var + LN_EPS)


def _gelu(x):
    h = 0.5 * x
    return h + h * jnp.tanh(x * (0.7978845608028654 + (0.7978845608028654 * 0.044715) * (x * x)))


def _silu(x):
    return x * (1.0 / (1.0 + jnp.exp(-x)))


def _dot(a, b):
    return jnp.dot(a, b, preferred_element_type=F32)


def _dot_nt(a, b):
    return lax.dot_general(a, b, (((1,), (1,)), ((), ())), preferred_element_type=F32)


def _dot_tn(a, b):
    return lax.dot_general(a, b, (((0,), (0,)), ((), ())), preferred_element_type=F32)


def _group_mean(z, avg):
    hi = z.astype(BF16)
    lo = (z - hi.astype(F32)).astype(BF16)
    return _dot(hi, avg) + _dot(lo, avg)


def _group_ln(x, avg):
    mu = _group_mean(x, avg)
    xc = x - mu
    var = _group_mean(xc * xc, avg)
    return xc * lax.rsqrt(var + LN_EPS)


def _group_avg_matrix(width):
    r = lax.broadcasted_iota(jnp.int32, (width, width), 0) // HEAD_DIM
    c = lax.broadcasted_iota(jnp.int32, (width, width), 1) // HEAD_DIM
    return jnp.where(r == c, 1.0 / HEAD_DIM, 0.0).astype(BF16)


def _head_mask_stack(n_heads, rows, width):
    r = lax.broadcasted_iota(jnp.int32, (n_heads * rows, width), 0) // rows
    c = lax.broadcasted_iota(jnp.int32, (n_heads * rows, width), 1) // HEAD_DIM
    return r == c


def _mod_kernel(c_ref, w_ref, b_ref, o_ref):
    s = _silu(c_ref[...])
    hi = s.astype(BF16)
    lo = (s - hi.astype(F32)).astype(BF16)
    w = w_ref[0]
    whi = w.astype(BF16)
    wlo = (w - whi.astype(F32)).astype(BF16)
    o_ref[0] = _dot(hi, whi) + _dot(lo, whi) + _dot(hi, wlo) + b_ref[0]


def _modulations(cond_rows, w_mod, b_mod):
    depth = w_mod.shape[0]
    n_rows = cond_rows.shape[0]
    col = 1024
    n_col = w_mod.shape[2] // col
    return pl.pallas_call(
        _mod_kernel,
        grid=(depth, n_col),
        in_specs=[
            pl.BlockSpec((n_rows, D_MODEL), lambda l, j: (0, 0)),
            pl.BlockSpec((1, D_MODEL, col), lambda l, j: (l, 0, j)),
            pl.BlockSpec((1, 1, col), lambda l, j: (l, 0, j)),
        ],
        out_specs=pl.BlockSpec((1, n_rows, col), lambda l, j: (l, 0, j)),
        out_shape=jax.ShapeDtypeStruct((depth, n_rows, w_mod.shape[2]), F32),
        name="adaln_mod",
    )(cond_rows, w_mod, b_mod.reshape(depth, 1, -1))


def _inproj_kernel(x_ref, mod_ref, w_ref, cos_ref, sin_ref,
                   rq_ref, rk_ref, rv_ref, rg_ref, su_ref, sv_ref, aq_ref, ak_ref, av_ref,
                   *, tiles_per_batch, n_batch):
    i = pl.program_id(0)
    grp = jnp.minimum(i // tiles_per_batch, n_batch)
    shift = mod_ref[pl.ds(grp, 1), 0:D_MODEL]
    scale = mod_ref[pl.ds(grp, 1), D_MODEL:2 * D_MODEL]
    h = (_ln(x_ref[...]) * (1.0 + scale) + shift).astype(BF16)
    cos = cos_ref[...]
    sin = sin_ref[...]

    def proj(c0, w):
        return _dot(h, w_ref[:, c0:c0 + w])

    rot0 = D_IN
    rq_ref[...] = (proj(0, RET_W) * cos + proj(rot0, RET_W) * sin).astype(BF16)
    rk_ref[...] = ((proj(RET_W, RET_W) * cos + proj(rot0 + RET_W, RET_W) * sin) * RET_SCALE).astype(BF16)
    rv_ref[...] = proj(2 * RET_W, RET_W).astype(BF16)
    rg_ref[...] = proj(3 * RET_W, RET_W)
    su_ref[...] = proj(4 * RET_W, SGU_W)
    sv_ref[...] = proj(4 * RET_W + SGU_W, SGU_W)
    aq0 = 4 * RET_W + 2 * SGU_W
    aqr = rot0 + 2 * RET_W
    for half in range(ATT_QW // RET_W):
        o = half * RET_W
        aq_ref[:, o:o + RET_W] = ((proj(aq0 + o, RET_W) * cos + proj(aqr + o, RET_W) * sin) * ATT_SCALE).astype(BF16)
    ak0 = aq0 + ATT_QW
    akr = aqr + ATT_QW
    ak_ref[...] = (proj(ak0, ATT_KVW) * cos[:, 0:ATT_KVW] + proj(akr, ATT_KVW) * sin[:, 0:ATT_KVW]).astype(BF16)
    av_ref[...] = proj(ak0 + ATT_KVW, ATT_KVW).astype(BF16)


def _inproj(x_all, mod_l, w_ext, cos_tab, sin_tab, *, n_batch, seq):
    n_tok = x_all.shape[0]
    n_tiles = n_tok // TOK_TILE
    tiles_per_batch = seq // TOK_TILE
    n_lat_tiles = n_batch * tiles_per_batch

    def tab_map(i):
        return (jnp.where(i < n_lat_tiles, i % tiles_per_batch, tiles_per_batch), 0)

    tok = lambda w: pl.BlockSpec((TOK_TILE, w), lambda i: (i, 0))
    full = lambda a: pl.BlockSpec(a.shape, lambda i: (0,) * a.ndim)
    out_w = [(RET_W, BF16), (RET_W, BF16), (RET_W, BF16), (RET_W, F32), (SGU_W, F32), (SGU_W, F32),
             (ATT_QW, BF16), (ATT_KVW, BF16), (ATT_KVW, BF16)]
    return pl.pallas_call(
        functools.partial(_inproj_kernel, tiles_per_batch=tiles_per_batch, n_batch=n_batch),
        grid=(n_tiles,),
        in_specs=[tok(D_MODEL), full(mod_l), full(w_ext),
                  pl.BlockSpec((TOK_TILE, RET_W), tab_map), pl.BlockSpec((TOK_TILE, RET_W), tab_map)],
        out_specs=[tok(w) for w, _ in out_w],
        out_shape=[jax.ShapeDtypeStruct((n_tok, w), dt) for w, dt in out_w],
        compiler_params=pltpu.CompilerParams(dimension_semantics=("arbitrary",), vmem_limit_bytes=VMEM_LIMIT),
        name="inproj",
    )(x_all, mod_l, w_ext, cos_tab, sin_tab)


def _ret_kernel(dec_ref, qf_ref, kf_ref, vf_ref, qb_ref, kb_ref, vb_ref, of_ref, ob_ref,
                sf_ref, sb_ref, intra_ref, qd_ref, kd_ref, cd_ref):
    b = pl.program_id(0)
    s = pl.program_id(1)
    C = CHUNK
    W = RET_W

    @pl.when((b == 0) & (s == 0))
    def _tables():
        dec = dec_ref[...]
        lg = jnp.minimum(dec, 0.0) - jnp.log(1.0 + jnp.exp(-jnp.abs(dec)))
        lane_head = lax.broadcasted_iota(jnp.int32, (1, W), 1) // HEAD_DIM
        ii = lax.broadcasted_iota(jnp.int32, (C, C), 0)
        jj = lax.broadcasted_iota(jnp.int32, (C, C), 1)
        ri = lax.broadcasted_iota(jnp.int32, (C, W), 0).astype(F32)
        rb = lax.broadcasted_iota(jnp.int32, (W, W), 0) // HEAD_DIM
        cb = lax.broadcasted_iota(jnp.int32, (W, W), 1) // HEAD_DIM
        for d in range(2):
            lgl = jnp.zeros((1, W), F32)
            for hh in range(RET_HEADS):
                lg_h = lg[d:d + 1, hh:hh + 1]
                lgl = lgl + jnp.where(lane_head == hh, lg_h, 0.0)
                rel = (ii - jj) if d == 0 else (jj - ii)
                m = jnp.exp(jnp.maximum(rel, 0).astype(F32) * lg_h)
                intra_ref[d, hh * C:(hh + 1) * C, :] = jnp.where(rel >= 0, m, 0.0)
            if d == 0:
                qd_ref[d] = jnp.exp((ri + 1.0) * lgl)
                kd_ref[d] = jnp.exp((C - 1.0 - ri) * lgl)
            else:
                qd_ref[d] = jnp.exp((C - ri) * lgl)
                kd_ref[d] = jnp.exp(ri * lgl)
            cd_ref[d] = jnp.where(rb == cb, jnp.exp(C * lgl), 0.0)

    @pl.when(s == 0)
    def _zero():
        sf_ref[...] = jnp.zeros_like(sf_ref)
        sb_ref[...] = jnp.zeros_like(sb_ref)

    hm = _head_mask_stack(RET_HEADS, C, W)
    rb = lax.broadcasted_iota(jnp.int32, (W, W), 0) // HEAD_DIM
    cb = lax.broadcasted_iota(jnp.int32, (W, W), 1) // HEAD_DIM
    bd = rb == cb
    zero_b = jnp.zeros((), BF16)

    def direction(d, q_ref, k_ref, v_ref, st_ref, o_ref):
        q = q_ref[...]
        k = k_ref[...]
        v = v_ref[...]
        qs = jnp.where(hm, jnp.concatenate([q] * RET_HEADS, axis=0), zero_b)
        att = (_dot_nt(qs, k) * intra_ref[d]).astype(BF16)
        att = jnp.concatenate([att[hh * C:(hh + 1) * C] for hh in range(RET_HEADS)], axis=1)
        vs = jnp.where(hm, jnp.concatenate([v] * RET_HEADS, axis=0), zero_b)
        st = st_ref[...]
        o = _dot(att, vs) + _dot(q, st.astype(BF16)) * qd_ref[d]
        o_ref[...] = o
        kdec = (k.astype(F32) * kd_ref[d]).astype(BF16)
        st_ref[...] = st * cd_ref[d] + jnp.where(bd, _dot_tn(kdec, v), 0.0)

    direction(0, qf_ref, kf_ref, vf_ref, sf_ref, of_ref)
    direction(1, qb_ref, kb_ref, vb_ref, sb_ref, ob_ref)


def _retention(decays, rq, rk, rv, *, n_batch, seq, ctx_len):
    n_tok = rq.shape[0]
    nl = seq // CHUNK
    nc = ctx_len // CHUNK
    ctx0 = n_batch * nl

    def fwd_map(b, s):
        return (jnp.where(s < nc, ctx0 + b * nc + s, b * nl + (s - nc)), 0)

    def bwd_map(b, s):
        return (jnp.where(s < nc, ctx0 + b * nc + (nc - 1 - s), b * nl + (nl - 1 - (s - nc))), 0)

    fspec = pl.BlockSpec((CHUNK, RET_W), fwd_map)
    bspec = pl.BlockSpec((CHUNK, RET_W), bwd_map)
    return pl.pallas_call(
        _ret_kernel,
        grid=(n_batch, nc + nl),
        in_specs=[pl.BlockSpec(decays.shape, lambda b, s: (0, 0)), fspec, fspec, fspec, bspec, bspec, bspec],
        out_specs=[fspec, bspec],
        out_shape=[jax.ShapeDtypeStruct((n_tok, RET_W), F32)] * 2,
        scratch_shapes=[
            pltpu.VMEM((RET_W, RET_W), F32), pltpu.VMEM((RET_W, RET_W), F32),
            pltpu.VMEM((2, RET_HEADS * CHUNK, CHUNK), F32),
            pltpu.VMEM((2, CHUNK, RET_W), F32), pltpu.VMEM((2, CHUNK, RET_W), F32),
            pltpu.VMEM((2, RET_W, RET_W), F32),
        ],
        compiler_params=pltpu.CompilerParams(dimension_semantics=("arbitrary", "arbitrary")),
        name="retention",
    )(decays, rq, rk, rv, rq, rk, rv)


def _attn_kernel(sink_ref, q_ref, kp_ref, kc_ref, kn_ref, vp_ref, vc_ref, vn_ref, kx_ref, vx_ref, o_ref,
                 *, n_lat_blocks, seq):
    n = pl.program_id(1)
    W = CHUNK
    qi = lax.broadcasted_iota(jnp.int32, (W, 3 * W), 0)
    kj = lax.broadcasted_iota(jnp.int32, (W, 3 * W), 1)
    rel = kj - W - qi
    kpos = n * W - W + kj
    mask = (jnp.abs(rel) <= W) & (kpos >= 0) & (kpos < seq) & (n < n_lat_blocks)
    keys = jnp.concatenate([kp_ref[...], kc_ref[...], kn_ref[...], kx_ref[...]], axis=0)
    vals = jnp.concatenate([vp_ref[...], vc_ref[...], vn_ref[...], vx_ref[...]], axis=0)
    lane_head = lax.broadcasted_iota(jnp.int32, (1, LANES), 1) // HEAD_DIM
    zero_b = jnp.zeros((), BF16)
    for r in range(GQA_GROUP):
        slab = q_ref[:, r * LANES:(r + 1) * LANES]
        out = jnp.zeros((W, LANES), F32)
        for g in range(ATT_KV_HEADS):
            mg = lane_head == g
            sc = _dot_nt(jnp.where(mg, slab, zero_b), keys)
            s_loc = jnp.where(mask, sc[:, 0:3 * W], NEG_INF)
            s_ctx = sc[:, 3 * W:]
            sink = sink_ref[g * GQA_GROUP + r]
            m = jnp.maximum(jnp.maximum(jnp.max(s_loc, axis=-1, keepdims=True),
                                        jnp.max(s_ctx, axis=-1, keepdims=True)), sink)
            p_loc = jnp.exp(s_loc - m)
            p_ctx = jnp.exp(s_ctx - m)
            den = (jnp.sum(p_loc, axis=-1, keepdims=True) + jnp.sum(p_ctx, axis=-1, keepdims=True)
                   + jnp.exp(sink - m))
            p = jnp.concatenate([p_loc, p_ctx], axis=1).astype(BF16)
            o = _dot(p, vals) * (1.0 / den)
            out = out + jnp.where(mg, o, 0.0)
        o_ref[:, r * LANES:(r + 1) * LANES] = out.astype(BF16)


def _attention(sink, aq, ak, av, *, n_batch, seq, ctx_len):
    n_tok = aq.shape[0]
    nl = seq // CHUNK
    nc = ctx_len // CHUNK
    ctx0 = n_batch * nl

    def q_map(b, n):
        return (jnp.where(n < nl, b * nl + n, ctx0 + b * nc + (n - nl)), 0)

    def k_map(off):
        def f(b, n):
            return (b * nl + jnp.clip(n + off, 0, nl - 1), 0)
        return f

    x_map = lambda b, n: (n_batch * seq // ctx_len + b, 0)
    kv = lambda off: pl.BlockSpec((CHUNK, ATT_KVW), k_map(off))
    xspec = pl.BlockSpec((ctx_len, ATT_KVW), x_map)
    return pl.pallas_call(
        functools.partial(_attn_kernel, n_lat_blocks=nl, seq=seq),
        grid=(n_batch, nl + nc),
        in_specs=[pl.BlockSpec(memory_space=pltpu.SMEM), pl.BlockSpec((CHUNK, ATT_QW), q_map),
                  kv(-1), kv(0), kv(1), kv(-1), kv(0), kv(1), xspec, xspec],
        out_specs=pl.BlockSpec((CHUNK, ATT_QW), q_map),
        out_shape=jax.ShapeDtypeStruct((n_tok, ATT_QW), BF16),
        compiler_params=pltpu.CompilerParams(dimension_semantics=("arbitrary", "arbitrary")),
        name="window_attn",
    )(sink, aq, ak, ak, ak, av, av, av, ak, av)


def _mixout_kernel(x_ref, of_ref, ob_ref, rg_ref, su_ref, sv_ref, ao_ref, mod_ref, wout_ref, wcat_ref, bs_ref,
                   lng_ref, lnb_ref, wq_ref, x1_ref, h2_ref, qp_ref, *, tiles_per_batch, n_batch):
    i = pl.program_id(0)
    grp = jnp.minimum(i // tiles_per_batch, n_batch)
    avg = _group_avg_matrix(RET_W)
    a = _group_ln(of_ref[...] + ob_ref[...], avg) * _silu(rg_ref[...])
    u = _gelu(su_ref[...])
    vn = _group_ln(_gelu(sv_ref[...]), avg)
    hm = _head_mask_stack(SGU_GROUPS, CHUNK, SGU_W)
    zero_b = jnp.zeros((), BF16)
    parts = []
    for c in range(TOK_TILE // CHUNK):
        vc = vn[c * CHUNK:(c + 1) * CHUNK].astype(BF16)
        vs = jnp.where(hm, jnp.concatenate([vc] * SGU_GROUPS, axis=0), zero_b)
        mix = _dot(wcat_ref[...], vs) + bs_ref[...]
        parts.append(u[c * CHUNK:(c + 1) * CHUNK] * mix)
    bmix = jnp.concatenate(parts, axis=0)
    y = (_dot(a.astype(BF16), wout_ref[0:RET_W, :])
         + _dot(bmix.astype(BF16), wout_ref[RET_W:RET_W + SGU_W, :])
         + _dot(ao_ref[...], wout_ref[RET_W + SGU_W:, :]))
    gate1 = mod_ref[pl.ds(grp, 1), 2 * D_MODEL:3 * D_MODEL]
    x1 = _ln(DEEPNORM_ALPHA * x_ref[...] + gate1 * y) * lng_ref[...] + lnb_ref[...]
    x1_ref[...] = x1
    shift2 = mod_ref[pl.ds(grp, 1), 3 * D_MODEL:4 * D_MODEL]
    scale2 = mod_ref[pl.ds(grp, 1), 4 * D_MODEL:5 * D_MODEL]
    h2 = (_ln(x1) * (1.0 + scale2) + shift2).astype(BF16)
    h2_ref[...] = h2
    qp_ref[...] = _dot(h2, wq_ref[...]).astype(BF16)


def _mixout(x_all, o_f, o_b, rg, su, sv, ao, mod_l, w_out, w_cat, b_tab, ln_g, ln_b, wq, *, n_batch, seq):
    n_tok = x_all.shape[0]
    tiles_per_batch = seq // TOK_TILE
    tok = lambda w: pl.BlockSpec((TOK_TILE, w), lambda i: (i, 0))
    full = lambda a: pl.BlockSpec(a.shape, lambda i: (0,) * a.ndim)
    n_q = wq.shape[1]
    return pl.pallas_call(
        functools.partial(_mixout_kernel, tiles_per_batch=tiles_per_batch, n_batch=n_batch),
        grid=(n_tok // TOK_TILE,),
        in_specs=[tok(D_MODEL), tok(RET_W), tok(RET_W), tok(RET_W), tok(SGU_W), tok(SGU_W), tok(ATT_QW),
                  full(mod_l), full(w_out), full(w_cat), full(b_tab), full(ln_g), full(ln_b), full(wq)],
        out_specs=[tok(D_MODEL), tok(D_MODEL), tok(n_q)],
        out_shape=[jax.ShapeDtypeStruct((n_tok, D_MODEL), F32), jax.ShapeDtypeStruct((n_tok, D_MODEL), BF16),
                   jax.ShapeDtypeStruct((n_tok, n_q), BF16)],
        compiler_params=pltpu.CompilerParams(dimension_semantics=("arbitrary",), vmem_limit_bytes=VMEM_LIMIT),
        name="mix_out",
    )(x_all, o_f, o_b, rg, su, sv, ao, mod_l, w_out, w_cat, b_tab, ln_g, ln_b, wq)


_CAND_SEGS = [(a, PEER_TOPK // (a + 1)) for a in range(1, 8)]


def _dup_bf16_words(x):
    hi = pltpu.bitcast(x.astype(BF16).astype(F32), jnp.uint32)
    return hi | (hi >> 16)


def _peer_kernel(h2_ref, qp_ref, x1_ref, mod_ref, k1_ref, k2_ref, u_ref, vt_ref, lng_ref, lnb_ref, out_ref,
                 s_sc, v_sc, n_sc, a_sc, rk_sc, b_sc, act0_sc, act1_sc, hs0_sc, hs1_sc, acc_sc,
                 *, tiles_per_batch, n_batch, n_blocks):
    t = pl.program_id(0)
    e = pl.program_id(1)
    NG = PEER_TOK // LANES
    K = PEER_TOPK
    half = PEER_QDIM // 2

    @pl.when(e == 0)
    def _prologue():
        acc_sc[...] = jnp.zeros_like(acc_sc)
        act1_sc[...] = jnp.zeros_like(act1_sc)
        hs1_sc[...] = jnp.zeros_like(hs1_sc)
        for h in range(PEER_HEADS):
            for p, kref in ((0, k1_ref), (1, k2_ref)):
                c0 = h * PEER_QDIM + p * half
                sT = _dot_nt(kref[...], qp_ref[:, c0:c0 + half])
                for g in range(NG):
                    s_sc[2 * h + p, g] = sT[:, g * LANES:(g + 1) * LANES]

        row8 = lax.broadcasted_iota(jnp.int32, (8, LANES), 0)

        def stats(it, carry):
            h = it // NG
            g = it % NG
            s1 = s_sc[2 * h, g]
            s2 = s_sc[2 * h + 1, g]
            cur1, cur2 = s1, s2
            rank2 = jnp.full(s2.shape, float(K), F32)
            for r in range(K):
                m1 = jnp.max(cur1, axis=0, keepdims=True)
                m2 = jnp.max(cur2, axis=0, keepdims=True)
                v_sc[0, r:r + 1, :] = m1
                v_sc[1, r:r + 1, :] = m2
                cur1 = jnp.where(cur1 >= m1, NEG_INF, cur1)
                hit2 = cur2 >= m2
                rank2 = jnp.where(hit2, float(r), rank2)
                cur2 = jnp.where(hit2, NEG_INF, cur2)
            v1 = v_sc[0]
            v2 = v_sc[1]
            segs = [v1[0:1] + v2]
            for a, n_a in _CAND_SEGS:
                segs.append(jnp.where(row8 < n_a, v1[a:a + 1] + v2[0:8], NEG_INF))
            segs.append(v1[8:16] + v2[0:1])
            cand = jnp.concatenate(segs, axis=0)
            cur = cand
            tau = None
            for r in range(K):
                tau = jnp.max(cur, axis=0, keepdims=True)
                cur = jnp.where(cur >= tau, NEG_INF, cur)
            cmax = v1[0:1] + v2[0:1]
            z = jnp.sum(jnp.where(cand >= tau, jnp.exp(cand - cmax), 0.0), axis=0, keepdims=True)
            n = jnp.zeros(s1.shape, F32)
            for bb in range(K):
                n = n + jnp.where((s1 + v2[bb:bb + 1]) >= tau, 1.0, 0.0)
            n_sc[h, g] = _dup_bf16_words(n)
            a_sc[h, g] = _dup_bf16_words(jnp.exp(s1 - v1[0:1]) * (1.0 / z))
            rk_sc[h, g] = rank2.astype(BF16)
            b_sc[h, g] = jnp.exp(s2 - v2[0:1]).astype(BF16)
            return carry

        lax.fori_loop(0, PEER_HEADS * NG, stats, 0)

    rows_per_blk = PEER_EB // PEER_NKEYS
    kc_rows = 32
    i_base = jnp.clip(e - 1, 0, n_blocks - 1) * rows_per_blk
    zero_b = jnp.zeros((), BF16)

    def step(act_w, act_r, hs_w, hs_r):
        for g in range(NG):
            if g % 2 == 0:
                tcol = slice(g * LANES, (g + 2) * LANES)
                act_w[:, tcol] = _dot_nt(u_ref[...], h2_ref[tcol, :]).astype(BF16)
            else:
                tcol = slice((g - 1) * LANES, (g + 1) * LANES)
                acc_sc[:, tcol] += _dot(vt_ref[...], hs_r[:, tcol])
            lanes = slice(g * LANES, (g + 1) * LANES)
            for kc in range(PEER_NKEYS // kc_rows):
                keys = slice(kc * kc_rows, (kc + 1) * kc_rows)
                gates = [jnp.zeros((kc_rows, LANES), BF16) for _ in range(rows_per_blk)]
                for h in range(PEER_HEADS):
                    rk = rk_sc[h, g, keys, :]
                    bb = b_sc[h, g, keys, :]
                    for r in range(rows_per_blk):
                        i = i_base + r
                        n_row = pltpu.bitcast(jnp.broadcast_to(n_sc[h, g, pl.ds(i, 1), :], (8, LANES)), BF16)
                        a_row = pltpu.bitcast(jnp.broadcast_to(a_sc[h, g, pl.ds(i, 1), :], (8, LANES)), BF16)
                        n_full = jnp.concatenate([n_row] * (kc_rows // 16), axis=0)
                        a_full = jnp.concatenate([a_row] * (kc_rows // 16), axis=0)
                        gates[r] = gates[r] + jnp.where(rk < n_full, a_full * bb, zero_b)
                for r in range(rows_per_blk):
                    rows = slice(r * PEER_NKEYS + kc * kc_rows, r * PEER_NKEYS + (kc + 1) * kc_rows)
                    hs_w[rows, lanes] = gates[r] * _gelu(act_r[rows, lanes])

    @pl.when(e % 2 == 0)
    def _even():
        step(act0_sc, act1_sc, hs0_sc, hs1_sc)

    @pl.when(e % 2 == 1)
    def _odd():
        step(act1_sc, act0_sc, hs1_sc, hs0_sc)

    @pl.when(e == n_blocks + 1)
    def _epilogue():
        grp = jnp.minimum(t // tiles_per_batch, n_batch)
        gate2 = mod_ref[pl.ds(grp, 1), 5 * D_MODEL:6 * D_MODEL]
        f = acc_sc[...].T
        out_ref[...] = _ln(DEEPNORM_ALPHA * x1_ref[...] + gate2 * f) * lng_ref[...] + lnb_ref[...]


def _peer(h2, qp, x1, mod_l, k1, k2, u_bf, vt3, ln_g, ln_b, *, n_batch, seq, n_tiles):
    tiles_per_batch = seq // PEER_TOK
    n_e = PEER_N // PEER_EB
    NG = PEER_TOK // LANES
    tok = lambda w: pl.BlockSpec((PEER_TOK, w), lambda t, e: (t, 0))
    full = lambda a: pl.BlockSpec(a.shape, lambda t, e: (0,) * a.ndim)
    stat = lambda n, dt: pltpu.VMEM((n, NG, PEER_NKEYS, LANES), dt)
    return pl.pallas_call(
        functools.partial(_peer_kernel, tiles_per_batch=tiles_per_batch, n_batch=n_batch, n_blocks=n_e),
        grid=(n_tiles, n_e + 2),
        in_specs=[tok(D_MODEL), tok(qp.shape[1]), tok(D_MODEL), full(mod_l), full(k1), full(k2),
                  pl.BlockSpec((PEER_EB, D_MODEL), lambda t, e: (jnp.minimum(e, n_e - 1), 0)),
                  pl.BlockSpec((D_MODEL, PEER_EB), lambda t, e: (0, jnp.clip(e - 2, 0, n_e - 1))),
                  full(ln_g), full(ln_b)],
        out_specs=tok(D_MODEL),
        out_shape=jax.ShapeDtypeStruct((n_tiles * PEER_TOK, D_MODEL), F32),
        scratch_shapes=[stat(2 * PEER_HEADS, F32),
                        pltpu.VMEM((2, PEER_TOPK, LANES), F32),
                        stat(PEER_HEADS, jnp.uint32), stat(PEER_HEADS, jnp.uint32),
                        stat(PEER_HEADS, BF16), stat(PEER_HEADS, BF16),
                        pltpu.VMEM((PEER_EB, PEER_TOK), BF16), pltpu.VMEM((PEER_EB, PEER_TOK), BF16),
                        pltpu.VMEM((PEER_EB, PEER_TOK), BF16), pltpu.VMEM((PEER_EB, PEER_TOK), BF16),
                        pltpu.VMEM((D_MODEL, PEER_TOK), F32)],
        compiler_params=pltpu.CompilerParams(dimension_semantics=("arbitrary", "arbitrary"),
                                             vmem_limit_bytes=VMEM_LIMIT),
        name="peer",
    )(h2, qp, x1, mod_l, k1, k2, u_bf, vt3, ln_g, ln_b)


def _rope_tables(seq, ctx_len):
    rows = seq // GRID_W
    row_id = jnp.repeat(jnp.arange(rows), GRID_W).astype(F32)
    col_id = jnp.tile(jnp.arange(GRID_W), rows).astype(F32)
    inv = jnp.power(ROPE_BASE, -jnp.arange(ROPE_PAIRS, dtype=F32) / ROPE_PAIRS)
    ang_r = row_id[:, None] * inv
    ang_c = col_id[:, None] * inv
    cos64 = jnp.concatenate([jnp.cos(ang_r)] * 2 + [jnp.cos(ang_c)] * 2, axis=-1)
    sin64 = jnp.concatenate([jnp.sin(ang_r)] * 2 + [jnp.sin(ang_c)] * 2, axis=-1)
    n_rep = RET_W // HEAD_DIM
    cos_tab = jnp.concatenate([jnp.tile(cos64, (1, n_rep)), jnp.ones((ctx_len, RET_W), F32)], axis=0)
    sin_tab = jnp.concatenate([jnp.tile(sin64, (1, n_rep)), jnp.zeros((ctx_len, RET_W), F32)], axis=0)
    return cos_tab, sin_tab


def _rot_partner(width):
    l = np.arange(width)
    lo = (l % 32) < 16
    partner = np.where(lo, l + 16, l - 16)
    sign = np.where(lo, -1.0, 1.0).astype(np.float32)
    return partner, sign


def _slab_perm():
    new = np.arange(ATT_QW)
    r, rem = new // LANES, new % LANES
    g, d = rem // HEAD_DIM, rem % HEAD_DIM
    return (g * GQA_GROUP + r) * HEAD_DIM + d


def _prep_w_in(w_in):
    o_rq, o_rk, o_aq, o_ak = 0, RET_W, 4 * RET_W + 2 * SGU_W, 4 * RET_W + 2 * SGU_W + ATT_QW
    slab = _slab_perm()
    aq = w_in[..., o_aq:o_aq + ATT_QW]

    def partner(block):
        p, sg = _rot_partner(block.shape[-1])
        return block[..., p] * sg

    base = jnp.concatenate([w_in[..., :o_aq], aq[..., slab], w_in[..., o_ak:]], axis=-1)
    rot = jnp.concatenate([partner(w_in[..., o_rq:o_rq + RET_W]), partner(w_in[..., o_rk:o_rk + RET_W]),
                           partner(aq)[..., slab], partner(w_in[..., o_ak:o_ak + ATT_KVW])], axis=-1)
    return jnp.concatenate([base, rot], axis=-1).astype(BF16)


def kernel(x, c, ctx, c_ctx, w_mod, b_mod, w_in, w_out, ret_decay_fwd, ret_decay_bwd, sgu_w, sgu_b, attn_sink,
           ln_mix_g, ln_mix_b, peer_wq, peer_k1, peer_k2, peer_u, peer_v, ln_ffn_g, ln_ffn_b):
    n_batch, seq, _ = x.shape
    ctx_len = ctx.shape[1]
    depth = w_in.shape[0]
    assert seq % PEER_TOK == 0 and ctx_len == TOK_TILE and (n_batch * ctx_len) % PEER_TOK == 0
    assert n_batch + 1 <= 8
    n_lat = n_batch * seq
    kw = dict(n_batch=n_batch, seq=seq)

    x_all = jnp.concatenate([x.reshape(n_lat, D_MODEL), ctx.reshape(n_batch * ctx_len, D_MODEL)], axis=0)
    cond = jnp.concatenate([c, c_ctx[None, :], jnp.zeros((8 - n_batch - 1, D_MODEL), F32)], axis=0)
    mods = _modulations(cond, w_mod, b_mod)

    cos_tab, sin_tab = _rope_tables(seq, ctx_len)
    w_ext = _prep_w_in(w_in)
    slab = _slab_perm()
    w_out_p = jnp.concatenate([w_out[:, :RET_W + SGU_W], w_out[:, RET_W + SGU_W:][:, slab]], axis=1).astype(BF16)
    w_cat = jnp.transpose(sgu_w, (0, 2, 1, 3)).reshape(depth, CHUNK, SGU_GROUPS * CHUNK).astype(BF16)
    b_tab = jnp.repeat(jnp.transpose(sgu_b, (0, 2, 1)), HEAD_DIM, axis=2)
    decays = jnp.stack([ret_decay_fwd, ret_decay_bwd], axis=1)
    wq = peer_wq.astype(BF16)
    k1 = peer_k1.astype(BF16)
    k2 = peer_k2.astype(BF16)
    u_bf = peer_u.astype(BF16)
    vt3 = jnp.transpose(peer_v, (0, 2, 1)).astype(BF16)
    row = lambda a: a.reshape(depth, 1, D_MODEL)
    lmg, lmb, lfg, lfb = row(ln_mix_g), row(ln_mix_b), row(ln_ffn_g), row(ln_ffn_b)

    for l in range(depth):
        last = l == depth - 1
        rq, rk, rv, rg, su, sv, aq, ak, av = _inproj(x_all, mods[l], w_ext[l], cos_tab, sin_tab, **kw)
        o_f, o_b = _retention(decays[l], rq, rk, rv, ctx_len=ctx_len, **kw)
        ao = _attention(attn_sink[l], aq, ak, av, ctx_len=ctx_len, **kw)
        x1, h2, qp = _mixout(x_all, o_f, o_b, rg, su, sv, ao, mods[l], w_out_p[l], w_cat[l], b_tab[l],
                             lmg[l], lmb[l], wq[l], **kw)
        n_tiles = (n_lat if last else x_all.shape[0]) // PEER_TOK
        x_all = _peer(h2, qp, x1, mods[l], k1[l], k2[l], u_bf[l], vt3[l], lfg[l], lfb[l], n_tiles=n_tiles, **kw)
    return x_all[:n_lat].reshape(n_batch, seq, D_MODEL)
```

```python
import functools

import numpy as np
import jax
import jax.numpy as jnp
from jax import lax
from jax.experimental import pallas as pl
from jax.experimental.pallas import tpu as pltpu

F32 = jnp.float32
BF16 = jnp.bfloat16

D_MODEL = 1024
DEPTH = 4
GRID_W = 64
HEAD_DIM = 64
ROPE_PAIRS = HEAD_DIM // 4
ROPE_BASE = 10000.0
RET_HEADS = 4
CHUNK = 128
RET_W = RET_HEADS * HEAD_DIM
RET_SCALE = HEAD_DIM ** -0.5
SGU_GROUPS = 4
SGU_W = SGU_GROUPS * HEAD_DIM
ATT_Q_HEADS = 8
ATT_KV_HEADS = 2
GQA_GROUP = ATT_Q_HEADS // ATT_KV_HEADS
ATT_QW = ATT_Q_HEADS * HEAD_DIM
ATT_KVW = ATT_KV_HEADS * HEAD_DIM
ATT_SCALE = HEAD_DIM ** -0.5
D_IN = 4 * RET_W + 2 * SGU_W + ATT_QW + 2 * ATT_KVW
D_ROT = 2 * RET_W + ATT_QW + ATT_KVW
PEER_HEADS = 8
PEER_NKEYS = 128
PEER_N = PEER_NKEYS * PEER_NKEYS
PEER_QDIM = 256
PEER_TOPK = 16
LN_EPS = 1e-5
DEEPNORM_ALPHA = (2 * DEPTH) ** 0.25

LANES = 128
TOK_TILE = 256
PEER_TOK = 512
PEER_EB = 512
VMEM_LIMIT = 56 * 1024 * 1024

NEG_INF = float("-inf")


def _ln(x):
    mu = jnp.mean(x, axis=-1, keepdims=True)
    xc = x - mu
    var = jnp.mean(xc * xc, axis=-1, keepdims=True)
    return xc * lax.rsqrt(var + LN_EPS)


def _gelu(x):
    h = 0.5 * x
    return h + h * jnp.tanh(x * (0.7978845608028654 + (0.7978845608028654 * 0.044715) * (x * x)))


def _silu(x):
    return x * (1.0 / (1.0 + jnp.exp(-x)))


def _dot(a, b):
    return jnp.dot(a, b, preferred_element_type=F32)


def _dot_nt(a, b):
    return lax.dot_general(a, b, (((1,), (1,)), ((), ())), preferred_element_type=F32)


def _dot_tn(a, b):
    return lax.dot_general(a, b, (((0,), (0,)), ((), ())), preferred_element_type=F32)


def _group_mean(z, avg):
    hi = z.astype(BF16)
    lo = (z - hi.astype(F32)).astype(BF16)
    return _dot(hi, avg) + _dot(lo, avg)


def _group_ln(x, avg):
    mu = _group_mean(x, avg)
    xc = x - mu
    var = _group_mean(xc * xc, avg)
    return xc * lax.rsqrt(var + LN_EPS)


def _group_avg_matrix(width):
    r = lax.broadcasted_iota(jnp.int32, (width, width), 0) // HEAD_DIM
    c = lax.broadcasted_iota(jnp.int32, (width, width), 1) // HEAD_DIM
    return jnp.where(r == c, 1.0 / HEAD_DIM, 0.0).astype(BF16)


def _head_mask_stack(n_heads, rows, width):
    r = lax.broadcasted_iota(jnp.int32, (n_heads * rows, width), 0) // rows
    c = lax.broadcasted_iota(jnp.int32, (n_heads * rows, width), 1) // HEAD_DIM
    return r == c


def _mod_kernel(c_ref, w_ref, b_ref, o_ref):
    s = _silu(c_ref[...])
    hi = s.astype(BF16)
    lo = (s - hi.astype(F32)).astype(BF16)
    w = w_ref[0]
    whi = w.astype(BF16)
    wlo = (w - whi.astype(F32)).astype(BF16)
    o_ref[0] = _dot(hi, whi) + _dot(lo, whi) + _dot(hi, wlo) + b_ref[0]


def _modulations(cond_rows, w_mod, b_mod):
    depth = w_mod.shape[0]
    n_rows = cond_rows.shape[0]
    col = 1024
    n_col = w_mod.shape[2] // col
    return pl.pallas_call(
        _mod_kernel,
        grid=(depth, n_col),
        in_specs=[
            pl.BlockSpec((n_rows, D_MODEL), lambda l, j: (0, 0)),
            pl.BlockSpec((1, D_MODEL, col), lambda l, j: (l, 0, j)),
            pl.BlockSpec((1, 1, col), lambda l, j: (l, 0, j)),
        ],
        out_specs=pl.BlockSpec((1, n_rows, col), lambda l, j: (l, 0, j)),
        out_shape=jax.ShapeDtypeStruct((depth, n_rows, w_mod.shape[2]), F32),
        name="adaln_mod",
    )(cond_rows, w_mod, b_mod.reshape(depth, 1, -1))


def _inproj_kernel(x_ref, mod_ref, w_ref, cos_ref, sin_ref,
                   rq_ref, rk_ref, rv_ref, rg_ref, su_ref, sv_ref, aq_ref, ak_ref, av_ref,
                   *, tiles_per_batch, n_batch):
    i = pl.program_id(0)
    grp = jnp.minimum(i // tiles_per_batch, n_batch)
    shift = mod_ref[pl.ds(grp, 1), 0:D_MODEL]
    scale = mod_ref[pl.ds(grp, 1), D_MODEL:2 * D_MODEL]
    h = (_ln(x_ref[...]) * (1.0 + scale) + shift).astype(BF16)
    cos = cos_ref[...]
    sin = sin_ref[...]

    def proj(c0, w):
        return _dot(h, w_ref[:, c0:c0 + w])

    rot0 = D_IN
    rq_ref[...] = (proj(0, RET_W) * cos + proj(rot0, RET_W) * sin).astype(BF16)
    rk_ref[...] = ((proj(RET_W, RET_W) * cos + proj(rot0 + RET_W, RET_W) * sin) * RET_SCALE).astype(BF16)
    rv_ref[...] = proj(2 * RET_W, RET_W).astype(BF16)
    rg_ref[...] = proj(3 * RET_W, RET_W)
    su_ref[...] = proj(4 * RET_W, SGU_W)
    sv_ref[...] = proj(4 * RET_W + SGU_W, SGU_W)
    aq0 = 4 * RET_W + 2 * SGU_W
    aqr = rot0 + 2 * RET_W
    for half in range(ATT_QW // RET_W):
        o = half * RET_W
        aq_ref[:, o:o + RET_W] = ((proj(aq0 + o, RET_W) * cos + proj(aqr + o, RET_W) * sin) * ATT_SCALE).astype(BF16)
    ak0 = aq0 + ATT_QW
    akr = aqr + ATT_QW
    ak_ref[...] = (proj(ak0, ATT_KVW) * cos[:, 0:ATT_KVW] + proj(akr, ATT_KVW) * sin[:, 0:ATT_KVW]).astype(BF16)
    av_ref[...] = proj(ak0 + ATT_KVW, ATT_KVW).astype(BF16)


def _inproj(x_all, mod_l, w_ext, cos_tab, sin_tab, *, n_batch, seq):
    n_tok = x_all.shape[0]
    n_tiles = n_tok // TOK_TILE
    tiles_per_batch = seq // TOK_TILE
    n_lat_tiles = n_batch * tiles_per_batch

    def tab_map(i):
        return (jnp.where(i < n_lat_tiles, i % tiles_per_batch, tiles_per_batch), 0)

    tok = lambda w: pl.BlockSpec((TOK_TILE, w), lambda i: (i, 0))
    full = lambda a: pl.BlockSpec(a.shape, lambda i: (0,) * a.ndim)
    out_w = [(RET_W, BF16), (RET_W, BF16), (RET_W, BF16), (RET_W, F32), (SGU_W, F32), (SGU_W, F32),
             (ATT_QW, BF16), (ATT_KVW, BF16), (ATT_KVW, BF16)]
    return pl.pallas_call(
        functools.partial(_inproj_kernel, tiles_per_batch=tiles_per_batch, n_batch=n_batch),
        grid=(n_tiles,),
        in_specs=[tok(D_MODEL), full(mod_l), full(w_ext),
                  pl.BlockSpec((TOK_TILE, RET_W), tab_map), pl.BlockSpec((TOK_TILE, RET_W), tab_map)],
        out_specs=[tok(w) for w, _ in out_w],
        out_shape=[jax.ShapeDtypeStruct((n_tok, w), dt) for w, dt in out_w],
        compiler_params=pltpu.CompilerParams(dimension_semantics=("arbitrary",), vmem_limit_bytes=VMEM_LIMIT),
        name="inproj",
    )(x_all, mod_l, w_ext, cos_tab, sin_tab)


def _ret_kernel(dec_ref, qf_ref, kf_ref, vf_ref, qb_ref, kb_ref, vb_ref, of_ref, ob_ref,
                sf_ref, sb_ref, intra_ref, qd_ref, kd_ref, cd_ref):
    b = pl.program_id(0)
    s = pl.program_id(1)
    C = CHUNK
    W = RET_W

    @pl.when((b == 0) & (s == 0))
    def _tables():
        dec = dec_ref[...]
        lg = jnp.minimum(dec, 0.0) - jnp.log(1.0 + jnp.exp(-jnp.abs(dec)))
        lane_head = lax.broadcasted_iota(jnp.int32, (1, W), 1) // HEAD_DIM
        ii = lax.broadcasted_iota(jnp.int32, (C, C), 0)
        jj = lax.broadcasted_iota(jnp.int32, (C, C), 1)
        ri = lax.broadcasted_iota(jnp.int32, (C, W), 0).astype(F32)
        rb = lax.broadcasted_iota(jnp.int32, (W, W), 0) // HEAD_DIM
        cb = lax.broadcasted_iota(jnp.int32, (W, W), 1) // HEAD_DIM
        for d in range(2):
            lgl = jnp.zeros((1, W), F32)
            for hh in range(RET_HEADS):
                lg_h = lg[d:d + 1, hh:hh + 1]
                lgl = lgl + jnp.where(lane_head == hh, lg_h, 0.0)
                rel = (ii - jj) if d == 0 else (jj - ii)
                m = jnp.exp(jnp.maximum(rel, 0).astype(F32) * lg_h)
                intra_ref[d, hh * C:(hh + 1) * C, :] = jnp.where(rel >= 0, m, 0.0)
            if d == 0:
                qd_ref[d] = jnp.exp((ri + 1.0) * lgl)
                kd_ref[d] = jnp.exp((C - 1.0 - ri) * lgl)
            else:
                qd_ref[d] = jnp.exp((C - ri) * lgl)
                kd_ref[d] = jnp.exp(ri * lgl)
            cd_ref[d] = jnp.where(rb == cb, jnp.exp(C * lgl), 0.0)

    @pl.when(s == 0)
    def _zero():
        sf_ref[...] = jnp.zeros_like(sf_ref)
        sb_ref[...] = jnp.zeros_like(sb_ref)

    hm = _head_mask_stack(RET_HEADS, C, W)
    rb = lax.broadcasted_iota(jnp.int32, (W, W), 0) // HEAD_DIM
    cb = lax.broadcasted_iota(jnp.int32, (W, W), 1) // HEAD_DIM
    bd = rb == cb
    zero_b = jnp.zeros((), BF16)

    def direction(d, q_ref, k_ref, v_ref, st_ref, o_ref):
        q = q_ref[...]
        k = k_ref[...]
        v = v_ref[...]
        qs = jnp.where(hm, jnp.concatenate([q] * RET_HEADS, axis=0), zero_b)
        att = (_dot_nt(qs, k) * intra_ref[d]).astype(BF16)
        att = jnp.concatenate([att[hh * C:(hh + 1) * C] for hh in range(RET_HEADS)], axis=1)
        vs = jnp.where(hm, jnp.concatenate([v] * RET_HEADS, axis=0), zero_b)
        st = st_ref[...]
        o = _dot(att, vs) + _dot(q, st.astype(BF16)) * qd_ref[d]
        o_ref[...] = o
        kdec = (k.astype(F32) * kd_ref[d]).astype(BF16)
        st_ref[...] = st * cd_ref[d] + jnp.where(bd, _dot_tn(kdec, v), 0.0)

    direction(0, qf_ref, kf_ref, vf_ref, sf_ref, of_ref)
    direction(1, qb_ref, kb_ref, vb_ref, sb_ref, ob_ref)


def _retention(decays, rq, rk, rv, *, n_batch, seq, ctx_len):
    n_tok = rq.shape[0]
    nl = seq // CHUNK
    nc = ctx_len // CHUNK
    ctx0 = n_batch * nl

    def fwd_map(b, s):
        return (jnp.where(s < nc, ctx0 + b * nc + s, b * nl + (s - nc)), 0)

    def bwd_map(b, s):
        return (jnp.where(s < nc, ctx0 + b * nc + (nc - 1 - s), b * nl + (nl - 1 - (s - nc))), 0)

    fspec = pl.BlockSpec((CHUNK, RET_W), fwd_map)
    bspec = pl.BlockSpec((CHUNK, RET_W), bwd_map)
    return pl.pallas_call(
        _ret_kernel,
        grid=(n_batch, nc + nl),
        in_specs=[pl.BlockSpec(decays.shape, lambda b, s: (0, 0)), fspec, fspec, fspec, bspec, bspec, bspec],
        out_specs=[fspec, bspec],
        out_shape=[jax.ShapeDtypeStruct((n_tok, RET_W), F32)] * 2,
        scratch_shapes=[
            pltpu.VMEM((RET_W, RET_W), F32), pltpu.VMEM((RET_W, RET_W), F32),
            pltpu.VMEM((2, RET_HEADS * CHUNK, CHUNK), F32),
            pltpu.VMEM((2, CHUNK, RET_W), F32), pltpu.VMEM((2, CHUNK, RET_W), F32),
            pltpu.VMEM((2, RET_W, RET_W), F32),
        ],
        compiler_params=pltpu.CompilerParams(dimension_semantics=("arbitrary", "arbitrary")),
        name="retention",
    )(decays, rq, rk, rv, rq, rk, rv)


def _attn_kernel(sink_ref, q_ref, kp_ref, kc_ref, kn_ref, vp_ref, vc_ref, vn_ref, kx_ref, vx_ref, o_ref,
                 *, n_lat_blocks, seq):
    n = pl.program_id(1)
    W = CHUNK
    qi = lax.broadcasted_iota(jnp.int32, (W, 3 * W), 0)
    kj = lax.broadcasted_iota(jnp.int32, (W, 3 * W), 1)
    rel = kj - W - qi
    kpos = n * W - W + kj
    mask = (jnp.abs(rel) <= W) & (kpos >= 0) & (kpos < seq) & (n < n_lat_blocks)
    keys = jnp.concatenate([kp_ref[...], kc_ref[...], kn_ref[...], kx_ref[...]], axis=0)
    vals = jnp.concatenate([vp_ref[...], vc_ref[...], vn_ref[...], vx_ref[...]], axis=0)
    lane_head = lax.broadcasted_iota(jnp.int32, (1, LANES), 1) // HEAD_DIM
    zero_b = jnp.zeros((), BF16)
    for r in range(GQA_GROUP):
        slab = q_ref[:, r * LANES:(r + 1) * LANES]
        out = jnp.zeros((W, LANES), F32)
        for g in range(ATT_KV_HEADS):
            mg = lane_head == g
            sc = _dot_nt(jnp.where(mg, slab, zero_b), keys)
            s_loc = jnp.where(mask, sc[:, 0:3 * W], NEG_INF)
            s_ctx = sc[:, 3 * W:]
            sink = sink_ref[g * GQA_GROUP + r]
            m = jnp.maximum(jnp.maximum(jnp.max(s_loc, axis=-1, keepdims=True),
                                        jnp.max(s_ctx, axis=-1, keepdims=True)), sink)
            p_loc = jnp.exp(s_loc - m)
            p_ctx = jnp.exp(s_ctx - m)
            den = (jnp.sum(p_loc, axis=-1, keepdims=True) + jnp.sum(p_ctx, axis=-1, keepdims=True)
                   + jnp.exp(sink - m))
            p = jnp.concatenate([p_loc, p_ctx], axis=1).astype(BF16)
            o = _dot(p, vals) * (1.0 / den)
            out = out + jnp.where(mg, o, 0.0)
        o_ref[:, r * LANES:(r + 1) * LANES] = out.astype(BF16)


def _attention(sink, aq, ak, av, *, n_batch, seq, ctx_len):
    n_tok = aq.shape[0]
    nl = seq // CHUNK
    nc = ctx_len // CHUNK
    ctx0 = n_batch * nl

    def q_map(b, n):
        return (jnp.where(n < nl, b * nl + n, ctx0 + b * nc + (n - nl)), 0)

    def k_map(off):
        def f(b, n):
            return (b * nl + jnp.clip(n + off, 0, nl - 1), 0)
        return f

    x_map = lambda b, n: (n_batch * seq // ctx_len + b, 0)
    kv = lambda off: pl.BlockSpec((CHUNK, ATT_KVW), k_map(off))
    xspec = pl.BlockSpec((ctx_len, ATT_KVW), x_map)
    return pl.pallas_call(
        functools.partial(_attn_kernel, n_lat_blocks=nl, seq=seq),
        grid=(n_batch, nl + nc),
        in_specs=[pl.BlockSpec(memory_space=pltpu.SMEM), pl.BlockSpec((CHUNK, ATT_QW), q_map),
                  kv(-1), kv(0), kv(1), kv(-1), kv(0), kv(1), xspec, xspec],
        out_specs=pl.BlockSpec((CHUNK, ATT_QW), q_map),
        out_shape=jax.ShapeDtypeStruct((n_tok, ATT_QW), BF16),
        compiler_params=pltpu.CompilerParams(dimension_semantics=("arbitrary", "arbitrary")),
        name="window_attn",
    )(sink, aq, ak, ak, ak, av, av, av, ak, av)


def _mixout_kernel(x_ref, of_ref, ob_ref, rg_ref, su_ref, sv_ref, ao_ref, mod_ref, wout_ref, wcat_ref, bs_ref,
                   lng_ref, lnb_ref, wq_ref, x1_ref, h2t_ref, qp_ref, *, tiles_per_batch, n_batch):
    i = pl.program_id(0)
    grp = jnp.minimum(i // tiles_per_batch, n_batch)
    avg = _group_avg_matrix(RET_W)
    a = _group_ln(of_ref[...] + ob_ref[...], avg) * _silu(rg_ref[...])
    u = _gelu(su_ref[...])
    vn = _group_ln(_gelu(sv_ref[...]), avg)
    hm = _head_mask_stack(SGU_GROUPS, CHUNK, SGU_W)
    zero_b = jnp.zeros((), BF16)
    parts = []
    for c in range(TOK_TILE // CHUNK):
        vc = vn[c * CHUNK:(c + 1) * CHUNK].astype(BF16)
        vs = jnp.where(hm, jnp.concatenate([vc] * SGU_GROUPS, axis=0), zero_b)
        mix = _dot(wcat_ref[...], vs) + bs_ref[...]
        parts.append(u[c * CHUNK:(c + 1) * CHUNK] * mix)
    bmix = jnp.concatenate(parts, axis=0)
    y = (_dot(a.astype(BF16), wout_ref[0:RET_W, :])
         + _dot(bmix.astype(BF16), wout_ref[RET_W:RET_W + SGU_W, :])
         + _dot(ao_ref[...], wout_ref[RET_W + SGU_W:, :]))
    gate1 = mod_ref[pl.ds(grp, 1), 2 * D_MODEL:3 * D_MODEL]
    x1 = _ln(DEEPNORM_ALPHA * x_ref[...] + gate1 * y) * lng_ref[...] + lnb_ref[...]
    x1_ref[...] = x1
    shift2 = mod_ref[pl.ds(grp, 1), 3 * D_MODEL:4 * D_MODEL]
    scale2 = mod_ref[pl.ds(grp, 1), 4 * D_MODEL:5 * D_MODEL]
    h2f = _ln(x1) * (1.0 + scale2) + shift2
    h2t_ref[0] = h2f.T.astype(BF16)
    qp_ref[...] = _dot(h2f.astype(BF16), wq_ref[...]).astype(BF16)


def _mixout(x_all, o_f, o_b, rg, su, sv, ao, mod_l, w_out, w_cat, b_tab, ln_g, ln_b, wq, *, n_batch, seq):
    n_tok = x_all.shape[0]
    tiles_per_batch = seq // TOK_TILE
    tok = lambda w: pl.BlockSpec((TOK_TILE, w), lambda i: (i, 0))
    full = lambda a: pl.BlockSpec(a.shape, lambda i: (0,) * a.ndim)
    n_q = wq.shape[1]
    return pl.pallas_call(
        functools.partial(_mixout_kernel, tiles_per_batch=tiles_per_batch, n_batch=n_batch),
        grid=(n_tok // TOK_TILE,),
        in_specs=[tok(D_MODEL), tok(RET_W), tok(RET_W), tok(RET_W), tok(SGU_W), tok(SGU_W), tok(ATT_QW),
                  full(mod_l), full(w_out), full(w_cat), full(b_tab), full(ln_g), full(ln_b), full(wq)],
        out_specs=[tok(D_MODEL), pl.BlockSpec((1, D_MODEL, TOK_TILE), lambda i: (i, 0, 0)), tok(n_q)],
        out_shape=[jax.ShapeDtypeStruct((n_tok, D_MODEL), F32),
                   jax.ShapeDtypeStruct((n_tok // TOK_TILE, D_MODEL, TOK_TILE), BF16),
                   jax.ShapeDtypeStruct((n_tok, n_q), BF16)],
        compiler_params=pltpu.CompilerParams(dimension_semantics=("arbitrary",), vmem_limit_bytes=VMEM_LIMIT),
        name="mix_out",
    )(x_all, o_f, o_b, rg, su, sv, ao, mod_l, w_out, w_cat, b_tab, ln_g, ln_b, wq)


_CAND_SEGS = [(a, PEER_TOPK // (a + 1)) for a in range(1, 8)]


def _dup_bf16_words(x):
    hi = pltpu.bitcast(x.astype(BF16).astype(F32), jnp.uint32)
    return hi | (hi >> 16)


def _peer_kernel(h2t_ref, qp_ref, x1_ref, mod_ref, k1_ref, k2_ref, u_ref, vt_ref, lng_ref, lnb_ref, out_ref,
                 s_sc, v_sc, n_sc, a_sc, rk_sc, b_sc, row_sc, act0_sc, act1_sc, hs0_sc, hs1_sc, acc_sc,
                 *, tiles_per_batch, n_batch, n_blocks):
    t = pl.program_id(0)
    e = pl.program_id(1)
    NG = PEER_TOK // LANES
    K = PEER_TOPK
    half = PEER_QDIM // 2

    @pl.when(e == 0)
    def _prologue():
        acc_sc[...] = jnp.zeros_like(acc_sc)
        act1_sc[...] = jnp.zeros_like(act1_sc)
        hs1_sc[...] = jnp.zeros_like(hs1_sc)
        for h in range(PEER_HEADS):
            for p, kref in ((0, k1_ref), (1, k2_ref)):
                c0 = h * PEER_QDIM + p * half
                sT = _dot_nt(kref[...], qp_ref[:, c0:c0 + half])
                for g in range(NG):
                    s_sc[2 * h + p, g] = sT[:, g * LANES:(g + 1) * LANES]

        row8 = lax.broadcasted_iota(jnp.int32, (8, LANES), 0)

        def stats(it, carry):
            h = it // NG
            g = it % NG
            s1 = s_sc[2 * h, g]
            s2 = s_sc[2 * h + 1, g]
            cur1, cur2 = s1, s2
            rank2 = jnp.full(s2.shape, float(K), F32)
            for r in range(K):
                m1 = jnp.max(cur1, axis=0, keepdims=True)
                m2 = jnp.max(cur2, axis=0, keepdims=True)
                v_sc[0, r:r + 1, :] = m1
                v_sc[1, r:r + 1, :] = m2
                cur1 = jnp.where(cur1 >= m1, NEG_INF, cur1)
                hit2 = cur2 >= m2
                rank2 = jnp.where(hit2, float(r), rank2)
                cur2 = jnp.where(hit2, NEG_INF, cur2)
            v1 = v_sc[0]
            v2 = v_sc[1]
            segs = [v1[0:1] + v2]
            for a, n_a in _CAND_SEGS:
                segs.append(jnp.where(row8 < n_a, v1[a:a + 1] + v2[0:8], NEG_INF))
            segs.append(v1[8:16] + v2[0:1])
            cand = jnp.concatenate(segs, axis=0)
            cur = cand
            tau = None
            for r in range(K):
                tau = jnp.max(cur, axis=0, keepdims=True)
                cur = jnp.where(cur >= tau, NEG_INF, cur)
            cmax = v1[0:1] + v2[0:1]
            z = jnp.sum(jnp.where(cand >= tau, jnp.exp(cand - cmax), 0.0), axis=0, keepdims=True)
            n = jnp.zeros(s1.shape, F32)
            for bb in range(K):
                n = n + jnp.where((s1 + v2[bb:bb + 1]) >= tau, 1.0, 0.0)
            n_sc[h, g] = _dup_bf16_words(n)
            a_sc[h, g] = _dup_bf16_words(jnp.exp(s1 - v1[0:1]) * (1.0 / z))
            rk_sc[h, g] = rank2.astype(BF16)
            b_sc[h, g] = jnp.exp(s2 - v2[0:1]).astype(BF16)
            return carry

        lax.fori_loop(0, PEER_HEADS * NG, stats, 0)

    rows_per_blk = PEER_EB // PEER_NKEYS
    kc_rows = 16
    zero_b = jnp.zeros((), BF16)
    blk = jnp.where(e == 0, 1, jnp.where(e == n_blocks + 1, n_blocks - 2, e - 1))
    tile0 = pl.multiple_of((blk // 2) * 8, 8)

    def step(act_w, act_r, hs_w, hs_r, row_off):
        def half(c, carry):
            a_new = _gelu(_dot(u_ref[...], h2t_ref[c])).astype(BF16)
            act_w[2 * c] = a_new[:, 0:LANES]
            act_w[2 * c + 1] = a_new[:, LANES:2 * LANES]
            hs_prev = jnp.concatenate([hs_r[2 * c], hs_r[2 * c + 1]], axis=1)
            acc_sc[c] += _dot(vt_ref[0], hs_prev)
            gate_group(2 * c)
            gate_group(2 * c + 1)
            return carry

        def gate_group(g):
            for h in range(PEER_HEADS):
                n_tile = n_sc[h, g, pl.ds(tile0, 8), :]
                a_tile = a_sc[h, g, pl.ds(tile0, 8), :]
                for r in range(rows_per_blk):
                    k = row_off + r
                    row_sc[0, h, r] = pltpu.bitcast(jnp.broadcast_to(n_tile[k:k + 1, :], (8, LANES)), BF16)
                    row_sc[1, h, r] = pltpu.bitcast(jnp.broadcast_to(a_tile[k:k + 1, :], (8, LANES)), BF16)
            for kc in range(PEER_NKEYS // kc_rows):
                keys = slice(kc * kc_rows, (kc + 1) * kc_rows)
                gates = [jnp.zeros((kc_rows, LANES), BF16) for _ in range(rows_per_blk)]
                for h in range(PEER_HEADS):
                    rk = rk_sc[h, g, keys, :]
                    bb = b_sc[h, g, keys, :]
                    for r in range(rows_per_blk):
                        n_full = jnp.concatenate([row_sc[0, h, r]] * (kc_rows // 16), axis=0)
                        a_full = jnp.concatenate([row_sc[1, h, r]] * (kc_rows // 16), axis=0)
                        gates[r] = gates[r] + jnp.where(rk < n_full, a_full * bb, zero_b)
                for r in range(rows_per_blk):
                    rows = slice(r * PEER_NKEYS + kc * kc_rows, r * PEER_NKEYS + (kc + 1) * kc_rows)
                    hs_w[g, rows, :] = gates[r] * act_r[g, rows, :]

        lax.fori_loop(0, NG // 2, half, 0)

    @pl.when(e % 2 == 0)
    def _even():
        step(act0_sc, act1_sc, hs0_sc, hs1_sc, rows_per_blk)

    @pl.when(e % 2 == 1)
    def _odd():
        step(act1_sc, act0_sc, hs1_sc, hs0_sc, 0)

    @pl.when(e == n_blocks + 1)
    def _epilogue():
        grp = jnp.minimum(t // tiles_per_batch, n_batch)
        gate2 = mod_ref[pl.ds(grp, 1), 5 * D_MODEL:6 * D_MODEL]
        f = jnp.concatenate([acc_sc[c].T for c in range(NG // 2)], axis=0)
        out_ref[...] = _ln(DEEPNORM_ALPHA * x1_ref[...] + gate2 * f) * lng_ref[...] + lnb_ref[...]


def _peer(h2, qp, x1, mod_l, k1, k2, u_bf, vt3, ln_g, ln_b, *, n_batch, seq, n_tiles):
    tiles_per_batch = seq // PEER_TOK
    n_e = PEER_N // PEER_EB
    NG = PEER_TOK // LANES
    tok = lambda w: pl.BlockSpec((PEER_TOK, w), lambda t, e: (t, 0))
    full = lambda a: pl.BlockSpec(a.shape, lambda t, e: (0,) * a.ndim)
    stat = lambda n, dt: pltpu.VMEM((n, NG, PEER_NKEYS, LANES), dt)
    return pl.pallas_call(
        functools.partial(_peer_kernel, tiles_per_batch=tiles_per_batch, n_batch=n_batch, n_blocks=n_e),
        grid=(n_tiles, n_e + 2),
        in_specs=[pl.BlockSpec((PEER_TOK // TOK_TILE, D_MODEL, TOK_TILE), lambda t, e: (t, 0, 0)),
                  tok(qp.shape[1]), tok(D_MODEL), full(mod_l), full(k1), full(k2),
                  pl.BlockSpec((PEER_EB, D_MODEL), lambda t, e: (jnp.minimum(e, n_e - 1), 0)),
                  pl.BlockSpec((1, D_MODEL, PEER_EB), lambda t, e: (jnp.clip(e - 2, 0, n_e - 1), 0, 0)),
                  full(ln_g), full(ln_b)],
        out_specs=tok(D_MODEL),
        out_shape=jax.ShapeDtypeStruct((n_tiles * PEER_TOK, D_MODEL), F32),
        scratch_shapes=[stat(2 * PEER_HEADS, F32),
                        pltpu.VMEM((2, PEER_TOPK, LANES), F32),
                        stat(PEER_HEADS, jnp.uint32), stat(PEER_HEADS, jnp.uint32),
                        stat(PEER_HEADS, BF16), stat(PEER_HEADS, BF16),
                        pltpu.VMEM((2, PEER_HEADS, PEER_EB // PEER_NKEYS, 16, LANES), BF16),
                        pltpu.VMEM((NG, PEER_EB, LANES), BF16), pltpu.VMEM((NG, PEER_EB, LANES), BF16),
                        pltpu.VMEM((NG, PEER_EB, LANES), BF16), pltpu.VMEM((NG, PEER_EB, LANES), BF16),
                        pltpu.VMEM((NG // 2, D_MODEL, 2 * LANES), F32)],
        compiler_params=pltpu.CompilerParams(dimension_semantics=("arbitrary", "arbitrary"),
                                             vmem_limit_bytes=VMEM_LIMIT),
        name="peer",
    )(h2, qp, x1, mod_l, k1, k2, u_bf, vt3, ln_g, ln_b)


def _rope_tables(seq, ctx_len):
    rows = seq // GRID_W
    row_id = jnp.repeat(jnp.arange(rows), GRID_W).astype(F32)
    col_id = jnp.tile(jnp.arange(GRID_W), rows).astype(F32)
    inv = jnp.power(ROPE_BASE, -jnp.arange(ROPE_PAIRS, dtype=F32) / ROPE_PAIRS)
    ang_r = row_id[:, None] * inv
    ang_c = col_id[:, None] * inv
    cos64 = jnp.concatenate([jnp.cos(ang_r)] * 2 + [jnp.cos(ang_c)] * 2, axis=-1)
    sin64 = jnp.concatenate([jnp.sin(ang_r)] * 2 + [jnp.sin(ang_c)] * 2, axis=-1)
    n_rep = RET_W // HEAD_DIM
    cos_tab = jnp.concatenate([jnp.tile(cos64, (1, n_rep)), jnp.ones((ctx_len, RET_W), F32)], axis=0)
    sin_tab = jnp.concatenate([jnp.tile(sin64, (1, n_rep)), jnp.zeros((ctx_len, RET_W), F32)], axis=0)
    return cos_tab, sin_tab


def _rot_partner(width):
    l = np.arange(width)
    lo = (l % 32) < 16
    partner = np.where(lo, l + 16, l - 16)
    sign = np.where(lo, -1.0, 1.0).astype(np.float32)
    return partner, sign


def _slab_perm():
    new = np.arange(ATT_QW)
    r, rem = new // LANES, new % LANES
    g, d = rem // HEAD_DIM, rem % HEAD_DIM
    return (g * GQA_GROUP + r) * HEAD_DIM + d


def _prep_w_in(w_in):
    o_rq, o_rk, o_aq, o_ak = 0, RET_W, 4 * RET_W + 2 * SGU_W, 4 * RET_W + 2 * SGU_W + ATT_QW
    slab = _slab_perm()
    aq = w_in[..., o_aq:o_aq + ATT_QW]

    def partner(block):
        p, sg = _rot_partner(block.shape[-1])
        return block[..., p] * sg

    base = jnp.concatenate([w_in[..., :o_aq], aq[..., slab], w_in[..., o_ak:]], axis=-1)
    rot = jnp.concatenate([partner(w_in[..., o_rq:o_rq + RET_W]), partner(w_in[..., o_rk:o_rk + RET_W]),
                           partner(aq)[..., slab], partner(w_in[..., o_ak:o_ak + ATT_KVW])], axis=-1)
    return jnp.concatenate([base, rot], axis=-1).astype(BF16)


def kernel(x, c, ctx, c_ctx, w_mod, b_mod, w_in, w_out, ret_decay_fwd, ret_decay_bwd, sgu_w, sgu_b, attn_sink,
           ln_mix_g, ln_mix_b, peer_wq, peer_k1, peer_k2, peer_u, peer_v, ln_ffn_g, ln_ffn_b):
    n_batch, seq, _ = x.shape
    ctx_len = ctx.shape[1]
    depth = w_in.shape[0]
    assert seq % PEER_TOK == 0 and ctx_len == TOK_TILE and (n_batch * ctx_len) % PEER_TOK == 0
    assert n_batch + 1 <= 8
    n_lat = n_batch * seq
    kw = dict(n_batch=n_batch, seq=seq)

    x_all = jnp.concatenate([x.reshape(n_lat, D_MODEL), ctx.reshape(n_batch * ctx_len, D_MODEL)], axis=0)
    cond = jnp.concatenate([c, c_ctx[None, :], jnp.zeros((8 - n_batch - 1, D_MODEL), F32)], axis=0)
    mods = _modulations(cond, w_mod, b_mod)

    cos_tab, sin_tab = _rope_tables(seq, ctx_len)
    w_ext = _prep_w_in(w_in)
    slab = _slab_perm()
    w_out_p = jnp.concatenate([w_out[:, :RET_W + SGU_W], w_out[:, RET_W + SGU_W:][:, slab]], axis=1).astype(BF16)
    w_cat = jnp.transpose(sgu_w, (0, 2, 1, 3)).reshape(depth, CHUNK, SGU_GROUPS * CHUNK).astype(BF16)
    b_tab = jnp.repeat(jnp.transpose(sgu_b, (0, 2, 1)), HEAD_DIM, axis=2)
    decays = jnp.stack([ret_decay_fwd, ret_decay_bwd], axis=1)
    wq = peer_wq.astype(BF16)
    k1 = peer_k1.astype(BF16)
    k2 = peer_k2.astype(BF16)
    u_bf = peer_u.astype(BF16)
    vt3 = jnp.transpose(peer_v.reshape(depth, PEER_N // PEER_EB, PEER_EB, D_MODEL), (0, 1, 3, 2)).astype(BF16)
    row = lambda a: a.reshape(depth, 1, D_MODEL)
    lmg, lmb, lfg, lfb = row(ln_mix_g), row(ln_mix_b), row(ln_ffn_g), row(ln_ffn_b)

    for l in range(depth):
        last = l == depth - 1
        rq, rk, rv, rg, su, sv, aq, ak, av = _inproj(x_all, mods[l], w_ext[l], cos_tab, sin_tab, **kw)
        o_f, o_b = _retention(decays[l], rq, rk, rv, ctx_len=ctx_len, **kw)
        ao = _attention(attn_sink[l], aq, ak, av, ctx_len=ctx_len, **kw)
        x1, h2, qp = _mixout(x_all, o_f, o_b, rg, su, sv, ao, mods[l], w_out_p[l], w_cat[l], b_tab[l],
                             lmg[l], lmb[l], wq[l], **kw)
        n_tiles = (n_lat if last else x_all.shape[0]) // PEER_TOK
        x_all = _peer(h2, qp, x1, mods[l], k1[l], k2[l], u_bf[l], vt3[l], lfg[l], lfb[l], n_tiles=n_tiles, **kw)
    return x_all[:n_lat].reshape(n_batch, seq, D_MODEL)
```

```python
import functools

import numpy as np
import jax
import jax.numpy as jnp
from jax import lax
from jax.experimental import pallas as pl
from jax.experimental.pallas import tpu as pltpu

F32 = jnp.float32
BF16 = jnp.bfloat16

D_MODEL = 1024
DEPTH = 4
GRID_W = 64
HEAD_DIM = 64
ROPE_PAIRS = HEAD_DIM // 4
ROPE_BASE = 10000.0
RET_HEADS = 4
CHUNK = 128
RET_W = RET_HEADS * HEAD_DIM
RET_SCALE = HEAD_DIM ** -0.5
SGU_GROUPS = 4
SGU_W = SGU_GROUPS * HEAD_DIM
ATT_Q_HEADS = 8
ATT_KV_HEADS = 2
GQA_GROUP = ATT_Q_HEADS // ATT_KV_HEADS
ATT_QW = ATT_Q_HEADS * HEAD_DIM
ATT_KVW = ATT_KV_HEADS * HEAD_DIM
ATT_SCALE = HEAD_DIM ** -0.5
D_IN = 4 * RET_W + 2 * SGU_W + ATT_QW + 2 * ATT_KVW
D_ROT = 2 * RET_W + ATT_QW + ATT_KVW
PEER_HEADS = 8
PEER_NKEYS = 128
PEER_N = PEER_NKEYS * PEER_NKEYS
PEER_QDIM = 256
PEER_TOPK = 16
LN_EPS = 1e-5
DEEPNORM_ALPHA = (2 * DEPTH) ** 0.25

LANES = 128
TOK_TILE = 256
PEER_TOK = 512
PEER_EB = 512
VMEM_LIMIT = 56 * 1024 * 1024

NEG_INF = float("-inf")


def _ln(x):
    mu = jnp.mean(x, axis=-1, keepdims=True)
    xc = x - mu
    var = jnp.mean(xc * xc, axis=-1, keepdims=True)
    return xc * lax.rsqrt(var + LN_EPS)


def _gelu(x):
    h = 0.5 * x
    return h + h * jnp.tanh(x * (0.7978845608028654 + (0.7978845608028654 * 0.044715) * (x * x)))


def _silu(x):
    return x * (1.0 / (1.0 + jnp.exp(-x)))


def _dot(a, b):
    return jnp.dot(a, b, preferred_element_type=F32)


def _dot_nt(a, b):
    return lax.dot_general(a, b, (((1,), (1,)), ((), ())), preferred_element_type=F32)


def _dot_tn(a, b):
    return lax.dot_general(a, b, (((0,), (0,)), ((), ())), preferred_element_type=F32)


def _group_mean(z, avg):
    hi = z.astype(BF16)
    lo = (z - hi.astype(F32)).astype(BF16)
    return _dot(hi, avg) + _dot(lo, avg)


def _group_ln(x, avg):
    mu = _group_mean(x, avg)
    xc = x - mu
    var = _group_mean(xc * xc, avg)
    return xc * lax.rsqrt(var + LN_EPS)


def _group_avg_matrix(width):
    r = lax.broadcasted_iota(jnp.int32, (width, width), 0) // HEAD_DIM
    c = lax.broadcasted_iota(jnp.int32, (width, width), 1) // HEAD_DIM
    return jnp.where(r == c, 1.0 / HEAD_DIM, 0.0).astype(BF16)


def _head_mask_stack(n_heads, rows, width):
    r = lax.broadcasted_iota(jnp.int32, (n_heads * rows, width), 0) // rows
    c = lax.broadcasted_iota(jnp.int32, (n_heads * rows, width), 1) // HEAD_DIM
    return r == c


def _mod_kernel(c_ref, w_ref, b_ref, o_ref):
    s = _silu(c_ref[...])
    hi = s.astype(BF16)
    lo = (s - hi.astype(F32)).astype(BF16)
    w = w_ref[0]
    whi = w.astype(BF16)
    wlo = (w - whi.astype(F32)).astype(BF16)
    o_ref[0] = _dot(hi, whi) + _dot(lo, whi) + _dot(hi, wlo) + b_ref[0]


def _modulations(cond_rows, w_mod, b_mod):
    depth = w_mod.shape[0]
    n_rows = cond_rows.shape[0]
    col = 1024
    n_col = w_mod.shape[2] // col
    return pl.pallas_call(
        _mod_kernel,
        grid=(depth, n_col),
        in_specs=[
            pl.BlockSpec((n_rows, D_MODEL), lambda l, j: (0, 0)),
            pl.BlockSpec((1, D_MODEL, col), lambda l, j: (l, 0, j)),
            pl.BlockSpec((1, 1, col), lambda l, j: (l, 0, j)),
        ],
        out_specs=pl.BlockSpec((1, n_rows, col), lambda l, j: (l, 0, j)),
        out_shape=jax.ShapeDtypeStruct((depth, n_rows, w_mod.shape[2]), F32),
        name="adaln_mod",
    )(cond_rows, w_mod, b_mod.reshape(depth, 1, -1))


def _inproj_kernel(x_ref, mod_ref, w_ref, cos_ref, sin_ref,
                   rq_ref, rk_ref, rv_ref, rg_ref, su_ref, sv_ref, aq_ref, ak_ref, av_ref,
                   *, tiles_per_batch, n_batch):
    i = pl.program_id(0)
    grp = jnp.minimum(i // tiles_per_batch, n_batch)
    shift = mod_ref[pl.ds(grp, 1), 0:D_MODEL]
    scale = mod_ref[pl.ds(grp, 1), D_MODEL:2 * D_MODEL]
    h = (_ln(x_ref[...]) * (1.0 + scale) + shift).astype(BF16)
    cos = cos_ref[...]
    sin = sin_ref[...]

    def proj(c0, w):
        return _dot(h, w_ref[:, c0:c0 + w])

    rot0 = D_IN
    rq_ref[...] = (proj(0, RET_W) * cos + proj(rot0, RET_W) * sin).astype(BF16)
    rk_ref[...] = ((proj(RET_W, RET_W) * cos + proj(rot0 + RET_W, RET_W) * sin) * RET_SCALE).astype(BF16)
    rv_ref[...] = proj(2 * RET_W, RET_W).astype(BF16)
    rg_ref[...] = proj(3 * RET_W, RET_W)
    su_ref[...] = proj(4 * RET_W, SGU_W)
    sv_ref[...] = proj(4 * RET_W + SGU_W, SGU_W)
    aq0 = 4 * RET_W + 2 * SGU_W
    aqr = rot0 + 2 * RET_W
    for half in range(ATT_QW // RET_W):
        o = half * RET_W
        aq_ref[:, o:o + RET_W] = ((proj(aq0 + o, RET_W) * cos + proj(aqr + o, RET_W) * sin) * ATT_SCALE).astype(BF16)
    ak0 = aq0 + ATT_QW
    akr = aqr + ATT_QW
    ak_ref[...] = (proj(ak0, ATT_KVW) * cos[:, 0:ATT_KVW] + proj(akr, ATT_KVW) * sin[:, 0:ATT_KVW]).astype(BF16)
    av_ref[...] = proj(ak0 + ATT_KVW, ATT_KVW).astype(BF16)


def _inproj(x_all, mod_l, w_ext, cos_tab, sin_tab, *, n_batch, seq):
    n_tok = x_all.shape[0]
    n_tiles = n_tok // TOK_TILE
    tiles_per_batch = seq // TOK_TILE
    n_lat_tiles = n_batch * tiles_per_batch

    def tab_map(i):
        return (jnp.where(i < n_lat_tiles, i % tiles_per_batch, tiles_per_batch), 0)

    tok = lambda w: pl.BlockSpec((TOK_TILE, w), lambda i: (i, 0))
    full = lambda a: pl.BlockSpec(a.shape, lambda i: (0,) * a.ndim)
    out_w = [(RET_W, BF16), (RET_W, BF16), (RET_W, BF16), (RET_W, F32), (SGU_W, F32), (SGU_W, F32),
             (ATT_QW, BF16), (ATT_KVW, BF16), (ATT_KVW, BF16)]
    return pl.pallas_call(
        functools.partial(_inproj_kernel, tiles_per_batch=tiles_per_batch, n_batch=n_batch),
        grid=(n_tiles,),
        in_specs=[tok(D_MODEL), full(mod_l), full(w_ext),
                  pl.BlockSpec((TOK_TILE, RET_W), tab_map), pl.BlockSpec((TOK_TILE, RET_W), tab_map)],
        out_specs=[tok(w) for w, _ in out_w],
        out_shape=[jax.ShapeDtypeStruct((n_tok, w), dt) for w, dt in out_w],
        compiler_params=pltpu.CompilerParams(dimension_semantics=("arbitrary",), vmem_limit_bytes=VMEM_LIMIT),
        name="inproj",
    )(x_all, mod_l, w_ext, cos_tab, sin_tab)


def _ret_kernel(dec_ref, qf_ref, kf_ref, vf_ref, qb_ref, kb_ref, vb_ref, of_ref, ob_ref,
                sf_ref, sb_ref, intra_ref, qd_ref, kd_ref, cd_ref):
    b = pl.program_id(0)
    s = pl.program_id(1)
    C = CHUNK
    W = RET_W

    @pl.when((b == 0) & (s == 0))
    def _tables():
        dec = dec_ref[...]
        lg = jnp.minimum(dec, 0.0) - jnp.log(1.0 + jnp.exp(-jnp.abs(dec)))
        lane_head = lax.broadcasted_iota(jnp.int32, (1, W), 1) // HEAD_DIM
        ii = lax.broadcasted_iota(jnp.int32, (C, C), 0)
        jj = lax.broadcasted_iota(jnp.int32, (C, C), 1)
        ri = lax.broadcasted_iota(jnp.int32, (C, W), 0).astype(F32)
        rb = lax.broadcasted_iota(jnp.int32, (W, W), 0) // HEAD_DIM
        cb = lax.broadcasted_iota(jnp.int32, (W, W), 1) // HEAD_DIM
        for d in range(2):
            lgl = jnp.zeros((1, W), F32)
            for hh in range(RET_HEADS):
                lg_h = lg[d:d + 1, hh:hh + 1]
                lgl = lgl + jnp.where(lane_head == hh, lg_h, 0.0)
                rel = (ii - jj) if d == 0 else (jj - ii)
                m = jnp.exp(jnp.maximum(rel, 0).astype(F32) * lg_h)
                intra_ref[d, hh * C:(hh + 1) * C, :] = jnp.where(rel >= 0, m, 0.0)
            if d == 0:
                qd_ref[d] = jnp.exp((ri + 1.0) * lgl)
                kd_ref[d] = jnp.exp((C - 1.0 - ri) * lgl)
            else:
                qd_ref[d] = jnp.exp((C - ri) * lgl)
                kd_ref[d] = jnp.exp(ri * lgl)
            cd_ref[d] = jnp.where(rb == cb, jnp.exp(C * lgl), 0.0)

    @pl.when(s == 0)
    def _zero():
        sf_ref[...] = jnp.zeros_like(sf_ref)
        sb_ref[...] = jnp.zeros_like(sb_ref)

    hm = _head_mask_stack(RET_HEADS, C, W)
    rb = lax.broadcasted_iota(jnp.int32, (W, W), 0) // HEAD_DIM
    cb = lax.broadcasted_iota(jnp.int32, (W, W), 1) // HEAD_DIM
    bd = rb == cb
    zero_b = jnp.zeros((), BF16)

    def direction(d, q_ref, k_ref, v_ref, st_ref, o_ref):
        q = q_ref[...]
        k = k_ref[...]
        v = v_ref[...]
        qs = jnp.where(hm, jnp.concatenate([q] * RET_HEADS, axis=0), zero_b)
        att = (_dot_nt(qs, k) * intra_ref[d]).astype(BF16)
        att = jnp.concatenate([att[hh * C:(hh + 1) * C] for hh in range(RET_HEADS)], axis=1)
        vs = jnp.where(hm, jnp.concatenate([v] * RET_HEADS, axis=0), zero_b)
        st = st_ref[...]
        o = _dot(att, vs) + _dot(q, st.astype(BF16)) * qd_ref[d]
        o_ref[...] = o
        kdec = (k.astype(F32) * kd_ref[d]).astype(BF16)
        st_ref[...] = st * cd_ref[d] + jnp.where(bd, _dot_tn(kdec, v), 0.0)

    direction(0, qf_ref, kf_ref, vf_ref, sf_ref, of_ref)
    direction(1, qb_ref, kb_ref, vb_ref, sb_ref, ob_ref)


def _retention(decays, rq, rk, rv, *, n_batch, seq, ctx_len):
    n_tok = rq.shape[0]
    nl = seq // CHUNK
    nc = ctx_len // CHUNK
    ctx0 = n_batch * nl

    def fwd_map(b, s):
        return (jnp.where(s < nc, ctx0 + b * nc + s, b * nl + (s - nc)), 0)

    def bwd_map(b, s):
        return (jnp.where(s < nc, ctx0 + b * nc + (nc - 1 - s), b * nl + (nl - 1 - (s - nc))), 0)

    fspec = pl.BlockSpec((CHUNK, RET_W), fwd_map)
    bspec = pl.BlockSpec((CHUNK, RET_W), bwd_map)
    return pl.pallas_call(
        _ret_kernel,
        grid=(n_batch, nc + nl),
        in_specs=[pl.BlockSpec(decays.shape, lambda b, s: (0, 0)), fspec, fspec, fspec, bspec, bspec, bspec],
        out_specs=[fspec, bspec],
        out_shape=[jax.ShapeDtypeStruct((n_tok, RET_W), F32)] * 2,
        scratch_shapes=[
            pltpu.VMEM((RET_W, RET_W), F32), pltpu.VMEM((RET_W, RET_W), F32),
            pltpu.VMEM((2, RET_HEADS * CHUNK, CHUNK), F32),
            pltpu.VMEM((2, CHUNK, RET_W), F32), pltpu.VMEM((2, CHUNK, RET_W), F32),
            pltpu.VMEM((2, RET_W, RET_W), F32),
        ],
        compiler_params=pltpu.CompilerParams(dimension_semantics=("arbitrary", "arbitrary")),
        name="retention",
    )(decays, rq, rk, rv, rq, rk, rv)


def _attn_kernel(sink_ref, q_ref, kp_ref, kc_ref, kn_ref, vp_ref, vc_ref, vn_ref, kx_ref, vx_ref, o_ref,
                 *, n_lat_blocks, seq):
    n = pl.program_id(1)
    W = CHUNK
    qi = lax.broadcasted_iota(jnp.int32, (W, 3 * W), 0)
    kj = lax.broadcasted_iota(jnp.int32, (W, 3 * W), 1)
    rel = kj - W - qi
    kpos = n * W - W + kj
    mask = (jnp.abs(rel) <= W) & (kpos >= 0) & (kpos < seq) & (n < n_lat_blocks)
    keys = jnp.concatenate([kp_ref[...], kc_ref[...], kn_ref[...], kx_ref[...]], axis=0)
    vals = jnp.concatenate([vp_ref[...], vc_ref[...], vn_ref[...], vx_ref[...]], axis=0)
    lane_head = lax.broadcasted_iota(jnp.int32, (1, LANES), 1) // HEAD_DIM
    zero_b = jnp.zeros((), BF16)
    for r in range(GQA_GROUP):
        slab = q_ref[:, r * LANES:(r + 1) * LANES]
        out = jnp.zeros((W, LANES), F32)
        for g in range(ATT_KV_HEADS):
            mg = lane_head == g
            sc = _dot_nt(jnp.where(mg, slab, zero_b), keys)
            s_loc = jnp.where(mask, sc[:, 0:3 * W], NEG_INF)
            s_ctx = sc[:, 3 * W:]
            sink = sink_ref[g * GQA_GROUP + r]
            m = jnp.maximum(jnp.maximum(jnp.max(s_loc, axis=-1, keepdims=True),
                                        jnp.max(s_ctx, axis=-1, keepdims=True)), sink)
            p_loc = jnp.exp(s_loc - m)
            p_ctx = jnp.exp(s_ctx - m)
            den = (jnp.sum(p_loc, axis=-1, keepdims=True) + jnp.sum(p_ctx, axis=-1, keepdims=True)
                   + jnp.exp(sink - m))
            p = jnp.concatenate([p_loc, p_ctx], axis=1).astype(BF16)
            o = _dot(p, vals) * (1.0 / den)
            out = out + jnp.where(mg, o, 0.0)
        o_ref[:, r * LANES:(r + 1) * LANES] = out.astype(BF16)


def _attention(sink, aq, ak, av, *, n_batch, seq, ctx_len):
    n_tok = aq.shape[0]
    nl = seq // CHUNK
    nc = ctx_len // CHUNK
    ctx0 = n_batch * nl

    def q_map(b, n):
        return (jnp.where(n < nl, b * nl + n, ctx0 + b * nc + (n - nl)), 0)

    def k_map(off):
        def f(b, n):
            return (b * nl + jnp.clip(n + off, 0, nl - 1), 0)
        return f

    x_map = lambda b, n: (n_batch * seq // ctx_len + b, 0)
    kv = lambda off: pl.BlockSpec((CHUNK, ATT_KVW), k_map(off))
    xspec = pl.BlockSpec((ctx_len, ATT_KVW), x_map)
    return pl.pallas_call(
        functools.partial(_attn_kernel, n_lat_blocks=nl, seq=seq),
        grid=(n_batch, nl + nc),
        in_specs=[pl.BlockSpec(memory_space=pltpu.SMEM), pl.BlockSpec((CHUNK, ATT_QW), q_map),
                  kv(-1), kv(0), kv(1), kv(-1), kv(0), kv(1), xspec, xspec],
        out_specs=pl.BlockSpec((CHUNK, ATT_QW), q_map),
        out_shape=jax.ShapeDtypeStruct((n_tok, ATT_QW), BF16),
        compiler_params=pltpu.CompilerParams(dimension_semantics=("arbitrary", "arbitrary")),
        name="window_attn",
    )(sink, aq, ak, ak, ak, av, av, av, ak, av)


def _mixout_kernel(x_ref, of_ref, ob_ref, rg_ref, su_ref, sv_ref, ao_ref, mod_ref, wout_ref, wcat_ref, bs_ref,
                   lng_ref, lnb_ref, wq_ref, x1_ref, h2t_ref, qp_ref, *, tiles_per_batch, n_batch):
    i = pl.program_id(0)
    grp = jnp.minimum(i // tiles_per_batch, n_batch)
    avg = _group_avg_matrix(RET_W)
    a = _group_ln(of_ref[...] + ob_ref[...], avg) * _silu(rg_ref[...])
    u = _gelu(su_ref[...])
    vn = _group_ln(_gelu(sv_ref[...]), avg)
    hm = _head_mask_stack(SGU_GROUPS, CHUNK, SGU_W)
    zero_b = jnp.zeros((), BF16)
    parts = []
    for c in range(TOK_TILE // CHUNK):
        vc = vn[c * CHUNK:(c + 1) * CHUNK].astype(BF16)
        vs = jnp.where(hm, jnp.concatenate([vc] * SGU_GROUPS, axis=0), zero_b)
        mix = _dot(wcat_ref[...], vs) + bs_ref[...]
        parts.append(u[c * CHUNK:(c + 1) * CHUNK] * mix)
    bmix = jnp.concatenate(parts, axis=0)
    y = (_dot(a.astype(BF16), wout_ref[0:RET_W, :])
         + _dot(bmix.astype(BF16), wout_ref[RET_W:RET_W + SGU_W, :])
         + _dot(ao_ref[...], wout_ref[RET_W + SGU_W:, :]))
    gate1 = mod_ref[pl.ds(grp, 1), 2 * D_MODEL:3 * D_MODEL]
    x1 = _ln(DEEPNORM_ALPHA * x_ref[...] + gate1 * y) * lng_ref[...] + lnb_ref[...]
    x1_ref[...] = x1
    shift2 = mod_ref[pl.ds(grp, 1), 3 * D_MODEL:4 * D_MODEL]
    scale2 = mod_ref[pl.ds(grp, 1), 4 * D_MODEL:5 * D_MODEL]
    h2f = _ln(x1) * (1.0 + scale2) + shift2
    h2t_ref[0] = h2f.T.astype(BF16)
    qp_ref[...] = _dot(h2f.astype(BF16), wq_ref[...]).astype(BF16)


def _mixout(x_all, o_f, o_b, rg, su, sv, ao, mod_l, w_out, w_cat, b_tab, ln_g, ln_b, wq, *, n_batch, seq):
    n_tok = x_all.shape[0]
    tiles_per_batch = seq // TOK_TILE
    tok = lambda w: pl.BlockSpec((TOK_TILE, w), lambda i: (i, 0))
    full = lambda a: pl.BlockSpec(a.shape, lambda i: (0,) * a.ndim)
    n_q = wq.shape[1]
    return pl.pallas_call(
        functools.partial(_mixout_kernel, tiles_per_batch=tiles_per_batch, n_batch=n_batch),
        grid=(n_tok // TOK_TILE,),
        in_specs=[tok(D_MODEL), tok(RET_W), tok(RET_W), tok(RET_W), tok(SGU_W), tok(SGU_W), tok(ATT_QW),
                  full(mod_l), full(w_out), full(w_cat), full(b_tab), full(ln_g), full(ln_b), full(wq)],
        out_specs=[tok(D_MODEL), pl.BlockSpec((1, D_MODEL, TOK_TILE), lambda i: (i, 0, 0)), tok(n_q)],
        out_shape=[jax.ShapeDtypeStruct((n_tok, D_MODEL), F32),
                   jax.ShapeDtypeStruct((n_tok // TOK_TILE, D_MODEL, TOK_TILE), BF16),
                   jax.ShapeDtypeStruct((n_tok, n_q), BF16)],
        compiler_params=pltpu.CompilerParams(dimension_semantics=("arbitrary",), vmem_limit_bytes=VMEM_LIMIT),
        name="mix_out",
    )(x_all, o_f, o_b, rg, su, sv, ao, mod_l, w_out, w_cat, b_tab, ln_g, ln_b, wq)


_CAND_SEGS = [(a, PEER_TOPK // (a + 1)) for a in range(1, 8)]


def _dup_bf16_words(x):
    hi = pltpu.bitcast(x.astype(BF16).astype(F32), jnp.uint32)
    return hi | (hi >> 16)


def _peer_kernel(h2t_ref, qp_ref, x1_ref, mod_ref, k1_ref, k2_ref, u_ref, vt_ref, lng_ref, lnb_ref, out_ref,
                 s_sc, v_sc, n_sc, a_sc, rk_sc, b_sc, row_sc, act0_sc, act1_sc, hs0_sc, hs1_sc, acc_sc,
                 *, tiles_per_batch, n_batch, n_blocks):
    t = pl.program_id(0)
    e = pl.program_id(1)
    NG = PEER_TOK // LANES
    K = PEER_TOPK
    half = PEER_QDIM // 2

    @pl.when(e == 0)
    def _prologue():
        acc_sc[...] = jnp.zeros_like(acc_sc)
        act1_sc[...] = jnp.zeros_like(act1_sc)
        hs1_sc[...] = jnp.zeros_like(hs1_sc)
        for h in range(PEER_HEADS):
            for p, kref in ((0, k1_ref), (1, k2_ref)):
                c0 = h * PEER_QDIM + p * half
                sT = _dot_nt(kref[...], qp_ref[:, c0:c0 + half])
                for g in range(NG):
                    s_sc[2 * h + p, g] = sT[:, g * LANES:(g + 1) * LANES]

        row8 = lax.broadcasted_iota(jnp.int32, (8, LANES), 0)

        def stats(it, carry):
            h = it // NG
            g = it % NG
            s1 = s_sc[2 * h, g]
            s2 = s_sc[2 * h + 1, g]
            cur1, cur2 = s1, s2
            rank2 = jnp.full(s2.shape, float(K), F32)
            for r in range(K):
                m1 = jnp.max(cur1, axis=0, keepdims=True)
                m2 = jnp.max(cur2, axis=0, keepdims=True)
                v_sc[0, r:r + 1, :] = m1
                v_sc[1, r:r + 1, :] = m2
                cur1 = jnp.where(cur1 >= m1, NEG_INF, cur1)
                hit2 = cur2 >= m2
                rank2 = jnp.where(hit2, float(r), rank2)
                cur2 = jnp.where(hit2, NEG_INF, cur2)
            v1 = v_sc[0]
            v2 = v_sc[1]
            segs = [v1[0:1] + v2]
            for a, n_a in _CAND_SEGS:
                segs.append(jnp.where(row8 < n_a, v1[a:a + 1] + v2[0:8], NEG_INF))
            segs.append(v1[8:16] + v2[0:1])
            cand = jnp.concatenate(segs, axis=0)
            cur = cand
            tau = None
            for r in range(K):
                tau = jnp.max(cur, axis=0, keepdims=True)
                cur = jnp.where(cur >= tau, NEG_INF, cur)
            cmax = v1[0:1] + v2[0:1]
            z = jnp.sum(jnp.where(cand >= tau, jnp.exp(cand - cmax), 0.0), axis=0, keepdims=True)
            n = jnp.zeros(s1.shape, F32)
            for bb in range(K):
                n = n + jnp.where((s1 + v2[bb:bb + 1]) >= tau, 1.0, 0.0)
            n_sc[h, g] = _dup_bf16_words(n)
            a_sc[h, g] = _dup_bf16_words(jnp.exp(s1 - v1[0:1]) * (1.0 / z))
            rk_sc[h, g] = rank2.astype(BF16)
            b_sc[h, g] = jnp.exp(s2 - v2[0:1]).astype(BF16)
            return carry

        lax.fori_loop(0, PEER_HEADS * NG, stats, 0)

    rows_per_blk = PEER_EB // PEER_NKEYS
    kc_rows = 16
    zero_b = jnp.zeros((), BF16)
    blk = jnp.where(e == 0, 1, jnp.where(e == n_blocks + 1, n_blocks - 2, e - 1))
    tile0 = pl.multiple_of((blk // 2) * 8, 8)

    def step(act_w, act_r, hs_w, hs_r, row_off):
        def half(c, carry):
            a_new = _gelu(_dot(u_ref[0], h2t_ref[c])).astype(BF16)
            act_w[2 * c] = a_new[:, 0:LANES]
            act_w[2 * c + 1] = a_new[:, LANES:2 * LANES]
            hs_prev = jnp.concatenate([hs_r[2 * c], hs_r[2 * c + 1]], axis=1)
            acc_sc[c] += _dot(vt_ref[0, 0], hs_prev)
            gate_group(2 * c)
            gate_group(2 * c + 1)
            return carry

        def gate_group(g):
            for h in range(PEER_HEADS):
                n_tile = n_sc[h, g, pl.ds(tile0, 8), :]
                a_tile = a_sc[h, g, pl.ds(tile0, 8), :]
                for r in range(rows_per_blk):
                    k = row_off + r
                    row_sc[0, h, r] = pltpu.bitcast(jnp.broadcast_to(n_tile[k:k + 1, :], (8, LANES)), BF16)
                    row_sc[1, h, r] = pltpu.bitcast(jnp.broadcast_to(a_tile[k:k + 1, :], (8, LANES)), BF16)
            for kc in range(PEER_NKEYS // kc_rows):
                keys = slice(kc * kc_rows, (kc + 1) * kc_rows)
                gates = [jnp.zeros((kc_rows, LANES), BF16) for _ in range(rows_per_blk)]
                for h in range(PEER_HEADS):
                    rk = rk_sc[h, g, keys, :]
                    bb = b_sc[h, g, keys, :]
                    for r in range(rows_per_blk):
                        n_full = jnp.concatenate([row_sc[0, h, r]] * (kc_rows // 16), axis=0)
                        a_full = jnp.concatenate([row_sc[1, h, r]] * (kc_rows // 16), axis=0)
                        gates[r] = gates[r] + jnp.where(rk < n_full, a_full * bb, zero_b)
                for r in range(rows_per_blk):
                    rows = slice(r * PEER_NKEYS + kc * kc_rows, r * PEER_NKEYS + (kc + 1) * kc_rows)
                    hs_w[g, rows, :] = gates[r] * act_r[g, rows, :]

        lax.fori_loop(0, NG // 2, half, 0)

    @pl.when(e % 2 == 0)
    def _even():
        step(act0_sc, act1_sc, hs0_sc, hs1_sc, rows_per_blk)

    @pl.when(e % 2 == 1)
    def _odd():
        step(act1_sc, act0_sc, hs1_sc, hs0_sc, 0)

    @pl.when(e == n_blocks + 1)
    def _epilogue():
        grp = jnp.minimum(t // tiles_per_batch, n_batch)
        gate2 = mod_ref[pl.ds(grp, 1), 5 * D_MODEL:6 * D_MODEL]
        f = jnp.concatenate([acc_sc[c].T for c in range(NG // 2)], axis=0)
        out_ref[...] = _ln(DEEPNORM_ALPHA * x1_ref[...] + gate2 * f) * lng_ref[...] + lnb_ref[...]


def _peer(h2, qp, x1, mod_l, k1, k2, u_bf, vt3, ln_g, ln_b, *, layer, n_batch, seq, n_tiles):
    tiles_per_batch = seq // PEER_TOK
    n_e = PEER_N // PEER_EB
    NG = PEER_TOK // LANES
    tok = lambda w: pl.BlockSpec((PEER_TOK, w), lambda t, e: (t, 0))
    full = lambda a: pl.BlockSpec(a.shape, lambda t, e: (0,) * a.ndim)
    stat = lambda n, dt: pltpu.VMEM((n, NG, PEER_NKEYS, LANES), dt)
    return pl.pallas_call(
        functools.partial(_peer_kernel, tiles_per_batch=tiles_per_batch, n_batch=n_batch, n_blocks=n_e),
        grid=(n_tiles, n_e + 2),
        in_specs=[pl.BlockSpec((PEER_TOK // TOK_TILE, D_MODEL, TOK_TILE), lambda t, e: (t, 0, 0)),
                  tok(qp.shape[1]), tok(D_MODEL), full(mod_l), full(k1), full(k2),
                  pl.BlockSpec((1, PEER_EB, D_MODEL), lambda t, e: (layer, jnp.minimum(e, n_e - 1), 0)),
                  pl.BlockSpec((1, 1, D_MODEL, PEER_EB), lambda t, e: (layer, jnp.clip(e - 2, 0, n_e - 1), 0, 0)),
                  full(ln_g), full(ln_b)],
        out_specs=tok(D_MODEL),
        out_shape=jax.ShapeDtypeStruct((n_tiles * PEER_TOK, D_MODEL), F32),
        scratch_shapes=[stat(2 * PEER_HEADS, F32),
                        pltpu.VMEM((2, PEER_TOPK, LANES), F32),
                        stat(PEER_HEADS, jnp.uint32), stat(PEER_HEADS, jnp.uint32),
                        stat(PEER_HEADS, BF16), stat(PEER_HEADS, BF16),
                        pltpu.VMEM((2, PEER_HEADS, PEER_EB // PEER_NKEYS, 16, LANES), BF16),
                        pltpu.VMEM((NG, PEER_EB, LANES), BF16), pltpu.VMEM((NG, PEER_EB, LANES), BF16),
                        pltpu.VMEM((NG, PEER_EB, LANES), BF16), pltpu.VMEM((NG, PEER_EB, LANES), BF16),
                        pltpu.VMEM((NG // 2, D_MODEL, 2 * LANES), F32)],
        compiler_params=pltpu.CompilerParams(dimension_semantics=("arbitrary", "arbitrary"),
                                             vmem_limit_bytes=VMEM_LIMIT),
        name="peer",
    )(h2, qp, x1, mod_l, k1, k2, u_bf, vt3, ln_g, ln_b)


def _rope_tables(seq, ctx_len):
    rows = seq // GRID_W
    row_id = jnp.repeat(jnp.arange(rows), GRID_W).astype(F32)
    col_id = jnp.tile(jnp.arange(GRID_W), rows).astype(F32)
    inv = jnp.power(ROPE_BASE, -jnp.arange(ROPE_PAIRS, dtype=F32) / ROPE_PAIRS)
    ang_r = row_id[:, None] * inv
    ang_c = col_id[:, None] * inv
    cos64 = jnp.concatenate([jnp.cos(ang_r)] * 2 + [jnp.cos(ang_c)] * 2, axis=-1)
    sin64 = jnp.concatenate([jnp.sin(ang_r)] * 2 + [jnp.sin(ang_c)] * 2, axis=-1)
    n_rep = RET_W // HEAD_DIM
    cos_tab = jnp.concatenate([jnp.tile(cos64, (1, n_rep)), jnp.ones((ctx_len, RET_W), F32)], axis=0)
    sin_tab = jnp.concatenate([jnp.tile(sin64, (1, n_rep)), jnp.zeros((ctx_len, RET_W), F32)], axis=0)
    return cos_tab, sin_tab


def _rot_partner(width):
    l = np.arange(width)
    lo = (l % 32) < 16
    partner = np.where(lo, l + 16, l - 16)
    sign = np.where(lo, -1.0, 1.0).astype(np.float32)
    return partner, sign


def _slab_perm():
    new = np.arange(ATT_QW)
    r, rem = new // LANES, new % LANES
    g, d = rem // HEAD_DIM, rem % HEAD_DIM
    return (g * GQA_GROUP + r) * HEAD_DIM + d


def _prep_w_in(w_in):
    o_rq, o_rk, o_aq, o_ak = 0, RET_W, 4 * RET_W + 2 * SGU_W, 4 * RET_W + 2 * SGU_W + ATT_QW
    slab = _slab_perm()
    aq = w_in[..., o_aq:o_aq + ATT_QW]

    def partner(block):
        p, sg = _rot_partner(block.shape[-1])
        return block[..., p] * sg

    base = jnp.concatenate([w_in[..., :o_aq], aq[..., slab], w_in[..., o_ak:]], axis=-1)
    rot = jnp.concatenate([partner(w_in[..., o_rq:o_rq + RET_W]), partner(w_in[..., o_rk:o_rk + RET_W]),
                           partner(aq)[..., slab], partner(w_in[..., o_ak:o_ak + ATT_KVW])], axis=-1)
    return jnp.concatenate([base, rot], axis=-1).astype(BF16)


def kernel(x, c, ctx, c_ctx, w_mod, b_mod, w_in, w_out, ret_decay_fwd, ret_decay_bwd, sgu_w, sgu_b, attn_sink,
           ln_mix_g, ln_mix_b, peer_wq, peer_k1, peer_k2, peer_u, peer_v, ln_ffn_g, ln_ffn_b):
    n_batch, seq, _ = x.shape
    ctx_len = ctx.shape[1]
    depth = w_in.shape[0]
    assert seq % PEER_TOK == 0 and ctx_len == TOK_TILE and (n_batch * ctx_len) % PEER_TOK == 0
    assert n_batch + 1 <= 8
    n_lat = n_batch * seq
    kw = dict(n_batch=n_batch, seq=seq)

    x_all = jnp.concatenate([x.reshape(n_lat, D_MODEL), ctx.reshape(n_batch * ctx_len, D_MODEL)], axis=0)
    cond = jnp.concatenate([c, c_ctx[None, :], jnp.zeros((8 - n_batch - 1, D_MODEL), F32)], axis=0)
    mods = _modulations(cond, w_mod, b_mod)

    cos_tab, sin_tab = _rope_tables(seq, ctx_len)
    w_ext = _prep_w_in(w_in)
    slab = _slab_perm()
    w_out_p = jnp.concatenate([w_out[:, :RET_W + SGU_W], w_out[:, RET_W + SGU_W:][:, slab]], axis=1).astype(BF16)
    w_cat = jnp.transpose(sgu_w, (0, 2, 1, 3)).reshape(depth, CHUNK, SGU_GROUPS * CHUNK).astype(BF16)
    b_tab = jnp.repeat(jnp.transpose(sgu_b, (0, 2, 1)), HEAD_DIM, axis=2)
    decays = jnp.stack([ret_decay_fwd, ret_decay_bwd], axis=1)
    wq = peer_wq.astype(BF16)
    k1 = peer_k1.astype(BF16)
    k2 = peer_k2.astype(BF16)
    u_bf = peer_u.astype(BF16)
    vt3 = jnp.transpose(peer_v.reshape(depth, PEER_N // PEER_EB, PEER_EB, D_MODEL), (0, 1, 3, 2)).astype(BF16)
    row = lambda a: a.reshape(depth, 1, D_MODEL)
    lmg, lmb, lfg, lfb = row(ln_mix_g), row(ln_mix_b), row(ln_ffn_g), row(ln_ffn_b)

    for l in range(depth):
        last = l == depth - 1
        rq, rk, rv, rg, su, sv, aq, ak, av = _inproj(x_all, mods[l], w_ext[l], cos_tab, sin_tab, **kw)
        o_f, o_b = _retention(decays[l], rq, rk, rv, ctx_len=ctx_len, **kw)
        ao = _attention(attn_sink[l], aq, ak, av, ctx_len=ctx_len, **kw)
        x1, h2, qp = _mixout(x_all, o_f, o_b, rg, su, sv, ao, mods[l], w_out_p[l], w_cat[l], b_tab[l],
                             lmg[l], lmb[l], wq[l], **kw)
        n_tiles = (n_lat if last else x_all.shape[0]) // PEER_TOK
        x_all = _peer(h2, qp, x1, mods[l], k1[l], k2[l], u_bf, vt3, lfg[l], lfb[l], layer=l, n_tiles=n_tiles, **kw)
    return x_all[:n_lat].reshape(n_batch, seq, D_MODEL)
```

```python
import functools

import numpy as np
import jax
import jax.numpy as jnp
from jax import lax
from jax.experimental import pallas as pl
from jax.experimental.pallas import tpu as pltpu

F32 = jnp.float32
BF16 = jnp.bfloat16

D_MODEL = 1024
DEPTH = 4
GRID_W = 64
HEAD_DIM = 64
ROPE_PAIRS = HEAD_DIM // 4
ROPE_BASE = 10000.0
RET_HEADS = 4
CHUNK = 128
RET_W = RET_HEADS * HEAD_DIM
RET_SCALE = HEAD_DIM ** -0.5
SGU_GROUPS = 4
SGU_W = SGU_GROUPS * HEAD_DIM
ATT_Q_HEADS = 8
ATT_KV_HEADS = 2
GQA_GROUP = ATT_Q_HEADS // ATT_KV_HEADS
ATT_QW = ATT_Q_HEADS * HEAD_DIM
ATT_KVW = ATT_KV_HEADS * HEAD_DIM
ATT_SCALE = HEAD_DIM ** -0.5
D_IN = 4 * RET_W + 2 * SGU_W + ATT_QW + 2 * ATT_KVW
D_ROT = 2 * RET_W + ATT_QW + ATT_KVW
PEER_HEADS = 8
PEER_NKEYS = 128
PEER_N = PEER_NKEYS * PEER_NKEYS
PEER_QDIM = 256
PEER_TOPK = 16
LN_EPS = 1e-5
DEEPNORM_ALPHA = (2 * DEPTH) ** 0.25

LANES = 128
TOK_TILE = 256
PEER_TOK = 512
PEER_EB = 512
VMEM_LIMIT = 56 * 1024 * 1024

NEG_INF = float("-inf")


def _ln(x):
    mu = jnp.mean(x, axis=-1, keepdims=True)
    xc = x - mu
    var = jnp.mean(xc * xc, axis=-1, keepdims=True)
    return xc * lax.rsqrt(var + LN_EPS)


def _gelu(x):
    h = 0.5 * x
    return h + h * jnp.tanh(x * (0.7978845608028654 + (0.7978845608028654 * 0.044715) * (x * x)))


def _silu(x):
    return x * (1.0 / (1.0 + jnp.exp(-x)))


def _dot(a, b):
    return jnp.dot(a, b, preferred_element_type=F32)


def _dot_nt(a, b):
    return lax.dot_general(a, b, (((1,), (1,)), ((), ())), preferred_element_type=F32)


def _dot_tn(a, b):
    return lax.dot_general(a, b, (((0,), (0,)), ((), ())), preferred_element_type=F32)


def _group_mean(z, avg):
    hi = z.astype(BF16)
    lo = (z - hi.astype(F32)).astype(BF16)
    return _dot(hi, avg) + _dot(lo, avg)


def _group_ln(x, avg):
    mu = _group_mean(x, avg)
    xc = x - mu
    var = _group_mean(xc * xc, avg)
    return xc * lax.rsqrt(var + LN_EPS)


def _group_avg_matrix(width):
    r = lax.broadcasted_iota(jnp.int32, (width, width), 0) // HEAD_DIM
    c = lax.broadcasted_iota(jnp.int32, (width, width), 1) // HEAD_DIM
    return jnp.where(r == c, 1.0 / HEAD_DIM, 0.0).astype(BF16)


def _head_mask_stack(n_heads, rows, width):
    r = lax.broadcasted_iota(jnp.int32, (n_heads * rows, width), 0) // rows
    c = lax.broadcasted_iota(jnp.int32, (n_heads * rows, width), 1) // HEAD_DIM
    return r == c


def _mod_kernel(c_ref, w_ref, b_ref, o_ref):
    s = _silu(c_ref[...])
    hi = s.astype(BF16)
    lo = (s - hi.astype(F32)).astype(BF16)
    w = w_ref[0]
    whi = w.astype(BF16)
    wlo = (w - whi.astype(F32)).astype(BF16)
    o_ref[0] = _dot(hi, whi) + _dot(lo, whi) + _dot(hi, wlo) + b_ref[0]


def _modulations(cond_rows, w_mod, b_mod):
    depth = w_mod.shape[0]
    n_rows = cond_rows.shape[0]
    col = 1024
    n_col = w_mod.shape[2] // col
    return pl.pallas_call(
        _mod_kernel,
        grid=(depth, n_col),
        in_specs=[
            pl.BlockSpec((n_rows, D_MODEL), lambda l, j: (0, 0)),
            pl.BlockSpec((1, D_MODEL, col), lambda l, j: (l, 0, j)),
            pl.BlockSpec((1, 1, col), lambda l, j: (l, 0, j)),
        ],
        out_specs=pl.BlockSpec((1, n_rows, col), lambda l, j: (l, 0, j)),
        out_shape=jax.ShapeDtypeStruct((depth, n_rows, w_mod.shape[2]), F32),
        name="adaln_mod",
    )(cond_rows, w_mod, b_mod.reshape(depth, 1, -1))


def _inproj_kernel(x_ref, mod_ref, w_ref, cos_ref, sin_ref,
                   rq_ref, rk_ref, rv_ref, rg_ref, su_ref, sv_ref, aq_ref, ak_ref, av_ref,
                   *, tiles_per_batch, n_batch):
    i = pl.program_id(0)
    grp = jnp.minimum(i // tiles_per_batch, n_batch)
    shift = mod_ref[pl.ds(grp, 1), 0:D_MODEL]
    scale = mod_ref[pl.ds(grp, 1), D_MODEL:2 * D_MODEL]
    h = (_ln(x_ref[...]) * (1.0 + scale) + shift).astype(BF16)
    cos = cos_ref[...]
    sin = sin_ref[...]

    def proj(c0, w):
        return _dot(h, w_ref[:, c0:c0 + w])

    rot0 = D_IN
    rq_ref[...] = (proj(0, RET_W) * cos + proj(rot0, RET_W) * sin).astype(BF16)
    rk_ref[...] = ((proj(RET_W, RET_W) * cos + proj(rot0 + RET_W, RET_W) * sin) * RET_SCALE).astype(BF16)
    rv_ref[...] = proj(2 * RET_W, RET_W).astype(BF16)
    rg_ref[...] = proj(3 * RET_W, RET_W)
    su_ref[...] = proj(4 * RET_W, SGU_W)
    sv_ref[...] = proj(4 * RET_W + SGU_W, SGU_W)
    aq0 = 4 * RET_W + 2 * SGU_W
    aqr = rot0 + 2 * RET_W
    for half in range(ATT_QW // RET_W):
        o = half * RET_W
        aq_ref[:, o:o + RET_W] = ((proj(aq0 + o, RET_W) * cos + proj(aqr + o, RET_W) * sin) * ATT_SCALE).astype(BF16)
    ak0 = aq0 + ATT_QW
    akr = aqr + ATT_QW
    ak_ref[...] = (proj(ak0, ATT_KVW) * cos[:, 0:ATT_KVW] + proj(akr, ATT_KVW) * sin[:, 0:ATT_KVW]).astype(BF16)
    av_ref[...] = proj(ak0 + ATT_KVW, ATT_KVW).astype(BF16)


def _inproj(x_all, mod_l, w_ext, cos_tab, sin_tab, *, n_batch, seq):
    n_tok = x_all.shape[0]
    n_tiles = n_tok // TOK_TILE
    tiles_per_batch = seq // TOK_TILE
    n_lat_tiles = n_batch * tiles_per_batch

    def tab_map(i):
        return (jnp.where(i < n_lat_tiles, i % tiles_per_batch, tiles_per_batch), 0)

    tok = lambda w: pl.BlockSpec((TOK_TILE, w), lambda i: (i, 0))
    full = lambda a: pl.BlockSpec(a.shape, lambda i: (0,) * a.ndim)
    out_w = [(RET_W, BF16), (RET_W, BF16), (RET_W, BF16), (RET_W, F32), (SGU_W, F32), (SGU_W, F32),
             (ATT_QW, BF16), (ATT_KVW, BF16), (ATT_KVW, BF16)]
    return pl.pallas_call(
        functools.partial(_inproj_kernel, tiles_per_batch=tiles_per_batch, n_batch=n_batch),
        grid=(n_tiles,),
        in_specs=[tok(D_MODEL), full(mod_l), full(w_ext),
                  pl.BlockSpec((TOK_TILE, RET_W), tab_map), pl.BlockSpec((TOK_TILE, RET_W), tab_map)],
        out_specs=[tok(w) for w, _ in out_w],
        out_shape=[jax.ShapeDtypeStruct((n_tok, w), dt) for w, dt in out_w],
        compiler_params=pltpu.CompilerParams(dimension_semantics=("arbitrary",), vmem_limit_bytes=VMEM_LIMIT),
        name="inproj",
    )(x_all, mod_l, w_ext, cos_tab, sin_tab)


def _ret_kernel(dec_ref, qf_ref, kf_ref, vf_ref, qb_ref, kb_ref, vb_ref, of_ref, ob_ref,
                sf_ref, sb_ref, intra_ref, qd_ref, kd_ref, cd_ref):
    b = pl.program_id(0)
    s = pl.program_id(1)
    C = CHUNK
    W = RET_W

    @pl.when((b == 0) & (s == 0))
    def _tables():
        dec = dec_ref[...]
        lg = jnp.minimum(dec, 0.0) - jnp.log(1.0 + jnp.exp(-jnp.abs(dec)))
        lane_head = lax.broadcasted_iota(jnp.int32, (1, W), 1) // HEAD_DIM
        ii = lax.broadcasted_iota(jnp.int32, (C, C), 0)
        jj = lax.broadcasted_iota(jnp.int32, (C, C), 1)
        ri = lax.broadcasted_iota(jnp.int32, (C, W), 0).astype(F32)
        rb = lax.broadcasted_iota(jnp.int32, (W, W), 0) // HEAD_DIM
        cb = lax.broadcasted_iota(jnp.int32, (W, W), 1) // HEAD_DIM
        for d in range(2):
            lgl = jnp.zeros((1, W), F32)
            for hh in range(RET_HEADS):
                lg_h = lg[d:d + 1, hh:hh + 1]
                lgl = lgl + jnp.where(lane_head == hh, lg_h, 0.0)
                rel = (ii - jj) if d == 0 else (jj - ii)
                m = jnp.exp(jnp.maximum(rel, 0).astype(F32) * lg_h)
                intra_ref[d, hh * C:(hh + 1) * C, :] = jnp.where(rel >= 0, m, 0.0)
            if d == 0:
                qd_ref[d] = jnp.exp((ri + 1.0) * lgl)
                kd_ref[d] = jnp.exp((C - 1.0 - ri) * lgl)
            else:
                qd_ref[d] = jnp.exp((C - ri) * lgl)
                kd_ref[d] = jnp.exp(ri * lgl)
            cd_ref[d] = jnp.where(rb == cb, jnp.exp(C * lgl), 0.0)

    @pl.when(s == 0)
    def _zero():
        sf_ref[...] = jnp.zeros_like(sf_ref)
        sb_ref[...] = jnp.zeros_like(sb_ref)

    hm = _head_mask_stack(RET_HEADS, C, W)
    rb = lax.broadcasted_iota(jnp.int32, (W, W), 0) // HEAD_DIM
    cb = lax.broadcasted_iota(jnp.int32, (W, W), 1) // HEAD_DIM
    bd = rb == cb
    zero_b = jnp.zeros((), BF16)

    def direction(d, q_ref, k_ref, v_ref, st_ref, o_ref):
        q = q_ref[...]
        k = k_ref[...]
        v = v_ref[...]
        qs = jnp.where(hm, jnp.concatenate([q] * RET_HEADS, axis=0), zero_b)
        att = (_dot_nt(qs, k) * intra_ref[d]).astype(BF16)
        att = jnp.concatenate([att[hh * C:(hh + 1) * C] for hh in range(RET_HEADS)], axis=1)
        vs = jnp.where(hm, jnp.concatenate([v] * RET_HEADS, axis=0), zero_b)
        st = st_ref[...]
        o = _dot(att, vs) + _dot(q, st.astype(BF16)) * qd_ref[d]
        o_ref[...] = o
        kdec = (k.astype(F32) * kd_ref[d]).astype(BF16)
        st_ref[...] = st * cd_ref[d] + jnp.where(bd, _dot_tn(kdec, v), 0.0)

    direction(0, qf_ref, kf_ref, vf_ref, sf_ref, of_ref)
    direction(1, qb_ref, kb_ref, vb_ref, sb_ref, ob_ref)


def _retention(decays, rq, rk, rv, *, n_batch, seq, ctx_len):
    n_tok = rq.shape[0]
    nl = seq // CHUNK
    nc = ctx_len // CHUNK
    ctx0 = n_batch * nl

    def fwd_map(b, s):
        return (jnp.where(s < nc, ctx0 + b * nc + s, b * nl + (s - nc)), 0)

    def bwd_map(b, s):
        return (jnp.where(s < nc, ctx0 + b * nc + (nc - 1 - s), b * nl + (nl - 1 - (s - nc))), 0)

    fspec = pl.BlockSpec((CHUNK, RET_W), fwd_map)
    bspec = pl.BlockSpec((CHUNK, RET_W), bwd_map)
    return pl.pallas_call(
        _ret_kernel,
        grid=(n_batch, nc + nl),
        in_specs=[pl.BlockSpec(decays.shape, lambda b, s: (0, 0)), fspec, fspec, fspec, bspec, bspec, bspec],
        out_specs=[fspec, bspec],
        out_shape=[jax.ShapeDtypeStruct((n_tok, RET_W), F32)] * 2,
        scratch_shapes=[
            pltpu.VMEM((RET_W, RET_W), F32), pltpu.VMEM((RET_W, RET_W), F32),
            pltpu.VMEM((2, RET_HEADS * CHUNK, CHUNK), F32),
            pltpu.VMEM((2, CHUNK, RET_W), F32), pltpu.VMEM((2, CHUNK, RET_W), F32),
            pltpu.VMEM((2, RET_W, RET_W), F32),
        ],
        compiler_params=pltpu.CompilerParams(dimension_semantics=("arbitrary", "arbitrary")),
        name="retention",
    )(decays, rq, rk, rv, rq, rk, rv)


def _attn_kernel(sink_ref, q_ref, kp_ref, kc_ref, kn_ref, vp_ref, vc_ref, vn_ref, kx_ref, vx_ref, o_ref,
                 *, n_lat_blocks, seq):
    n = pl.program_id(1)
    W = CHUNK
    qi = lax.broadcasted_iota(jnp.int32, (W, 3 * W), 0)
    kj = lax.broadcasted_iota(jnp.int32, (W, 3 * W), 1)
    rel = kj - W - qi
    kpos = n * W - W + kj
    mask = (jnp.abs(rel) <= W) & (kpos >= 0) & (kpos < seq) & (n < n_lat_blocks)
    keys = jnp.concatenate([kp_ref[...], kc_ref[...], kn_ref[...], kx_ref[...]], axis=0)
    vals = jnp.concatenate([vp_ref[...], vc_ref[...], vn_ref[...], vx_ref[...]], axis=0)
    lane_head = lax.broadcasted_iota(jnp.int32, (1, LANES), 1) // HEAD_DIM
    zero_b = jnp.zeros((), BF16)
    slabs = [q_ref[:, r * LANES:(r + 1) * LANES] for r in range(GQA_GROUP)]
    mask4 = jnp.concatenate([mask] * GQA_GROUP, axis=0)
    outs = [jnp.zeros((W, LANES), F32) for _ in range(GQA_GROUP)]
    for g in range(ATT_KV_HEADS):
        mg = lane_head == g
        qs = jnp.concatenate([jnp.where(mg, s, zero_b) for s in slabs], axis=0)
        sink = jnp.concatenate([jnp.full((W, 1), sink_ref[g * GQA_GROUP + r], F32) for r in range(GQA_GROUP)], axis=0)
        sc = _dot_nt(qs, keys)
        s_loc = jnp.where(mask4, sc[:, 0:3 * W], NEG_INF)
        s_ctx = sc[:, 3 * W:]
        m = jnp.maximum(jnp.maximum(jnp.max(s_loc, axis=-1, keepdims=True),
                                    jnp.max(s_ctx, axis=-1, keepdims=True)), sink)
        p_loc = jnp.exp(s_loc - m)
        p_ctx = jnp.exp(s_ctx - m)
        den = (jnp.sum(p_loc, axis=-1, keepdims=True) + jnp.sum(p_ctx, axis=-1, keepdims=True)
               + jnp.exp(sink - m))
        p = jnp.concatenate([p_loc, p_ctx], axis=1).astype(BF16)
        o = _dot(p, vals) * (1.0 / den)
        for r in range(GQA_GROUP):
            outs[r] = outs[r] + jnp.where(mg, o[r * W:(r + 1) * W], 0.0)
    for r in range(GQA_GROUP):
        o_ref[:, r * LANES:(r + 1) * LANES] = outs[r].astype(BF16)


def _attention(sink, aq, ak, av, *, n_batch, seq, ctx_len):
    n_tok = aq.shape[0]
    nl = seq // CHUNK
    nc = ctx_len // CHUNK
    ctx0 = n_batch * nl

    def q_map(b, n):
        return (jnp.where(n < nl, b * nl + n, ctx0 + b * nc + (n - nl)), 0)

    def k_map(off):
        def f(b, n):
            return (b * nl + jnp.clip(n + off, 0, nl - 1), 0)
        return f

    x_map = lambda b, n: (n_batch * seq // ctx_len + b, 0)
    kv = lambda off: pl.BlockSpec((CHUNK, ATT_KVW), k_map(off))
    xspec = pl.BlockSpec((ctx_len, ATT_KVW), x_map)
    return pl.pallas_call(
        functools.partial(_attn_kernel, n_lat_blocks=nl, seq=seq),
        grid=(n_batch, nl + nc),
        in_specs=[pl.BlockSpec(memory_space=pltpu.SMEM), pl.BlockSpec((CHUNK, ATT_QW), q_map),
                  kv(-1), kv(0), kv(1), kv(-1), kv(0), kv(1), xspec, xspec],
        out_specs=pl.BlockSpec((CHUNK, ATT_QW), q_map),
        out_shape=jax.ShapeDtypeStruct((n_tok, ATT_QW), BF16),
        compiler_params=pltpu.CompilerParams(dimension_semantics=("arbitrary", "arbitrary")),
        name="window_attn",
    )(sink, aq, ak, ak, ak, av, av, av, ak, av)


def _mixout_kernel(x_ref, of_ref, ob_ref, rg_ref, su_ref, sv_ref, ao_ref, mod_ref, wout_ref, wcat_ref, bs_ref,
                   lng_ref, lnb_ref, wq_ref, x1_ref, h2t_ref, qp_ref, *, tiles_per_batch, n_batch):
    i = pl.program_id(0)
    grp = jnp.minimum(i // tiles_per_batch, n_batch)
    avg = _group_avg_matrix(RET_W)
    a = _group_ln(of_ref[...] + ob_ref[...], avg) * _silu(rg_ref[...])
    u = _gelu(su_ref[...])
    vn = _group_ln(_gelu(sv_ref[...]), avg)
    hm = _head_mask_stack(SGU_GROUPS, CHUNK, SGU_W)
    zero_b = jnp.zeros((), BF16)
    parts = []
    for c in range(TOK_TILE // CHUNK):
        vc = vn[c * CHUNK:(c + 1) * CHUNK].astype(BF16)
        vs = jnp.where(hm, jnp.concatenate([vc] * SGU_GROUPS, axis=0), zero_b)
        mix = _dot(wcat_ref[...], vs) + bs_ref[...]
        parts.append(u[c * CHUNK:(c + 1) * CHUNK] * mix)
    bmix = jnp.concatenate(parts, axis=0)
    y = (_dot(a.astype(BF16), wout_ref[0:RET_W, :])
         + _dot(bmix.astype(BF16), wout_ref[RET_W:RET_W + SGU_W, :])
         + _dot(ao_ref[...], wout_ref[RET_W + SGU_W:, :]))
    gate1 = mod_ref[pl.ds(grp, 1), 2 * D_MODEL:3 * D_MODEL]
    x1 = _ln(DEEPNORM_ALPHA * x_ref[...] + gate1 * y) * lng_ref[...] + lnb_ref[...]
    x1_ref[...] = x1
    shift2 = mod_ref[pl.ds(grp, 1), 3 * D_MODEL:4 * D_MODEL]
    scale2 = mod_ref[pl.ds(grp, 1), 4 * D_MODEL:5 * D_MODEL]
    h2f = _ln(x1) * (1.0 + scale2) + shift2
    h2t_ref[0] = h2f.T.astype(BF16)
    qp_ref[...] = _dot(h2f.astype(BF16), wq_ref[...]).astype(BF16)


def _mixout(x_all, o_f, o_b, rg, su, sv, ao, mod_l, w_out, w_cat, b_tab, ln_g, ln_b, wq, *, n_batch, seq):
    n_tok = x_all.shape[0]
    tiles_per_batch = seq // TOK_TILE
    tok = lambda w: pl.BlockSpec((TOK_TILE, w), lambda i: (i, 0))
    full = lambda a: pl.BlockSpec(a.shape, lambda i: (0,) * a.ndim)
    n_q = wq.shape[1]
    return pl.pallas_call(
        functools.partial(_mixout_kernel, tiles_per_batch=tiles_per_batch, n_batch=n_batch),
        grid=(n_tok // TOK_TILE,),
        in_specs=[tok(D_MODEL), tok(RET_W), tok(RET_W), tok(RET_W), tok(SGU_W), tok(SGU_W), tok(ATT_QW),
                  full(mod_l), full(w_out), full(w_cat), full(b_tab), full(ln_g), full(ln_b), full(wq)],
        out_specs=[tok(D_MODEL), pl.BlockSpec((1, D_MODEL, TOK_TILE), lambda i: (i, 0, 0)), tok(n_q)],
        out_shape=[jax.ShapeDtypeStruct((n_tok, D_MODEL), F32),
                   jax.ShapeDtypeStruct((n_tok // TOK_TILE, D_MODEL, TOK_TILE), BF16),
                   jax.ShapeDtypeStruct((n_tok, n_q), BF16)],
        compiler_params=pltpu.CompilerParams(dimension_semantics=("arbitrary",), vmem_limit_bytes=VMEM_LIMIT),
        name="mix_out",
    )(x_all, o_f, o_b, rg, su, sv, ao, mod_l, w_out, w_cat, b_tab, ln_g, ln_b, wq)


_CAND_SEGS = [(a, PEER_TOPK // (a + 1)) for a in range(1, 8)]


def _dup_bf16_words(x):
    hi = pltpu.bitcast(x.astype(BF16).astype(F32), jnp.uint32)
    return hi | (hi >> 16)


def _peer_kernel(h2t_ref, qp_ref, x1_ref, mod_ref, k1_ref, k2_ref, u_ref, vt_ref, lng_ref, lnb_ref, out_ref,
                 s_sc, v_sc, n_sc, a_sc, rk_sc, b_sc, row_sc, act0_sc, act1_sc, hs0_sc, hs1_sc, acc_sc,
                 *, tiles_per_batch, n_batch, n_blocks):
    t = pl.program_id(0)
    e = pl.program_id(1)
    NG = PEER_TOK // LANES
    K = PEER_TOPK
    half = PEER_QDIM // 2

    @pl.when(e == 0)
    def _prologue():
        acc_sc[...] = jnp.zeros_like(acc_sc)
        act1_sc[...] = jnp.zeros_like(act1_sc)
        hs1_sc[...] = jnp.zeros_like(hs1_sc)
        for h in range(PEER_HEADS):
            for p, kref in ((0, k1_ref), (1, k2_ref)):
                c0 = h * PEER_QDIM + p * half
                sT = _dot_nt(kref[...], qp_ref[:, c0:c0 + half])
                for g in range(NG):
                    s_sc[2 * h + p, g] = sT[:, g * LANES:(g + 1) * LANES]

        row8 = lax.broadcasted_iota(jnp.int32, (8, LANES), 0)

        def stats(it, carry):
            h = it // NG
            g = it % NG
            s1 = s_sc[2 * h, g]
            s2 = s_sc[2 * h + 1, g]
            cur1, cur2 = s1, s2
            rank2 = jnp.full(s2.shape, float(K), F32)
            for r in range(K):
                m1 = jnp.max(cur1, axis=0, keepdims=True)
                m2 = jnp.max(cur2, axis=0, keepdims=True)
                v_sc[0, r:r + 1, :] = m1
                v_sc[1, r:r + 1, :] = m2
                cur1 = jnp.where(cur1 >= m1, NEG_INF, cur1)
                hit2 = cur2 >= m2
                rank2 = jnp.where(hit2, float(r), rank2)
                cur2 = jnp.where(hit2, NEG_INF, cur2)
            v1 = v_sc[0]
            v2 = v_sc[1]
            segs = [v1[0:1] + v2]
            for a, n_a in _CAND_SEGS:
                segs.append(jnp.where(row8 < n_a, v1[a:a + 1] + v2[0:8], NEG_INF))
            segs.append(v1[8:16] + v2[0:1])
            cand = jnp.concatenate(segs, axis=0)
            cur = cand
            tau = None
            for r in range(K):
                tau = jnp.max(cur, axis=0, keepdims=True)
                cur = jnp.where(cur >= tau, NEG_INF, cur)
            cmax = v1[0:1] + v2[0:1]
            z = jnp.sum(jnp.where(cand >= tau, jnp.exp(cand - cmax), 0.0), axis=0, keepdims=True)
            n = jnp.zeros(s1.shape, F32)
            for bb in range(K):
                n = n + jnp.where((s1 + v2[bb:bb + 1]) >= tau, 1.0, 0.0)
            n_sc[h, g] = _dup_bf16_words(n)
            a_sc[h, g] = _dup_bf16_words(jnp.exp(s1 - v1[0:1]) * (1.0 / z))
            rk_sc[h, g] = rank2.astype(BF16)
            b_sc[h, g] = jnp.exp(s2 - v2[0:1]).astype(BF16)
            return carry

        lax.fori_loop(0, PEER_HEADS * NG, stats, 0)

    rows_per_blk = PEER_EB // PEER_NKEYS
    kc_rows = 16
    zero_b = jnp.zeros((), BF16)
    blk = jnp.where(e == 0, 1, jnp.where(e == n_blocks + 1, n_blocks - 2, e - 1))
    tile0 = pl.multiple_of((blk // 2) * 8, 8)

    def step(act_w, act_r, hs_w, hs_r, row_off):
        def half(c, carry):
            a_new = _gelu(_dot(u_ref[0], h2t_ref[c])).astype(BF16)
            act_w[2 * c] = a_new[:, 0:LANES]
            act_w[2 * c + 1] = a_new[:, LANES:2 * LANES]
            hs_prev = jnp.concatenate([hs_r[2 * c], hs_r[2 * c + 1]], axis=1)
            acc_sc[c] += _dot(vt_ref[0, 0], hs_prev)
            gate_group(2 * c)
            gate_group(2 * c + 1)
            return carry

        def gate_group(g):
            for h in range(PEER_HEADS):
                n_tile = n_sc[h, g, pl.ds(tile0, 8), :]
                a_tile = a_sc[h, g, pl.ds(tile0, 8), :]
                for r in range(rows_per_blk):
                    k = row_off + r
                    row_sc[0, h, r] = pltpu.bitcast(jnp.broadcast_to(n_tile[k:k + 1, :], (8, LANES)), BF16)
                    row_sc[1, h, r] = pltpu.bitcast(jnp.broadcast_to(a_tile[k:k + 1, :], (8, LANES)), BF16)
            for kc in range(PEER_NKEYS // kc_rows):
                keys = slice(kc * kc_rows, (kc + 1) * kc_rows)
                gates = [jnp.zeros((kc_rows, LANES), BF16) for _ in range(rows_per_blk)]
                for h in range(PEER_HEADS):
                    rk = rk_sc[h, g, keys, :]
                    bb = b_sc[h, g, keys, :]
                    for r in range(rows_per_blk):
                        n_full = jnp.concatenate([row_sc[0, h, r]] * (kc_rows // 16), axis=0)
                        a_full = jnp.concatenate([row_sc[1, h, r]] * (kc_rows // 16), axis=0)
                        gates[r] = gates[r] + jnp.where(rk < n_full, a_full * bb, zero_b)
                for r in range(rows_per_blk):
                    rows = slice(r * PEER_NKEYS + kc * kc_rows, r * PEER_NKEYS + (kc + 1) * kc_rows)
                    hs_w[g, rows, :] = gates[r] * act_r[g, rows, :]

        lax.fori_loop(0, NG // 2, half, 0)

    @pl.when(e % 2 == 0)
    def _even():
        step(act0_sc, act1_sc, hs0_sc, hs1_sc, rows_per_blk)

    @pl.when(e % 2 == 1)
    def _odd():
        step(act1_sc, act0_sc, hs1_sc, hs0_sc, 0)

    @pl.when(e == n_blocks + 1)
    def _epilogue():
        grp = jnp.minimum(t // tiles_per_batch, n_batch)
        gate2 = mod_ref[pl.ds(grp, 1), 5 * D_MODEL:6 * D_MODEL]
        f = jnp.concatenate([acc_sc[c].T for c in range(NG // 2)], axis=0)
        out_ref[...] = _ln(DEEPNORM_ALPHA * x1_ref[...] + gate2 * f) * lng_ref[...] + lnb_ref[...]


def _peer(h2, qp, x1, mod_l, k1, k2, u_bf, vt3, ln_g, ln_b, *, layer, n_batch, seq, n_tiles):
    tiles_per_batch = seq // PEER_TOK
    n_e = PEER_N // PEER_EB
    NG = PEER_TOK // LANES
    tok = lambda w: pl.BlockSpec((PEER_TOK, w), lambda t, e: (t, 0))
    full = lambda a: pl.BlockSpec(a.shape, lambda t, e: (0,) * a.ndim)
    stat = lambda n, dt: pltpu.VMEM((n, NG, PEER_NKEYS, LANES), dt)
    return pl.pallas_call(
        functools.partial(_peer_kernel, tiles_per_batch=tiles_per_batch, n_batch=n_batch, n_blocks=n_e),
        grid=(n_tiles, n_e + 2),
        in_specs=[pl.BlockSpec((PEER_TOK // TOK_TILE, D_MODEL, TOK_TILE), lambda t, e: (t, 0, 0)),
                  tok(qp.shape[1]), tok(D_MODEL), full(mod_l), full(k1), full(k2),
                  pl.BlockSpec((1, PEER_EB, D_MODEL), lambda t, e: (layer, jnp.minimum(e, n_e - 1), 0)),
                  pl.BlockSpec((1, 1, D_MODEL, PEER_EB), lambda t, e: (layer, jnp.clip(e - 2, 0, n_e - 1), 0, 0)),
                  full(ln_g), full(ln_b)],
        out_specs=tok(D_MODEL),
        out_shape=jax.ShapeDtypeStruct((n_tiles * PEER_TOK, D_MODEL), F32),
        scratch_shapes=[stat(2 * PEER_HEADS, F32),
                        pltpu.VMEM((2, PEER_TOPK, LANES), F32),
                        stat(PEER_HEADS, jnp.uint32), stat(PEER_HEADS, jnp.uint32),
                        stat(PEER_HEADS, BF16), stat(PEER_HEADS, BF16),
                        pltpu.VMEM((2, PEER_HEADS, PEER_EB // PEER_NKEYS, 16, LANES), BF16),
                        pltpu.VMEM((NG, PEER_EB, LANES), BF16), pltpu.VMEM((NG, PEER_EB, LANES), BF16),
                        pltpu.VMEM((NG, PEER_EB, LANES), BF16), pltpu.VMEM((NG, PEER_EB, LANES), BF16),
                        pltpu.VMEM((NG // 2, D_MODEL, 2 * LANES), F32)],
        compiler_params=pltpu.CompilerParams(dimension_semantics=("arbitrary", "arbitrary"),
                                             vmem_limit_bytes=VMEM_LIMIT),
        name="peer",
    )(h2, qp, x1, mod_l, k1, k2, u_bf, vt3, ln_g, ln_b)


def _rope_tables(seq, ctx_len):
    rows = seq // GRID_W
    row_id = jnp.repeat(jnp.arange(rows), GRID_W).astype(F32)
    col_id = jnp.tile(jnp.arange(GRID_W), rows).astype(F32)
    inv = jnp.power(ROPE_BASE, -jnp.arange(ROPE_PAIRS, dtype=F32) / ROPE_PAIRS)
    ang_r = row_id[:, None] * inv
    ang_c = col_id[:, None] * inv
    cos64 = jnp.concatenate([jnp.cos(ang_r)] * 2 + [jnp.cos(ang_c)] * 2, axis=-1)
    sin64 = jnp.concatenate([jnp.sin(ang_r)] * 2 + [jnp.sin(ang_c)] * 2, axis=-1)
    n_rep = RET_W // HEAD_DIM
    cos_tab = jnp.concatenate([jnp.tile(cos64, (1, n_rep)), jnp.ones((ctx_len, RET_W), F32)], axis=0)
    sin_tab = jnp.concatenate([jnp.tile(sin64, (1, n_rep)), jnp.zeros((ctx_len, RET_W), F32)], axis=0)
    return cos_tab, sin_tab


def _rot_partner(width):
    l = np.arange(width)
    lo = (l % 32) < 16
    partner = np.where(lo, l + 16, l - 16)
    sign = np.where(lo, -1.0, 1.0).astype(np.float32)
    return partner, sign


def _slab_perm():
    new = np.arange(ATT_QW)
    r, rem = new // LANES, new % LANES
    g, d = rem // HEAD_DIM, rem % HEAD_DIM
    return (g * GQA_GROUP + r) * HEAD_DIM + d


def _prep_w_in(w_in):
    o_rq, o_rk, o_aq, o_ak = 0, RET_W, 4 * RET_W + 2 * SGU_W, 4 * RET_W + 2 * SGU_W + ATT_QW
    slab = _slab_perm()
    aq = w_in[..., o_aq:o_aq + ATT_QW]

    def partner(block):
        p, sg = _rot_partner(block.shape[-1])
        return block[..., p] * sg

    base = jnp.concatenate([w_in[..., :o_aq], aq[..., slab], w_in[..., o_ak:]], axis=-1)
    rot = jnp.concatenate([partner(w_in[..., o_rq:o_rq + RET_W]), partner(w_in[..., o_rk:o_rk + RET_W]),
                           partner(aq)[..., slab], partner(w_in[..., o_ak:o_ak + ATT_KVW])], axis=-1)
    return jnp.concatenate([base, rot], axis=-1).astype(BF16)


def kernel(x, c, ctx, c_ctx, w_mod, b_mod, w_in, w_out, ret_decay_fwd, ret_decay_bwd, sgu_w, sgu_b, attn_sink,
           ln_mix_g, ln_mix_b, peer_wq, peer_k1, peer_k2, peer_u, peer_v, ln_ffn_g, ln_ffn_b):
    n_batch, seq, _ = x.shape
    ctx_len = ctx.shape[1]
    depth = w_in.shape[0]
    assert seq % PEER_TOK == 0 and ctx_len == TOK_TILE and (n_batch * ctx_len) % PEER_TOK == 0
    assert n_batch + 1 <= 8
    n_lat = n_batch * seq
    kw = dict(n_batch=n_batch, seq=seq)

    x_all = jnp.concatenate([x.reshape(n_lat, D_MODEL), ctx.reshape(n_batch * ctx_len, D_MODEL)], axis=0)
    cond = jnp.concatenate([c, c_ctx[None, :], jnp.zeros((8 - n_batch - 1, D_MODEL), F32)], axis=0)
    mods = _modulations(cond, w_mod, b_mod)

    cos_tab, sin_tab = _rope_tables(seq, ctx_len)
    w_ext = _prep_w_in(w_in)
    slab = _slab_perm()
    w_out_p = jnp.concatenate([w_out[:, :RET_W + SGU_W], w_out[:, RET_W + SGU_W:][:, slab]], axis=1).astype(BF16)
    w_cat = jnp.transpose(sgu_w, (0, 2, 1, 3)).reshape(depth, CHUNK, SGU_GROUPS * CHUNK).astype(BF16)
    b_tab = jnp.repeat(jnp.transpose(sgu_b, (0, 2, 1)), HEAD_DIM, axis=2)
    decays = jnp.stack([ret_decay_fwd, ret_decay_bwd], axis=1)
    wq = peer_wq.astype(BF16)
    k1 = peer_k1.astype(BF16)
    k2 = peer_k2.astype(BF16)
    u_bf = peer_u.astype(BF16)
    vt3 = jnp.transpose(peer_v.reshape(depth, PEER_N // PEER_EB, PEER_EB, D_MODEL), (0, 1, 3, 2)).astype(BF16)
    row = lambda a: a.reshape(depth, 1, D_MODEL)
    lmg, lmb, lfg, lfb = row(ln_mix_g), row(ln_mix_b), row(ln_ffn_g), row(ln_ffn_b)

    for l in range(depth):
        last = l == depth - 1
        rq, rk, rv, rg, su, sv, aq, ak, av = _inproj(x_all, mods[l], w_ext[l], cos_tab, sin_tab, **kw)
        o_f, o_b = _retention(decays[l], rq, rk, rv, ctx_len=ctx_len, **kw)
        ao = _attention(attn_sink[l], aq, ak, av, ctx_len=ctx_len, **kw)
        x1, h2, qp = _mixout(x_all, o_f, o_b, rg, su, sv, ao, mods[l], w_out_p[l], w_cat[l], b_tab[l],
                             lmg[l], lmb[l], wq[l], **kw)
        n_tiles = (n_lat if last else x_all.shape[0]) // PEER_TOK
        x_all = _peer(h2, qp, x1, mods[l], k1[l], k2[l], u_bf, vt3, lfg[l], lfb[l], layer=l, n_tiles=n_tiles, **kw)
    return x_all[:n_lat].reshape(n_batch, seq, D_MODEL)
```

```python
import functools

import numpy as np
import jax
import jax.numpy as jnp
from jax import lax
from jax.experimental import pallas as pl
from jax.experimental.pallas import tpu as pltpu

F32 = jnp.float32
BF16 = jnp.bfloat16

D_MODEL = 1024
DEPTH = 4
GRID_W = 64
HEAD_DIM = 64
ROPE_PAIRS = HEAD_DIM // 4
ROPE_BASE = 10000.0
RET_HEADS = 4
CHUNK = 128
RET_W = RET_HEADS * HEAD_DIM
RET_SCALE = HEAD_DIM ** -0.5
SGU_GROUPS = 4
SGU_W = SGU_GROUPS * HEAD_DIM
ATT_Q_HEADS = 8
ATT_KV_HEADS = 2
GQA_GROUP = ATT_Q_HEADS // ATT_KV_HEADS
ATT_QW = ATT_Q_HEADS * HEAD_DIM
ATT_KVW = ATT_KV_HEADS * HEAD_DIM
ATT_SCALE = HEAD_DIM ** -0.5
D_IN = 4 * RET_W + 2 * SGU_W + ATT_QW + 2 * ATT_KVW
D_ROT = 2 * RET_W + ATT_QW + ATT_KVW
PEER_HEADS = 8
PEER_NKEYS = 128
PEER_N = PEER_NKEYS * PEER_NKEYS
PEER_QDIM = 256
PEER_TOPK = 16
LN_EPS = 1e-5
DEEPNORM_ALPHA = (2 * DEPTH) ** 0.25

LANES = 128
TOK_TILE = 256
PEER_TOK = 512
PEER_EB = 512
VMEM_LIMIT = 56 * 1024 * 1024

NEG_INF = float("-inf")


def _ln(x):
    mu = jnp.mean(x, axis=-1, keepdims=True)
    xc = x - mu
    var = jnp.mean(xc * xc, axis=-1, keepdims=True)
    return xc * lax.rsqrt(var + LN_EPS)


def _gelu(x):
    h = 0.5 * x
    return h + h * jnp.tanh(x * (0.7978845608028654 + (0.7978845608028654 * 0.044715) * (x * x)))


def _silu(x):
    return x * (1.0 / (1.0 + jnp.exp(-x)))


def _dot(a, b):
    return jnp.dot(a, b, preferred_element_type=F32)


def _dot_nt(a, b):
    return lax.dot_general(a, b, (((1,), (1,)), ((), ())), preferred_element_type=F32)


def _dot_tn(a, b):
    return lax.dot_general(a, b, (((0,), (0,)), ((), ())), preferred_element_type=F32)


def _group_mean(z, avg):
    hi = z.astype(BF16)
    lo = (z - hi.astype(F32)).astype(BF16)
    return _dot(hi, avg) + _dot(lo, avg)


def _group_ln(x, avg):
    mu = _group_mean(x, avg)
    xc = x - mu
    var = _group_mean(xc * xc, avg)
    return xc * lax.rsqrt(var + LN_EPS)


def _group_avg_matrix(width):
    r = lax.broadcasted_iota(jnp.int32, (width, width), 0) // HEAD_DIM
    c = lax.broadcasted_iota(jnp.int32, (width, width), 1) // HEAD_DIM
    return jnp.where(r == c, 1.0 / HEAD_DIM, 0.0).astype(BF16)


def _head_mask_stack(n_heads, rows, width):
    r = lax.broadcasted_iota(jnp.int32, (n_heads * rows, width), 0) // rows
    c = lax.broadcasted_iota(jnp.int32, (n_heads * rows, width), 1) // HEAD_DIM
    return r == c


def _mod_kernel(c_ref, w_ref, b_ref, o_ref):
    s = _silu(c_ref[...])
    hi = s.astype(BF16)
    lo = (s - hi.astype(F32)).astype(BF16)
    w = w_ref[0]
    whi = w.astype(BF16)
    wlo = (w - whi.astype(F32)).astype(BF16)
    o_ref[0] = _dot(hi, whi) + _dot(lo, whi) + _dot(hi, wlo) + b_ref[0]


def _modulations(cond_rows, w_mod, b_mod):
    depth = w_mod.shape[0]
    n_rows = cond_rows.shape[0]
    col = 1024
    n_col = w_mod.shape[2] // col
    return pl.pallas_call(
        _mod_kernel,
        grid=(depth, n_col),
        in_specs=[
            pl.BlockSpec((n_rows, D_MODEL), lambda l, j: (0, 0)),
            pl.BlockSpec((1, D_MODEL, col), lambda l, j: (l, 0, j)),
            pl.BlockSpec((1, 1, col), lambda l, j: (l, 0, j)),
        ],
        out_specs=pl.BlockSpec((1, n_rows, col), lambda l, j: (l, 0, j)),
        out_shape=jax.ShapeDtypeStruct((depth, n_rows, w_mod.shape[2]), F32),
        name="adaln_mod",
    )(cond_rows, w_mod, b_mod.reshape(depth, 1, -1))


def _inproj_kernel(x_ref, mod_ref, w_ref, cos_ref, sin_ref,
                   rq_ref, rk_ref, rv_ref, rg_ref, su_ref, sv_ref, aq_ref, ak_ref, av_ref,
                   *, tiles_per_batch, n_batch):
    i = pl.program_id(0)
    grp = jnp.minimum(i // tiles_per_batch, n_batch)
    shift = mod_ref[pl.ds(grp, 1), 0:D_MODEL]
    scale = mod_ref[pl.ds(grp, 1), D_MODEL:2 * D_MODEL]
    h = (_ln(x_ref[...]) * (1.0 + scale) + shift).astype(BF16)
    cos = cos_ref[...]
    sin = sin_ref[...]

    def proj(c0, w):
        return _dot(h, w_ref[:, c0:c0 + w])

    rot0 = D_IN
    rq_ref[...] = (proj(0, RET_W) * cos + proj(rot0, RET_W) * sin).astype(BF16)
    rk_ref[...] = ((proj(RET_W, RET_W) * cos + proj(rot0 + RET_W, RET_W) * sin) * RET_SCALE).astype(BF16)
    rv_ref[...] = proj(2 * RET_W, RET_W).astype(BF16)
    rg_ref[...] = proj(3 * RET_W, RET_W)
    su_ref[...] = proj(4 * RET_W, SGU_W)
    sv_ref[...] = proj(4 * RET_W + SGU_W, SGU_W)
    aq0 = 4 * RET_W + 2 * SGU_W
    aqr = rot0 + 2 * RET_W
    for half in range(ATT_QW // RET_W):
        o = half * RET_W
        aq_ref[:, o:o + RET_W] = ((proj(aq0 + o, RET_W) * cos + proj(aqr + o, RET_W) * sin) * ATT_SCALE).astype(BF16)
    ak0 = aq0 + ATT_QW
    akr = aqr + ATT_QW
    ak_ref[...] = (proj(ak0, ATT_KVW) * cos[:, 0:ATT_KVW] + proj(akr, ATT_KVW) * sin[:, 0:ATT_KVW]).astype(BF16)
    av_ref[...] = proj(ak0 + ATT_KVW, ATT_KVW).astype(BF16)


def _inproj(x_all, mod_l, w_ext, cos_tab, sin_tab, *, n_batch, seq):
    n_tok = x_all.shape[0]
    n_tiles = n_tok // TOK_TILE
    tiles_per_batch = seq // TOK_TILE
    n_lat_tiles = n_batch * tiles_per_batch

    def tab_map(i):
        return (jnp.where(i < n_lat_tiles, i % tiles_per_batch, tiles_per_batch), 0)

    tok = lambda w: pl.BlockSpec((TOK_TILE, w), lambda i: (i, 0))
    full = lambda a: pl.BlockSpec(a.shape, lambda i: (0,) * a.ndim)
    out_w = [(RET_W, BF16), (RET_W, BF16), (RET_W, BF16), (RET_W, F32), (SGU_W, F32), (SGU_W, F32),
             (ATT_QW, BF16), (ATT_KVW, BF16), (ATT_KVW, BF16)]
    return pl.pallas_call(
        functools.partial(_inproj_kernel, tiles_per_batch=tiles_per_batch, n_batch=n_batch),
        grid=(n_tiles,),
        in_specs=[tok(D_MODEL), full(mod_l), full(w_ext),
                  pl.BlockSpec((TOK_TILE, RET_W), tab_map), pl.BlockSpec((TOK_TILE, RET_W), tab_map)],
        out_specs=[tok(w) for w, _ in out_w],
        out_shape=[jax.ShapeDtypeStruct((n_tok, w), dt) for w, dt in out_w],
        compiler_params=pltpu.CompilerParams(dimension_semantics=("arbitrary",), vmem_limit_bytes=VMEM_LIMIT),
        name="inproj",
    )(x_all, mod_l, w_ext, cos_tab, sin_tab)


def _ret_kernel(dec_ref, qf_ref, kf_ref, vf_ref, qb_ref, kb_ref, vb_ref, of_ref, ob_ref,
                sf_ref, sb_ref, intra_ref, qd_ref, kd_ref, cd_ref):
    b = pl.program_id(0)
    s = pl.program_id(1)
    C = CHUNK
    W = RET_W

    @pl.when((b == 0) & (s == 0))
    def _tables():
        dec = dec_ref[...]
        lg = jnp.minimum(dec, 0.0) - jnp.log(1.0 + jnp.exp(-jnp.abs(dec)))
        lane_head = lax.broadcasted_iota(jnp.int32, (1, W), 1) // HEAD_DIM
        ii = lax.broadcasted_iota(jnp.int32, (C, C), 0)
        jj = lax.broadcasted_iota(jnp.int32, (C, C), 1)
        ri = lax.broadcasted_iota(jnp.int32, (C, W), 0).astype(F32)
        rb = lax.broadcasted_iota(jnp.int32, (W, W), 0) // HEAD_DIM
        cb = lax.broadcasted_iota(jnp.int32, (W, W), 1) // HEAD_DIM
        for d in range(2):
            lgl = jnp.zeros((1, W), F32)
            for hh in range(RET_HEADS):
                lg_h = lg[d:d + 1, hh:hh + 1]
                lgl = lgl + jnp.where(lane_head == hh, lg_h, 0.0)
                rel = (ii - jj) if d == 0 else (jj - ii)
                m = jnp.exp(jnp.maximum(rel, 0).astype(F32) * lg_h)
                intra_ref[d, hh * C:(hh + 1) * C, :] = jnp.where(rel >= 0, m, 0.0)
            if d == 0:
                qd_ref[d] = jnp.exp((ri + 1.0) * lgl)
                kd_ref[d] = jnp.exp((C - 1.0 - ri) * lgl)
            else:
                qd_ref[d] = jnp.exp((C - ri) * lgl)
                kd_ref[d] = jnp.exp(ri * lgl)
            cd_ref[d] = jnp.where(rb == cb, jnp.exp(C * lgl), 0.0)

    @pl.when(s == 0)
    def _zero():
        sf_ref[...] = jnp.zeros_like(sf_ref)
        sb_ref[...] = jnp.zeros_like(sb_ref)

    hm = _head_mask_stack(RET_HEADS, C, W)
    rb = lax.broadcasted_iota(jnp.int32, (W, W), 0) // HEAD_DIM
    cb = lax.broadcasted_iota(jnp.int32, (W, W), 1) // HEAD_DIM
    bd = rb == cb
    zero_b = jnp.zeros((), BF16)

    def direction(d, q_ref, k_ref, v_ref, st_ref, o_ref):
        q = q_ref[...]
        k = k_ref[...]
        v = v_ref[...]
        qs = jnp.where(hm, jnp.concatenate([q] * RET_HEADS, axis=0), zero_b)
        att = (_dot_nt(qs, k) * intra_ref[d]).astype(BF16)
        att = jnp.concatenate([att[hh * C:(hh + 1) * C] for hh in range(RET_HEADS)], axis=1)
        vs = jnp.where(hm, jnp.concatenate([v] * RET_HEADS, axis=0), zero_b)
        st = st_ref[...]
        o = _dot(att, vs) + _dot(q, st.astype(BF16)) * qd_ref[d]
        o_ref[...] = o
        kdec = (k.astype(F32) * kd_ref[d]).astype(BF16)
        st_ref[...] = st * cd_ref[d] + jnp.where(bd, _dot_tn(kdec, v), 0.0)

    direction(0, qf_ref, kf_ref, vf_ref, sf_ref, of_ref)
    direction(1, qb_ref, kb_ref, vb_ref, sb_ref, ob_ref)


def _retention(decays, rq, rk, rv, *, n_batch, seq, ctx_len):
    n_tok = rq.shape[0]
    nl = seq // CHUNK
    nc = ctx_len // CHUNK
    ctx0 = n_batch * nl

    def fwd_map(b, s):
        return (jnp.where(s < nc, ctx0 + b * nc + s, b * nl + (s - nc)), 0)

    def bwd_map(b, s):
        return (jnp.where(s < nc, ctx0 + b * nc + (nc - 1 - s), b * nl + (nl - 1 - (s - nc))), 0)

    fspec = pl.BlockSpec((CHUNK, RET_W), fwd_map)
    bspec = pl.BlockSpec((CHUNK, RET_W), bwd_map)
    return pl.pallas_call(
        _ret_kernel,
        grid=(n_batch, nc + nl),
        in_specs=[pl.BlockSpec(decays.shape, lambda b, s: (0, 0)), fspec, fspec, fspec, bspec, bspec, bspec],
        out_specs=[fspec, bspec],
        out_shape=[jax.ShapeDtypeStruct((n_tok, RET_W), F32)] * 2,
        scratch_shapes=[
            pltpu.VMEM((RET_W, RET_W), F32), pltpu.VMEM((RET_W, RET_W), F32),
            pltpu.VMEM((2, RET_HEADS * CHUNK, CHUNK), F32),
            pltpu.VMEM((2, CHUNK, RET_W), F32), pltpu.VMEM((2, CHUNK, RET_W), F32),
            pltpu.VMEM((2, RET_W, RET_W), F32),
        ],
        compiler_params=pltpu.CompilerParams(dimension_semantics=("arbitrary", "arbitrary")),
        name="retention",
    )(decays, rq, rk, rv, rq, rk, rv)


def _attn_kernel(sink_ref, q_ref, kp_ref, kc_ref, kn_ref, vp_ref, vc_ref, vn_ref, kx_ref, vx_ref, o_ref,
                 *, n_lat_blocks, seq):
    n = pl.program_id(1)
    W = CHUNK
    qi = lax.broadcasted_iota(jnp.int32, (W, 3 * W), 0)
    kj = lax.broadcasted_iota(jnp.int32, (W, 3 * W), 1)
    rel = kj - W - qi
    kpos = n * W - W + kj
    mask = (jnp.abs(rel) <= W) & (kpos >= 0) & (kpos < seq) & (n < n_lat_blocks)
    keys = jnp.concatenate([kp_ref[...], kc_ref[...], kn_ref[...], kx_ref[...]], axis=0)
    vals = jnp.concatenate([vp_ref[...], vc_ref[...], vn_ref[...], vx_ref[...]], axis=0)
    lane_head = lax.broadcasted_iota(jnp.int32, (1, LANES), 1) // HEAD_DIM
    zero_b = jnp.zeros((), BF16)
    slabs = [q_ref[:, r * LANES:(r + 1) * LANES] for r in range(GQA_GROUP)]
    mask4 = jnp.concatenate([mask] * GQA_GROUP, axis=0)
    outs = [jnp.zeros((W, LANES), F32) for _ in range(GQA_GROUP)]
    for g in range(ATT_KV_HEADS):
        mg = lane_head == g
        qs = jnp.concatenate([jnp.where(mg, s, zero_b) for s in slabs], axis=0)
        sink = jnp.concatenate([jnp.full((W, 1), sink_ref[g * GQA_GROUP + r], F32) for r in range(GQA_GROUP)], axis=0)
        sc = _dot_nt(qs, keys)
        s_loc = jnp.where(mask4, sc[:, 0:3 * W], NEG_INF)
        s_ctx = sc[:, 3 * W:]
        m = jnp.maximum(jnp.maximum(jnp.max(s_loc, axis=-1, keepdims=True),
                                    jnp.max(s_ctx, axis=-1, keepdims=True)), sink)
        p_loc = jnp.exp(s_loc - m)
        p_ctx = jnp.exp(s_ctx - m)
        den = (jnp.sum(p_loc, axis=-1, keepdims=True) + jnp.sum(p_ctx, axis=-1, keepdims=True)
               + jnp.exp(sink - m))
        p = jnp.concatenate([p_loc, p_ctx], axis=1).astype(BF16)
        o = _dot(p, vals) * (1.0 / den)
        for r in range(GQA_GROUP):
            outs[r] = outs[r] + jnp.where(mg, o[r * W:(r + 1) * W], 0.0)
    for r in range(GQA_GROUP):
        o_ref[:, r * LANES:(r + 1) * LANES] = outs[r].astype(BF16)


def _attention(sink, aq, ak, av, *, n_batch, seq, ctx_len):
    n_tok = aq.shape[0]
    nl = seq // CHUNK
    nc = ctx_len // CHUNK
    ctx0 = n_batch * nl

    def q_map(b, n):
        return (jnp.where(n < nl, b * nl + n, ctx0 + b * nc + (n - nl)), 0)

    def k_map(off):
        def f(b, n):
            return (b * nl + jnp.clip(n + off, 0, nl - 1), 0)
        return f

    x_map = lambda b, n: (n_batch * seq // ctx_len + b, 0)
    kv = lambda off: pl.BlockSpec((CHUNK, ATT_KVW), k_map(off))
    xspec = pl.BlockSpec((ctx_len, ATT_KVW), x_map)
    return pl.pallas_call(
        functools.partial(_attn_kernel, n_lat_blocks=nl, seq=seq),
        grid=(n_batch, nl + nc),
        in_specs=[pl.BlockSpec(memory_space=pltpu.SMEM), pl.BlockSpec((CHUNK, ATT_QW), q_map),
                  kv(-1), kv(0), kv(1), kv(-1), kv(0), kv(1), xspec, xspec],
        out_specs=pl.BlockSpec((CHUNK, ATT_QW), q_map),
        out_shape=jax.ShapeDtypeStruct((n_tok, ATT_QW), BF16),
        compiler_params=pltpu.CompilerParams(dimension_semantics=("arbitrary", "arbitrary")),
        name="window_attn",
    )(sink, aq, ak, ak, ak, av, av, av, ak, av)


def _mixout_kernel(x_ref, of_ref, ob_ref, rg_ref, su_ref, sv_ref, ao_ref, mod_ref, wout_ref, wcat_ref, bs_ref,
                   lng_ref, lnb_ref, wq_ref, x1_ref, h2t_ref, qp_ref, *, tiles_per_batch, n_batch):
    i = pl.program_id(0)
    grp = jnp.minimum(i // tiles_per_batch, n_batch)
    avg = _group_avg_matrix(RET_W)
    a = _group_ln(of_ref[...] + ob_ref[...], avg) * _silu(rg_ref[...])
    u = _gelu(su_ref[...])
    vn = _group_ln(_gelu(sv_ref[...]), avg)
    hm = _head_mask_stack(SGU_GROUPS, CHUNK, SGU_W)
    zero_b = jnp.zeros((), BF16)
    parts = []
    for c in range(TOK_TILE // CHUNK):
        vc = vn[c * CHUNK:(c + 1) * CHUNK].astype(BF16)
        vs = jnp.where(hm, jnp.concatenate([vc] * SGU_GROUPS, axis=0), zero_b)
        mix = _dot(wcat_ref[...], vs) + bs_ref[...]
        parts.append(u[c * CHUNK:(c + 1) * CHUNK] * mix)
    bmix = jnp.concatenate(parts, axis=0)
    y = (_dot(a.astype(BF16), wout_ref[0:RET_W, :])
         + _dot(bmix.astype(BF16), wout_ref[RET_W:RET_W + SGU_W, :])
         + _dot(ao_ref[...], wout_ref[RET_W + SGU_W:, :]))
    gate1 = mod_ref[pl.ds(grp, 1), 2 * D_MODEL:3 * D_MODEL]
    x1 = _ln(DEEPNORM_ALPHA * x_ref[...] + gate1 * y) * lng_ref[...] + lnb_ref[...]
    x1_ref[...] = x1
    shift2 = mod_ref[pl.ds(grp, 1), 3 * D_MODEL:4 * D_MODEL]
    scale2 = mod_ref[pl.ds(grp, 1), 4 * D_MODEL:5 * D_MODEL]
    h2f = _ln(x1) * (1.0 + scale2) + shift2
    h2t_ref[0] = h2f.T.astype(BF16)
    qp_ref[...] = _dot(h2f.astype(BF16), wq_ref[...]).astype(BF16)


def _mixout(x_all, o_f, o_b, rg, su, sv, ao, mod_l, w_out, w_cat, b_tab, ln_g, ln_b, wq, *, n_batch, seq):
    n_tok = x_all.shape[0]
    tiles_per_batch = seq // TOK_TILE
    tok = lambda w: pl.BlockSpec((TOK_TILE, w), lambda i: (i, 0))
    full = lambda a: pl.BlockSpec(a.shape, lambda i: (0,) * a.ndim)
    n_q = wq.shape[1]
    return pl.pallas_call(
        functools.partial(_mixout_kernel, tiles_per_batch=tiles_per_batch, n_batch=n_batch),
        grid=(n_tok // TOK_TILE,),
        in_specs=[tok(D_MODEL), tok(RET_W), tok(RET_W), tok(RET_W), tok(SGU_W), tok(SGU_W), tok(ATT_QW),
                  full(mod_l), full(w_out), full(w_cat), full(b_tab), full(ln_g), full(ln_b), full(wq)],
        out_specs=[tok(D_MODEL), pl.BlockSpec((1, D_MODEL, TOK_TILE), lambda i: (i, 0, 0)), tok(n_q)],
        out_shape=[jax.ShapeDtypeStruct((n_tok, D_MODEL), F32),
                   jax.ShapeDtypeStruct((n_tok // TOK_TILE, D_MODEL, TOK_TILE), BF16),
                   jax.ShapeDtypeStruct((n_tok, n_q), BF16)],
        compiler_params=pltpu.CompilerParams(dimension_semantics=("arbitrary",), vmem_limit_bytes=VMEM_LIMIT),
        name="mix_out",
    )(x_all, o_f, o_b, rg, su, sv, ao, mod_l, w_out, w_cat, b_tab, ln_g, ln_b, wq)


_CAND_SEGS = [(a, PEER_TOPK // (a + 1)) for a in range(1, 8)]


def _batcher_pairs(n):
    pairs = []
    p = 1
    while p < n:
        k = p
        while k >= 1:
            for j in range(k % p, n - k, 2 * k):
                for i in range(min(k, n - j - k)):
                    if (i + j) // (2 * p) == (i + j + k) // (2 * p):
                        pairs.append((i + j, i + j + k))
            k //= 2
        p *= 2
    return pairs


def _top16_sorted(slabs):
    n = len(slabs)
    x = list(slabs)
    for i, j in _batcher_pairs(n):
        x[i], x[j] = jnp.maximum(x[i], x[j]), jnp.minimum(x[i], x[j])
    shift = 4
    while shift >= 1:
        y = [pltpu.roll(v, shift, 0) for v in x]
        x = [jnp.maximum(x[k], y[n - 1 - k]) for k in range(n)]
        d = n // 2
        while d >= 1:
            for i in range(n):
                if (i & d) == 0:
                    x[i], x[i + d] = jnp.maximum(x[i], x[i + d]), jnp.minimum(x[i], x[i + d])
            d //= 2
        shift //= 2
    return x


def _dup_bf16_words(x):
    hi = pltpu.bitcast(x.astype(BF16).astype(F32), jnp.uint32)
    return hi | (hi >> 16)


def _peer_kernel(h2t_ref, qp_ref, x1_ref, mod_ref, k1_ref, k2_ref, u_ref, vt_ref, lng_ref, lnb_ref, out_ref,
                 s_sc, n_sc, a_sc, rk_sc, b_sc, row_sc, act0_sc, act1_sc, hs0_sc, hs1_sc, acc_sc,
                 *, tiles_per_batch, n_batch, n_blocks):
    t = pl.program_id(0)
    e = pl.program_id(1)
    NG = PEER_TOK // LANES
    K = PEER_TOPK
    half = PEER_QDIM // 2

    @pl.when(e == 0)
    def _prologue():
        acc_sc[...] = jnp.zeros_like(acc_sc)
        act1_sc[...] = jnp.zeros_like(act1_sc)
        hs1_sc[...] = jnp.zeros_like(hs1_sc)
        for h in range(PEER_HEADS):
            for p, kref in ((0, k1_ref), (1, k2_ref)):
                c0 = h * PEER_QDIM + p * half
                sT = _dot_nt(kref[...], qp_ref[:, c0:c0 + half])
                for g in range(NG):
                    s_sc[2 * h + p, g] = sT[:, g * LANES:(g + 1) * LANES]

        row8 = lax.broadcasted_iota(jnp.int32, (8, LANES), 0)

        def stats(it, carry):
            h = it // NG
            g = it % NG
            s1 = s_sc[2 * h, g]
            s2 = s_sc[2 * h + 1, g]
            slabs1 = [s1[8 * k:8 * k + 8] for k in range(K)]
            slabs2 = [s2[8 * k:8 * k + 8] for k in range(K)]
            t1 = _top16_sorted(slabs1)
            t2 = _top16_sorted(slabs2)
            v1 = jnp.concatenate([x[0:1] for x in t1], axis=0)
            v2 = jnp.concatenate([x[0:1] for x in t2], axis=0)
            segs = [v1[0:1] + v2]
            for a, n_a in _CAND_SEGS:
                segs.append(jnp.where(row8 < n_a, v1[a:a + 1] + v2[0:8], NEG_INF))
            segs.append(v1[8:16] + v2[0:1])
            cand = jnp.concatenate(segs, axis=0)
            cur = cand
            tau = None
            for r in range(K):
                tau = jnp.max(cur, axis=0, keepdims=True)
                cur = jnp.where(cur >= tau, NEG_INF, cur)
            cmax = v1[0:1] + v2[0:1]
            z = jnp.sum(jnp.where(cand >= tau, jnp.exp(cand - cmax), 0.0), axis=0, keepdims=True)
            tau8 = jnp.broadcast_to(tau, (8, LANES))
            rz8 = jnp.broadcast_to(1.0 / z, (8, LANES))
            n_parts, a_parts, rk_parts, b_parts = [], [], [], []
            for k in range(K):
                n = jnp.zeros((8, LANES), F32)
                rk = jnp.zeros((8, LANES), F32)
                for bb in range(K):
                    n = jnp.where((slabs1[k] + t2[bb]) >= tau8, float(bb + 1), n)
                    rk = jnp.where(t2[bb] > slabs2[k], float(bb + 1), rk)
                n_parts.append(n)
                rk_parts.append(rk)
                a_parts.append(jnp.exp(slabs1[k] - t1[0]) * rz8)
                b_parts.append(jnp.exp(slabs2[k] - t2[0]))
            n_sc[h, g] = _dup_bf16_words(jnp.concatenate(n_parts, axis=0))
            a_sc[h, g] = _dup_bf16_words(jnp.concatenate(a_parts, axis=0))
            rk_sc[h, g] = jnp.concatenate(rk_parts, axis=0).astype(BF16)
            b_sc[h, g] = jnp.concatenate(b_parts, axis=0).astype(BF16)
            return carry

        lax.fori_loop(0, PEER_HEADS * NG, stats, 0)

    rows_per_blk = PEER_EB // PEER_NKEYS
    kc_rows = 16
    zero_b = jnp.zeros((), BF16)
    blk = jnp.where(e == 0, 1, jnp.where(e == n_blocks + 1, n_blocks - 2, e - 1))
    tile0 = pl.multiple_of((blk // 2) * 8, 8)

    def step(act_w, act_r, hs_w, hs_r, row_off):
        def half(c, carry):
            a_new = _gelu(_dot(u_ref[0], h2t_ref[c])).astype(BF16)
            act_w[2 * c] = a_new[:, 0:LANES]
            act_w[2 * c + 1] = a_new[:, LANES:2 * LANES]
            hs_prev = jnp.concatenate([hs_r[2 * c], hs_r[2 * c + 1]], axis=1)
            acc_sc[c] += _dot(vt_ref[0, 0], hs_prev)
            gate_group(2 * c)
            gate_group(2 * c + 1)
            return carry

        def gate_group(g):
            for h in range(PEER_HEADS):
                n_tile = n_sc[h, g, pl.ds(tile0, 8), :]
                a_tile = a_sc[h, g, pl.ds(tile0, 8), :]
                for r in range(rows_per_blk):
                    k = row_off + r
                    row_sc[0, h, r] = pltpu.bitcast(jnp.broadcast_to(n_tile[k:k + 1, :], (8, LANES)), BF16)
                    row_sc[1, h, r] = pltpu.bitcast(jnp.broadcast_to(a_tile[k:k + 1, :], (8, LANES)), BF16)
            for kc in range(PEER_NKEYS // kc_rows):
                keys = slice(kc * kc_rows, (kc + 1) * kc_rows)
                gates = [jnp.zeros((kc_rows, LANES), BF16) for _ in range(rows_per_blk)]
                for h in range(PEER_HEADS):
                    rk = rk_sc[h, g, keys, :]
                    bb = b_sc[h, g, keys, :]
                    for r in range(rows_per_blk):
                        n_full = jnp.concatenate([row_sc[0, h, r]] * (kc_rows // 16), axis=0)
                        a_full = jnp.concatenate([row_sc[1, h, r]] * (kc_rows // 16), axis=0)
                        gates[r] = gates[r] + jnp.where(rk < n_full, a_full * bb, zero_b)
                for r in range(rows_per_blk):
                    rows = slice(r * PEER_NKEYS + kc * kc_rows, r * PEER_NKEYS + (kc + 1) * kc_rows)
                    hs_w[g, rows, :] = gates[r] * act_r[g, rows, :]

        lax.fori_loop(0, NG // 2, half, 0)

    @pl.when(e % 2 == 0)
    def _even():
        step(act0_sc, act1_sc, hs0_sc, hs1_sc, rows_per_blk)

    @pl.when(e % 2 == 1)
    def _odd():
        step(act1_sc, act0_sc, hs1_sc, hs0_sc, 0)

    @pl.when(e == n_blocks + 1)
    def _epilogue():
        grp = jnp.minimum(t // tiles_per_batch, n_batch)
        gate2 = mod_ref[pl.ds(grp, 1), 5 * D_MODEL:6 * D_MODEL]
        f = jnp.concatenate([acc_sc[c].T for c in range(NG // 2)], axis=0)
        out_ref[...] = _ln(DEEPNORM_ALPHA * x1_ref[...] + gate2 * f) * lng_ref[...] + lnb_ref[...]


def _peer(h2, qp, x1, mod_l, k1, k2, u_bf, vt3, ln_g, ln_b, *, layer, n_batch, seq, n_tiles):
    tiles_per_batch = seq // PEER_TOK
    n_e = PEER_N // PEER_EB
    NG = PEER_TOK // LANES
    tok = lambda w: pl.BlockSpec((PEER_TOK, w), lambda t, e: (t, 0))
    full = lambda a: pl.BlockSpec(a.shape, lambda t, e: (0,) * a.ndim)
    stat = lambda n, dt: pltpu.VMEM((n, NG, PEER_NKEYS, LANES), dt)
    return pl.pallas_call(
        functools.partial(_peer_kernel, tiles_per_batch=tiles_per_batch, n_batch=n_batch, n_blocks=n_e),
        grid=(n_tiles, n_e + 2),
        in_specs=[pl.BlockSpec((PEER_TOK // TOK_TILE, D_MODEL, TOK_TILE), lambda t, e: (t, 0, 0)),
                  tok(qp.shape[1]), tok(D_MODEL), full(mod_l), full(k1), full(k2),
                  pl.BlockSpec((1, PEER_EB, D_MODEL), lambda t, e: (layer, jnp.minimum(e, n_e - 1), 0)),
                  pl.BlockSpec((1, 1, D_MODEL, PEER_EB), lambda t, e: (layer, jnp.clip(e - 2, 0, n_e - 1), 0, 0)),
                  full(ln_g), full(ln_b)],
        out_specs=tok(D_MODEL),
        out_shape=jax.ShapeDtypeStruct((n_tiles * PEER_TOK, D_MODEL), F32),
        scratch_shapes=[stat(2 * PEER_HEADS, F32),
                        stat(PEER_HEADS, jnp.uint32), stat(PEER_HEADS, jnp.uint32),
                        stat(PEER_HEADS, BF16), stat(PEER_HEADS, BF16),
                        pltpu.VMEM((2, PEER_HEADS, PEER_EB // PEER_NKEYS, 16, LANES), BF16),
                        pltpu.VMEM((NG, PEER_EB, LANES), BF16), pltpu.VMEM((NG, PEER_EB, LANES), BF16),
                        pltpu.VMEM((NG, PEER_EB, LANES), BF16), pltpu.VMEM((NG, PEER_EB, LANES), BF16),
                        pltpu.VMEM((NG // 2, D_MODEL, 2 * LANES), F32)],
        compiler_params=pltpu.CompilerParams(dimension_semantics=("arbitrary", "arbitrary"),
                                             vmem_limit_bytes=VMEM_LIMIT),
        name="peer",
    )(h2, qp, x1, mod_l, k1, k2, u_bf, vt3, ln_g, ln_b)


def _rope_tables(seq, ctx_len):
    rows = seq // GRID_W
    row_id = jnp.repeat(jnp.arange(rows), GRID_W).astype(F32)
    col_id = jnp.tile(jnp.arange(GRID_W), rows).astype(F32)
    inv = jnp.power(ROPE_BASE, -jnp.arange(ROPE_PAIRS, dtype=F32) / ROPE_PAIRS)
    ang_r = row_id[:, None] * inv
    ang_c = col_id[:, None] * inv
    cos64 = jnp.concatenate([jnp.cos(ang_r)] * 2 + [jnp.cos(ang_c)] * 2, axis=-1)
    sin64 = jnp.concatenate([jnp.sin(ang_r)] * 2 + [jnp.sin(ang_c)] * 2, axis=-1)
    n_rep = RET_W // HEAD_DIM
    cos_tab = jnp.concatenate([jnp.tile(cos64, (1, n_rep)), jnp.ones((ctx_len, RET_W), F32)], axis=0)
    sin_tab = jnp.concatenate([jnp.tile(sin64, (1, n_rep)), jnp.zeros((ctx_len, RET_W), F32)], axis=0)
    return cos_tab, sin_tab


def _rot_partner(width):
    l = np.arange(width)
    lo = (l % 32) < 16
    partner = np.where(lo, l + 16, l - 16)
    sign = np.where(lo, -1.0, 1.0).astype(np.float32)
    return partner, sign


def _slab_perm():
    new = np.arange(ATT_QW)
    r, rem = new // LANES, new % LANES
    g, d = rem // HEAD_DIM, rem % HEAD_DIM
    return (g * GQA_GROUP + r) * HEAD_DIM + d


def _prep_w_in(w_in):
    o_rq, o_rk, o_aq, o_ak = 0, RET_W, 4 * RET_W + 2 * SGU_W, 4 * RET_W + 2 * SGU_W + ATT_QW
    slab = _slab_perm()
    aq = w_in[..., o_aq:o_aq + ATT_QW]

    def partner(block):
        p, sg = _rot_partner(block.shape[-1])
        return block[..., p] * sg

    base = jnp.concatenate([w_in[..., :o_aq], aq[..., slab], w_in[..., o_ak:]], axis=-1)
    rot = jnp.concatenate([partner(w_in[..., o_rq:o_rq + RET_W]), partner(w_in[..., o_rk:o_rk + RET_W]),
                           partner(aq)[..., slab], partner(w_in[..., o_ak:o_ak + ATT_KVW])], axis=-1)
    return jnp.concatenate([base, rot], axis=-1).astype(BF16)


def kernel(x, c, ctx, c_ctx, w_mod, b_mod, w_in, w_out, ret_decay_fwd, ret_decay_bwd, sgu_w, sgu_b, attn_sink,
           ln_mix_g, ln_mix_b, peer_wq, peer_k1, peer_k2, peer_u, peer_v, ln_ffn_g, ln_ffn_b):
    n_batch, seq, _ = x.shape
    ctx_len = ctx.shape[1]
    depth = w_in.shape[0]
    assert seq % PEER_TOK == 0 and ctx_len == TOK_TILE and (n_batch * ctx_len) % PEER_TOK == 0
    assert n_batch + 1 <= 8
    n_lat = n_batch * seq
    kw = dict(n_batch=n_batch, seq=seq)

    x_all = jnp.concatenate([x.reshape(n_lat, D_MODEL), ctx.reshape(n_batch * ctx_len, D_MODEL)], axis=0)
    cond = jnp.concatenate([c, c_ctx[None, :], jnp.zeros((8 - n_batch - 1, D_MODEL), F32)], axis=0)
    mods = _modulations(cond, w_mod, b_mod)

    cos_tab, sin_tab = _rope_tables(seq, ctx_len)
    w_ext = _prep_w_in(w_in)
    slab = _slab_perm()
    w_out_p = jnp.concatenate([w_out[:, :RET_W + SGU_W], w_out[:, RET_W + SGU_W:][:, slab]], axis=1).astype(BF16)
    w_cat = jnp.transpose(sgu_w, (0, 2, 1, 3)).reshape(depth, CHUNK, SGU_GROUPS * CHUNK).astype(BF16)
    b_tab = jnp.repeat(jnp.transpose(sgu_b, (0, 2, 1)), HEAD_DIM, axis=2)
    decays = jnp.stack([ret_decay_fwd, ret_decay_bwd], axis=1)
    wq = peer_wq.astype(BF16)
    k1 = peer_k1.astype(BF16)
    k2 = peer_k2.astype(BF16)
    u_bf = peer_u.astype(BF16)
    vt3 = jnp.transpose(peer_v.reshape(depth, PEER_N // PEER_EB, PEER_EB, D_MODEL), (0, 1, 3, 2)).astype(BF16)
    row = lambda a: a.reshape(depth, 1, D_MODEL)
    lmg, lmb, lfg, lfb = row(ln_mix_g), row(ln_mix_b), row(ln_ffn_g), row(ln_ffn_b)

    for l in range(depth):
        last = l == depth - 1
        rq, rk, rv, rg, su, sv, aq, ak, av = _inproj(x_all, mods[l], w_ext[l], cos_tab, sin_tab, **kw)
        o_f, o_b = _retention(decays[l], rq, rk, rv, ctx_len=ctx_len, **kw)
        ao = _attention(attn_sink[l], aq, ak, av, ctx_len=ctx_len, **kw)
        x1, h2, qp = _mixout(x_all, o_f, o_b, rg, su, sv, ao, mods[l], w_out_p[l], w_cat[l], b_tab[l],
                             lmg[l], lmb[l], wq[l], **kw)
        n_tiles = (n_lat if last else x_all.shape[0]) // PEER_TOK
        x_all = _peer(h2, qp, x1, mods[l], k1[l], k2[l], u_bf, vt3, lfg[l], lfb[l], layer=l, n_tiles=n_tiles, **kw)
    return x_all[:n_lat].reshape(n_batch, seq, D_MODEL)
```

```python
import functools

import numpy as np
import jax
import jax.numpy as jnp
from jax import lax
from jax.experimental import pallas as pl
from jax.experimental.pallas import tpu as pltpu

F32 = jnp.float32
BF16 = jnp.bfloat16

D_MODEL = 1024
DEPTH = 4
GRID_W = 64
HEAD_DIM = 64
ROPE_PAIRS = HEAD_DIM // 4
ROPE_BASE = 10000.0
RET_HEADS = 4
CHUNK = 128
RET_W = RET_HEADS * HEAD_DIM
RET_SCALE = HEAD_DIM ** -0.5
SGU_GROUPS = 4
SGU_W = SGU_GROUPS * HEAD_DIM
ATT_Q_HEADS = 8
ATT_KV_HEADS = 2
GQA_GROUP = ATT_Q_HEADS // ATT_KV_HEADS
ATT_QW = ATT_Q_HEADS * HEAD_DIM
ATT_KVW = ATT_KV_HEADS * HEAD_DIM
ATT_SCALE = HEAD_DIM ** -0.5
D_IN = 4 * RET_W + 2 * SGU_W + ATT_QW + 2 * ATT_KVW
D_ROT = 2 * RET_W + ATT_QW + ATT_KVW
PEER_HEADS = 8
PEER_NKEYS = 128
PEER_N = PEER_NKEYS * PEER_NKEYS
PEER_QDIM = 256
PEER_TOPK = 16
LN_EPS = 1e-5
DEEPNORM_ALPHA = (2 * DEPTH) ** 0.25

LANES = 128
TOK_TILE = 512
H2T_TOK = 2 * LANES
PEER_TOK = 512
PEER_EB = 512
VMEM_LIMIT = 56 * 1024 * 1024

NEG_INF = float("-inf")


def _ln(x):
    mu = jnp.mean(x, axis=-1, keepdims=True)
    xc = x - mu
    var = jnp.mean(xc * xc, axis=-1, keepdims=True)
    return xc * lax.rsqrt(var + LN_EPS)


def _gelu(x):
    h = 0.5 * x
    return h + h * jnp.tanh(x * (0.7978845608028654 + (0.7978845608028654 * 0.044715) * (x * x)))


def _silu(x):
    return x * (1.0 / (1.0 + jnp.exp(-x)))


def _dot(a, b):
    return jnp.dot(a, b, preferred_element_type=F32)


def _dot_nt(a, b):
    return lax.dot_general(a, b, (((1,), (1,)), ((), ())), preferred_element_type=F32)


def _dot_tn(a, b):
    return lax.dot_general(a, b, (((0,), (0,)), ((), ())), preferred_element_type=F32)


def _group_mean(z, avg):
    hi = z.astype(BF16)
    lo = (z - hi.astype(F32)).astype(BF16)
    return _dot(hi, avg) + _dot(lo, avg)


def _group_ln(x, avg):
    mu = _group_mean(x, avg)
    xc = x - mu
    var = _group_mean(xc * xc, avg)
    return xc * lax.rsqrt(var + LN_EPS)


def _group_avg_matrix(width):
    r = lax.broadcasted_iota(jnp.int32, (width, width), 0) // HEAD_DIM
    c = lax.broadcasted_iota(jnp.int32, (width, width), 1) // HEAD_DIM
    return jnp.where(r == c, 1.0 / HEAD_DIM, 0.0).astype(BF16)


def _head_mask_stack(n_heads, rows, width):
    r = lax.broadcasted_iota(jnp.int32, (n_heads * rows, width), 0) // rows
    c = lax.broadcasted_iota(jnp.int32, (n_heads * rows, width), 1) // HEAD_DIM
    return r == c


def _mod_kernel(c_ref, w_ref, b_ref, o_ref):
    s = _silu(c_ref[...])
    hi = s.astype(BF16)
    lo = (s - hi.astype(F32)).astype(BF16)
    w = w_ref[0]
    whi = w.astype(BF16)
    wlo = (w - whi.astype(F32)).astype(BF16)
    o_ref[0] = _dot(hi, whi) + _dot(lo, whi) + _dot(hi, wlo) + b_ref[0]


def _modulations(cond_rows, w_mod, b_mod):
    depth = w_mod.shape[0]
    n_rows = cond_rows.shape[0]
    col = 1024
    n_col = w_mod.shape[2] // col
    return pl.pallas_call(
        _mod_kernel,
        grid=(depth, n_col),
        in_specs=[
            pl.BlockSpec((n_rows, D_MODEL), lambda l, j: (0, 0)),
            pl.BlockSpec((1, D_MODEL, col), lambda l, j: (l, 0, j)),
            pl.BlockSpec((1, 1, col), lambda l, j: (l, 0, j)),
        ],
        out_specs=pl.BlockSpec((1, n_rows, col), lambda l, j: (l, 0, j)),
        out_shape=jax.ShapeDtypeStruct((depth, n_rows, w_mod.shape[2]), F32),
        name="adaln_mod",
    )(cond_rows, w_mod, b_mod.reshape(depth, 1, -1))


def _inproj_kernel(x_ref, mod_ref, w_ref, cos_ref, sin_ref,
                   rq_ref, rk_ref, rv_ref, rg_ref, su_ref, sv_ref, aq_ref, ak_ref, av_ref,
                   *, tiles_per_batch, n_batch):
    i = pl.program_id(0)
    grp = jnp.minimum(i // tiles_per_batch, n_batch)
    shift = mod_ref[pl.ds(grp, 1), 0:D_MODEL]
    scale = mod_ref[pl.ds(grp, 1), D_MODEL:2 * D_MODEL]
    h = (_ln(x_ref[...]) * (1.0 + scale) + shift).astype(BF16)
    cos = cos_ref[...]
    sin = sin_ref[...]

    def proj(c0, w):
        return _dot(h, w_ref[:, c0:c0 + w])

    rot0 = D_IN
    rq_ref[...] = (proj(0, RET_W) * cos + proj(rot0, RET_W) * sin).astype(BF16)
    rk_ref[...] = ((proj(RET_W, RET_W) * cos + proj(rot0 + RET_W, RET_W) * sin) * RET_SCALE).astype(BF16)
    rv_ref[...] = proj(2 * RET_W, RET_W).astype(BF16)
    rg_ref[...] = proj(3 * RET_W, RET_W)
    su_ref[...] = proj(4 * RET_W, SGU_W)
    sv_ref[...] = proj(4 * RET_W + SGU_W, SGU_W)
    aq0 = 4 * RET_W + 2 * SGU_W
    aqr = rot0 + 2 * RET_W
    for half in range(ATT_QW // RET_W):
        o = half * RET_W
        aq_ref[:, o:o + RET_W] = ((proj(aq0 + o, RET_W) * cos + proj(aqr + o, RET_W) * sin) * ATT_SCALE).astype(BF16)
    ak0 = aq0 + ATT_QW
    akr = aqr + ATT_QW
    ak_ref[...] = (proj(ak0, ATT_KVW) * cos[:, 0:ATT_KVW] + proj(akr, ATT_KVW) * sin[:, 0:ATT_KVW]).astype(BF16)
    av_ref[...] = proj(ak0 + ATT_KVW, ATT_KVW).astype(BF16)


def _inproj(x_all, mod_l, w_ext, cos_tab, sin_tab, *, n_batch, seq):
    n_tok = x_all.shape[0]
    n_tiles = n_tok // TOK_TILE
    tiles_per_batch = seq // TOK_TILE
    n_lat_tiles = n_batch * tiles_per_batch

    def tab_map(i):
        return (jnp.where(i < n_lat_tiles, i % tiles_per_batch, tiles_per_batch + (i - n_lat_tiles)), 0)

    tok = lambda w: pl.BlockSpec((TOK_TILE, w), lambda i: (i, 0))
    full = lambda a: pl.BlockSpec(a.shape, lambda i: (0,) * a.ndim)
    out_w = [(RET_W, BF16), (RET_W, BF16), (RET_W, BF16), (RET_W, F32), (SGU_W, F32), (SGU_W, F32),
             (ATT_QW, BF16), (ATT_KVW, BF16), (ATT_KVW, BF16)]
    return pl.pallas_call(
        functools.partial(_inproj_kernel, tiles_per_batch=tiles_per_batch, n_batch=n_batch),
        grid=(n_tiles,),
        in_specs=[tok(D_MODEL), full(mod_l), full(w_ext),
                  pl.BlockSpec((TOK_TILE, RET_W), tab_map), pl.BlockSpec((TOK_TILE, RET_W), tab_map)],
        out_specs=[tok(w) for w, _ in out_w],
        out_shape=[jax.ShapeDtypeStruct((n_tok, w), dt) for w, dt in out_w],
        compiler_params=pltpu.CompilerParams(dimension_semantics=("arbitrary",), vmem_limit_bytes=VMEM_LIMIT),
        name="inproj",
    )(x_all, mod_l, w_ext, cos_tab, sin_tab)


def _ret_kernel(dec_ref, qf_ref, kf_ref, vf_ref, qb_ref, kb_ref, vb_ref, of_ref, ob_ref,
                sf_ref, sb_ref, intra_ref, qd_ref, kd_ref, cd_ref):
    b = pl.program_id(0)
    s = pl.program_id(1)
    C = CHUNK
    W = RET_W

    @pl.when((b == 0) & (s == 0))
    def _tables():
        dec = dec_ref[...]
        lg = jnp.minimum(dec, 0.0) - jnp.log(1.0 + jnp.exp(-jnp.abs(dec)))
        lane_head = lax.broadcasted_iota(jnp.int32, (1, W), 1) // HEAD_DIM
        ii = lax.broadcasted_iota(jnp.int32, (C, C), 0)
        jj = lax.broadcasted_iota(jnp.int32, (C, C), 1)
        ri = lax.broadcasted_iota(jnp.int32, (C, W), 0).astype(F32)
        rb = lax.broadcasted_iota(jnp.int32, (W, W), 0) // HEAD_DIM
        cb = lax.broadcasted_iota(jnp.int32, (W, W), 1) // HEAD_DIM
        for d in range(2):
            lgl = jnp.zeros((1, W), F32)
            for hh in range(RET_HEADS):
                lg_h = lg[d:d + 1, hh:hh + 1]
                lgl = lgl + jnp.where(lane_head == hh, lg_h, 0.0)
                rel = (ii - jj) if d == 0 else (jj - ii)
                m = jnp.exp(jnp.maximum(rel, 0).astype(F32) * lg_h)
                intra_ref[d, hh * C:(hh + 1) * C, :] = jnp.where(rel >= 0, m, 0.0)
            if d == 0:
                qd_ref[d] = jnp.exp((ri + 1.0) * lgl)
                kd_ref[d] = jnp.exp((C - 1.0 - ri) * lgl)
            else:
                qd_ref[d] = jnp.exp((C - ri) * lgl)
                kd_ref[d] = jnp.exp(ri * lgl)
            cd_ref[d] = jnp.where(rb == cb, jnp.exp(C * lgl), 0.0)

    @pl.when(s == 0)
    def _zero():
        sf_ref[...] = jnp.zeros_like(sf_ref)
        sb_ref[...] = jnp.zeros_like(sb_ref)

    hm = _head_mask_stack(RET_HEADS, C, W)
    rb = lax.broadcasted_iota(jnp.int32, (W, W), 0) // HEAD_DIM
    cb = lax.broadcasted_iota(jnp.int32, (W, W), 1) // HEAD_DIM
    bd = rb == cb
    zero_b = jnp.zeros((), BF16)

    def direction(d, q_ref, k_ref, v_ref, st_ref, o_ref):
        q = q_ref[...]
        k = k_ref[...]
        v = v_ref[...]
        qs = jnp.where(hm, jnp.concatenate([q] * RET_HEADS, axis=0), zero_b)
        att = (_dot_nt(qs, k) * intra_ref[d]).astype(BF16)
        att = jnp.concatenate([att[hh * C:(hh + 1) * C] for hh in range(RET_HEADS)], axis=1)
        vs = jnp.where(hm, jnp.concatenate([v] * RET_HEADS, axis=0), zero_b)
        st = st_ref[...]
        o = _dot(att, vs) + _dot(q, st.astype(BF16)) * qd_ref[d]
        o_ref[...] = o
        kdec = (k.astype(F32) * kd_ref[d]).astype(BF16)
        st_ref[...] = st * cd_ref[d] + jnp.where(bd, _dot_tn(kdec, v), 0.0)

    direction(0, qf_ref, kf_ref, vf_ref, sf_ref, of_ref)
    direction(1, qb_ref, kb_ref, vb_ref, sb_ref, ob_ref)


def _retention(decays, rq, rk, rv, *, n_batch, seq, ctx_len):
    n_tok = rq.shape[0]
    nl = seq // CHUNK
    nc = ctx_len // CHUNK
    ctx0 = n_batch * nl

    def fwd_map(b, s):
        return (jnp.where(s < nc, ctx0 + b * nc + s, b * nl + (s - nc)), 0)

    def bwd_map(b, s):
        return (jnp.where(s < nc, ctx0 + b * nc + (nc - 1 - s), b * nl + (nl - 1 - (s - nc))), 0)

    fspec = pl.BlockSpec((CHUNK, RET_W), fwd_map)
    bspec = pl.BlockSpec((CHUNK, RET_W), bwd_map)
    return pl.pallas_call(
        _ret_kernel,
        grid=(n_batch, nc + nl),
        in_specs=[pl.BlockSpec(decays.shape, lambda b, s: (0, 0)), fspec, fspec, fspec, bspec, bspec, bspec],
        out_specs=[fspec, bspec],
        out_shape=[jax.ShapeDtypeStruct((n_tok, RET_W), F32)] * 2,
        scratch_shapes=[
            pltpu.VMEM((RET_W, RET_W), F32), pltpu.VMEM((RET_W, RET_W), F32),
            pltpu.VMEM((2, RET_HEADS * CHUNK, CHUNK), F32),
            pltpu.VMEM((2, CHUNK, RET_W), F32), pltpu.VMEM((2, CHUNK, RET_W), F32),
            pltpu.VMEM((2, RET_W, RET_W), F32),
        ],
        compiler_params=pltpu.CompilerParams(dimension_semantics=("arbitrary", "arbitrary")),
        name="retention",
    )(decays, rq, rk, rv, rq, rk, rv)


def _attn_kernel(sink_ref, q_ref, kp_ref, kc_ref, kn_ref, vp_ref, vc_ref, vn_ref, kx_ref, vx_ref, o_ref,
                 *, n_lat_blocks, seq):
    n = pl.program_id(1)
    W = CHUNK
    qi = lax.broadcasted_iota(jnp.int32, (W, 3 * W), 0)
    kj = lax.broadcasted_iota(jnp.int32, (W, 3 * W), 1)
    rel = kj - W - qi
    kpos = n * W - W + kj
    mask = (jnp.abs(rel) <= W) & (kpos >= 0) & (kpos < seq) & (n < n_lat_blocks)
    keys = jnp.concatenate([kp_ref[...], kc_ref[...], kn_ref[...], kx_ref[...]], axis=0)
    vals = jnp.concatenate([vp_ref[...], vc_ref[...], vn_ref[...], vx_ref[...]], axis=0)
    lane_head = lax.broadcasted_iota(jnp.int32, (1, LANES), 1) // HEAD_DIM
    zero_b = jnp.zeros((), BF16)
    slabs = [q_ref[:, r * LANES:(r + 1) * LANES] for r in range(GQA_GROUP)]
    mask4 = jnp.concatenate([mask] * GQA_GROUP, axis=0)
    outs = [jnp.zeros((W, LANES), F32) for _ in range(GQA_GROUP)]
    for g in range(ATT_KV_HEADS):
        mg = lane_head == g
        qs = jnp.concatenate([jnp.where(mg, s, zero_b) for s in slabs], axis=0)
        sink = jnp.concatenate([jnp.full((W, 1), sink_ref[g * GQA_GROUP + r], F32) for r in range(GQA_GROUP)], axis=0)
        sc = _dot_nt(qs, keys)
        s_loc = jnp.where(mask4, sc[:, 0:3 * W], NEG_INF)
        s_ctx = sc[:, 3 * W:]
        m = jnp.maximum(jnp.maximum(jnp.max(s_loc, axis=-1, keepdims=True),
                                    jnp.max(s_ctx, axis=-1, keepdims=True)), sink)
        p_loc = jnp.exp(s_loc - m)
        p_ctx = jnp.exp(s_ctx - m)
        den = (jnp.sum(p_loc, axis=-1, keepdims=True) + jnp.sum(p_ctx, axis=-1, keepdims=True)
               + jnp.exp(sink - m))
        p = jnp.concatenate([p_loc, p_ctx], axis=1).astype(BF16)
        o = _dot(p, vals) * (1.0 / den)
        for r in range(GQA_GROUP):
            outs[r] = outs[r] + jnp.where(mg, o[r * W:(r + 1) * W], 0.0)
    for r in range(GQA_GROUP):
        o_ref[:, r * LANES:(r + 1) * LANES] = outs[r].astype(BF16)


def _attention(sink, aq, ak, av, *, n_batch, seq, ctx_len):
    n_tok = aq.shape[0]
    nl = seq // CHUNK
    nc = ctx_len // CHUNK
    ctx0 = n_batch * nl

    def q_map(b, n):
        return (jnp.where(n < nl, b * nl + n, ctx0 + b * nc + (n - nl)), 0)

    def k_map(off):
        def f(b, n):
            return (b * nl + jnp.clip(n + off, 0, nl - 1), 0)
        return f

    x_map = lambda b, n: (n_batch * seq // ctx_len + b, 0)
    kv = lambda off: pl.BlockSpec((CHUNK, ATT_KVW), k_map(off))
    xspec = pl.BlockSpec((ctx_len, ATT_KVW), x_map)
    return pl.pallas_call(
        functools.partial(_attn_kernel, n_lat_blocks=nl, seq=seq),
        grid=(n_batch, nl + nc),
        in_specs=[pl.BlockSpec(memory_space=pltpu.SMEM), pl.BlockSpec((CHUNK, ATT_QW), q_map),
                  kv(-1), kv(0), kv(1), kv(-1), kv(0), kv(1), xspec, xspec],
        out_specs=pl.BlockSpec((CHUNK, ATT_QW), q_map),
        out_shape=jax.ShapeDtypeStruct((n_tok, ATT_QW), BF16),
        compiler_params=pltpu.CompilerParams(dimension_semantics=("arbitrary", "arbitrary")),
        name="window_attn",
    )(sink, aq, ak, ak, ak, av, av, av, ak, av)


def _mixout_kernel(x_ref, of_ref, ob_ref, rg_ref, su_ref, sv_ref, ao_ref, mod_ref, wout_ref, wcat_ref, bs_ref,
                   lng_ref, lnb_ref, wq_ref, x1_ref, h2t_ref, qp_ref, *, tiles_per_batch, n_batch):
    i = pl.program_id(0)
    grp = jnp.minimum(i // tiles_per_batch, n_batch)
    avg = _group_avg_matrix(RET_W)
    a = _group_ln(of_ref[...] + ob_ref[...], avg) * _silu(rg_ref[...])
    u = _gelu(su_ref[...])
    vn = _group_ln(_gelu(sv_ref[...]), avg)
    hm = _head_mask_stack(SGU_GROUPS, CHUNK, SGU_W)
    zero_b = jnp.zeros((), BF16)
    parts = []
    for c in range(TOK_TILE // CHUNK):
        vc = vn[c * CHUNK:(c + 1) * CHUNK].astype(BF16)
        vs = jnp.where(hm, jnp.concatenate([vc] * SGU_GROUPS, axis=0), zero_b)
        mix = _dot(wcat_ref[...], vs) + bs_ref[...]
        parts.append(u[c * CHUNK:(c + 1) * CHUNK] * mix)
    bmix = jnp.concatenate(parts, axis=0)
    y = (_dot(a.astype(BF16), wout_ref[0:RET_W, :])
         + _dot(bmix.astype(BF16), wout_ref[RET_W:RET_W + SGU_W, :])
         + _dot(ao_ref[...], wout_ref[RET_W + SGU_W:, :]))
    gate1 = mod_ref[pl.ds(grp, 1), 2 * D_MODEL:3 * D_MODEL]
    x1 = _ln(DEEPNORM_ALPHA * x_ref[...] + gate1 * y) * lng_ref[...] + lnb_ref[...]
    x1_ref[...] = x1
    shift2 = mod_ref[pl.ds(grp, 1), 3 * D_MODEL:4 * D_MODEL]
    scale2 = mod_ref[pl.ds(grp, 1), 4 * D_MODEL:5 * D_MODEL]
    h2f = _ln(x1) * (1.0 + scale2) + shift2
    for k in range(TOK_TILE // H2T_TOK):
        h2t_ref[k] = h2f[k * H2T_TOK:(k + 1) * H2T_TOK].T.astype(BF16)
    qp_ref[...] = _dot(h2f.astype(BF16), wq_ref[...]).astype(BF16)


def _mixout(x_all, o_f, o_b, rg, su, sv, ao, mod_l, w_out, w_cat, b_tab, ln_g, ln_b, wq, *, n_batch, seq):
    n_tok = x_all.shape[0]
    tiles_per_batch = seq // TOK_TILE
    tok = lambda w: pl.BlockSpec((TOK_TILE, w), lambda i: (i, 0))
    full = lambda a: pl.BlockSpec(a.shape, lambda i: (0,) * a.ndim)
    n_q = wq.shape[1]
    return pl.pallas_call(
        functools.partial(_mixout_kernel, tiles_per_batch=tiles_per_batch, n_batch=n_batch),
        grid=(n_tok // TOK_TILE,),
        in_specs=[tok(D_MODEL), tok(RET_W), tok(RET_W), tok(RET_W), tok(SGU_W), tok(SGU_W), tok(ATT_QW),
                  full(mod_l), full(w_out), full(w_cat), full(b_tab), full(ln_g), full(ln_b), full(wq)],
        out_specs=[tok(D_MODEL), pl.BlockSpec((TOK_TILE // H2T_TOK, D_MODEL, H2T_TOK), lambda i: (i, 0, 0)),
                   tok(n_q)],
        out_shape=[jax.ShapeDtypeStruct((n_tok, D_MODEL), F32),
                   jax.ShapeDtypeStruct((n_tok // H2T_TOK, D_MODEL, H2T_TOK), BF16),
                   jax.ShapeDtypeStruct((n_tok, n_q), BF16)],
        compiler_params=pltpu.CompilerParams(dimension_semantics=("arbitrary",), vmem_limit_bytes=VMEM_LIMIT),
        name="mix_out",
    )(x_all, o_f, o_b, rg, su, sv, ao, mod_l, w_out, w_cat, b_tab, ln_g, ln_b, wq)


_CAND_SEGS = [(a, PEER_TOPK // (a + 1)) for a in range(1, 8)]


def _batcher_pairs(n):
    pairs = []
    p = 1
    while p < n:
        k = p
        while k >= 1:
            for j in range(k % p, n - k, 2 * k):
                for i in range(min(k, n - j - k)):
                    if (i + j) // (2 * p) == (i + j + k) // (2 * p):
                        pairs.append((i + j, i + j + k))
            k //= 2
        p *= 2
    return pairs


def _top16_sorted(slabs):
    n = len(slabs)
    x = list(slabs)
    for i, j in _batcher_pairs(n):
        x[i], x[j] = jnp.maximum(x[i], x[j]), jnp.minimum(x[i], x[j])
    shift = 4
    while shift >= 1:
        y = [pltpu.roll(v, shift, 0) for v in x]
        x = [jnp.maximum(x[k], y[n - 1 - k]) for k in range(n)]
        d = n // 2
        while d >= 1:
            for i in range(n):
                if (i & d) == 0:
                    x[i], x[i + d] = jnp.maximum(x[i], x[i + d]), jnp.minimum(x[i], x[i + d])
            d //= 2
        shift //= 2
    return x


def _dup_bf16_words(x):
    hi = pltpu.bitcast(x.astype(BF16).astype(F32), jnp.uint32)
    return hi | (hi >> 16)


def _peer_kernel(h2t_ref, qp_ref, x1_ref, mod_ref, k1_ref, k2_ref, u_ref, vt_ref, lng_ref, lnb_ref, out_ref,
                 s_sc, n_sc, a_sc, rk_sc, b_sc, row_sc, act0_sc, act1_sc, hs0_sc, hs1_sc, acc_sc,
                 *, tiles_per_batch, n_batch, n_blocks):
    t = pl.program_id(0)
    e = pl.program_id(1)
    NG = PEER_TOK // LANES
    K = PEER_TOPK
    half = PEER_QDIM // 2

    @pl.when(e == 0)
    def _prologue():
        acc_sc[...] = jnp.zeros_like(acc_sc)
        act1_sc[...] = jnp.zeros_like(act1_sc)
        hs1_sc[...] = jnp.zeros_like(hs1_sc)
        for h in range(PEER_HEADS):
            for p, kref in ((0, k1_ref), (1, k2_ref)):
                c0 = h * PEER_QDIM + p * half
                sT = _dot_nt(kref[...], qp_ref[:, c0:c0 + half])
                for g in range(NG):
                    s_sc[2 * h + p, g] = sT[:, g * LANES:(g + 1) * LANES]

        row8 = lax.broadcasted_iota(jnp.int32, (8, LANES), 0)

        def stats(it, carry):
            h = it // NG
            g = it % NG
            s1 = s_sc[2 * h, g]
            s2 = s_sc[2 * h + 1, g]
            slabs1 = [s1[8 * k:8 * k + 8] for k in range(K)]
            slabs2 = [s2[8 * k:8 * k + 8] for k in range(K)]
            t1 = _top16_sorted(slabs1)
            t2 = _top16_sorted(slabs2)
            v1 = jnp.concatenate([x[0:1] for x in t1], axis=0)
            v2 = jnp.concatenate([x[0:1] for x in t2], axis=0)
            segs = [v1[0:1] + v2]
            for a, n_a in _CAND_SEGS:
                segs.append(jnp.where(row8 < n_a, v1[a:a + 1] + v2[0:8], NEG_INF))
            segs.append(v1[8:16] + v2[0:1])
            cand = jnp.concatenate(segs, axis=0)
            cur = cand
            tau = None
            for r in range(K):
                tau = jnp.max(cur, axis=0, keepdims=True)
                cur = jnp.where(cur >= tau, NEG_INF, cur)
            cmax = v1[0:1] + v2[0:1]
            z = jnp.sum(jnp.where(cand >= tau, jnp.exp(cand - cmax), 0.0), axis=0, keepdims=True)
            tau8 = jnp.broadcast_to(tau, (8, LANES))
            rz8 = jnp.broadcast_to(1.0 / z, (8, LANES))
            n_parts, a_parts, rk_parts, b_parts = [], [], [], []
            for k in range(K):
                n = jnp.zeros((8, LANES), F32)
                rk = jnp.zeros((8, LANES), F32)
                for bb in range(K):
                    n = jnp.where((slabs1[k] + t2[bb]) >= tau8, float(bb + 1), n)
                    rk = jnp.where(t2[bb] > slabs2[k], float(bb + 1), rk)
                n_parts.append(n)
                rk_parts.append(rk)
                a_parts.append(jnp.exp(slabs1[k] - t1[0]) * rz8)
                b_parts.append(jnp.exp(slabs2[k] - t2[0]))
            n_sc[h, g] = _dup_bf16_words(jnp.concatenate(n_parts, axis=0))
            a_sc[h, g] = _dup_bf16_words(jnp.concatenate(a_parts, axis=0))
            rk_sc[h, g] = jnp.concatenate(rk_parts, axis=0).astype(BF16)
            b_sc[h, g] = jnp.concatenate(b_parts, axis=0).astype(BF16)
            return carry

        lax.fori_loop(0, PEER_HEADS * NG, stats, 0)

    rows_per_blk = PEER_EB // PEER_NKEYS
    kc_rows = 16
    zero_b = jnp.zeros((), BF16)
    blk = jnp.where(e == 0, 1, jnp.where(e == n_blocks + 1, n_blocks - 2, e - 1))
    tile0 = pl.multiple_of((blk // 2) * 8, 8)

    def step(act_w, act_r, hs_w, hs_r, row_off):
        def half(c, carry):
            a_new = _gelu(_dot(u_ref[0], h2t_ref[c])).astype(BF16)
            act_w[2 * c] = a_new[:, 0:LANES]
            act_w[2 * c + 1] = a_new[:, LANES:2 * LANES]
            hs_prev = jnp.concatenate([hs_r[2 * c], hs_r[2 * c + 1]], axis=1)
            acc_sc[c] += _dot(vt_ref[0, 0], hs_prev)
            gate_group(2 * c)
            gate_group(2 * c + 1)
            return carry

        def gate_group(g):
            for h in range(PEER_HEADS):
                n_tile = n_sc[h, g, pl.ds(tile0, 8), :]
                a_tile = a_sc[h, g, pl.ds(tile0, 8), :]
                for r in range(rows_per_blk):
                    k = row_off + r
                    row_sc[0, h, r] = pltpu.bitcast(jnp.broadcast_to(n_tile[k:k + 1, :], (8, LANES)), BF16)
                    row_sc[1, h, r] = pltpu.bitcast(jnp.broadcast_to(a_tile[k:k + 1, :], (8, LANES)), BF16)
            for kc in range(PEER_NKEYS // kc_rows):
                keys = slice(kc * kc_rows, (kc + 1) * kc_rows)
                gates = [jnp.zeros((kc_rows, LANES), BF16) for _ in range(rows_per_blk)]
                for h in range(PEER_HEADS):
                    rk = rk_sc[h, g, keys, :]
                    bb = b_sc[h, g, keys, :]
                    for r in range(rows_per_blk):
                        n_full = jnp.concatenate([row_sc[0, h, r]] * (kc_rows // 16), axis=0)
                        a_full = jnp.concatenate([row_sc[1, h, r]] * (kc_rows // 16), axis=0)
                        gates[r] = gates[r] + jnp.where(rk < n_full, a_full * bb, zero_b)
                for r in range(rows_per_blk):
                    rows = slice(r * PEER_NKEYS + kc * kc_rows, r * PEER_NKEYS + (kc + 1) * kc_rows)
                    hs_w[g, rows, :] = gates[r] * act_r[g, rows, :]

        lax.fori_loop(0, NG // 2, half, 0)

    @pl.when(e % 2 == 0)
    def _even():
        step(act0_sc, act1_sc, hs0_sc, hs1_sc, rows_per_blk)

    @pl.when(e % 2 == 1)
    def _odd():
        step(act1_sc, act0_sc, hs1_sc, hs0_sc, 0)

    @pl.when(e == n_blocks + 1)
    def _epilogue():
        grp = jnp.minimum(t // tiles_per_batch, n_batch)
        gate2 = mod_ref[pl.ds(grp, 1), 5 * D_MODEL:6 * D_MODEL]
        f = jnp.concatenate([acc_sc[c].T for c in range(NG // 2)], axis=0)
        out_ref[...] = _ln(DEEPNORM_ALPHA * x1_ref[...] + gate2 * f) * lng_ref[...] + lnb_ref[...]


def _peer(h2, qp, x1, mod_l, k1, k2, u_bf, vt3, ln_g, ln_b, *, layer, n_batch, seq, n_tiles):
    tiles_per_batch = seq // PEER_TOK
    n_e = PEER_N // PEER_EB
    NG = PEER_TOK // LANES
    tok = lambda w: pl.BlockSpec((PEER_TOK, w), lambda t, e: (t, 0))
    full = lambda a: pl.BlockSpec(a.shape, lambda t, e: (0,) * a.ndim)
    stat = lambda n, dt: pltpu.VMEM((n, NG, PEER_NKEYS, LANES), dt)
    return pl.pallas_call(
        functools.partial(_peer_kernel, tiles_per_batch=tiles_per_batch, n_batch=n_batch, n_blocks=n_e),
        grid=(n_tiles, n_e + 2),
        in_specs=[pl.BlockSpec((PEER_TOK // H2T_TOK, D_MODEL, H2T_TOK), lambda t, e: (t, 0, 0)),
                  tok(qp.shape[1]), tok(D_MODEL), full(mod_l), full(k1), full(k2),
                  pl.BlockSpec((1, PEER_EB, D_MODEL), lambda t, e: (layer, jnp.minimum(e, n_e - 1), 0)),
                  pl.BlockSpec((1, 1, D_MODEL, PEER_EB), lambda t, e: (layer, jnp.clip(e - 2, 0, n_e - 1), 0, 0)),
                  full(ln_g), full(ln_b)],
        out_specs=tok(D_MODEL),
        out_shape=jax.ShapeDtypeStruct((n_tiles * PEER_TOK, D_MODEL), F32),
        scratch_shapes=[stat(2 * PEER_HEADS, F32),
                        stat(PEER_HEADS, jnp.uint32), stat(PEER_HEADS, jnp.uint32),
                        stat(PEER_HEADS, BF16), stat(PEER_HEADS, BF16),
                        pltpu.VMEM((2, PEER_HEADS, PEER_EB // PEER_NKEYS, 16, LANES), BF16),
                        pltpu.VMEM((NG, PEER_EB, LANES), BF16), pltpu.VMEM((NG, PEER_EB, LANES), BF16),
                        pltpu.VMEM((NG, PEER_EB, LANES), BF16), pltpu.VMEM((NG, PEER_EB, LANES), BF16),
                        pltpu.VMEM((NG // 2, D_MODEL, 2 * LANES), F32)],
        compiler_params=pltpu.CompilerParams(dimension_semantics=("arbitrary", "arbitrary"),
                                             vmem_limit_bytes=VMEM_LIMIT),
        name="peer",
    )(h2, qp, x1, mod_l, k1, k2, u_bf, vt3, ln_g, ln_b)


def _rope_tables(seq, ctx_len):
    rows = seq // GRID_W
    row_id = jnp.repeat(jnp.arange(rows), GRID_W).astype(F32)
    col_id = jnp.tile(jnp.arange(GRID_W), rows).astype(F32)
    inv = jnp.power(ROPE_BASE, -jnp.arange(ROPE_PAIRS, dtype=F32) / ROPE_PAIRS)
    ang_r = row_id[:, None] * inv
    ang_c = col_id[:, None] * inv
    cos64 = jnp.concatenate([jnp.cos(ang_r)] * 2 + [jnp.cos(ang_c)] * 2, axis=-1)
    sin64 = jnp.concatenate([jnp.sin(ang_r)] * 2 + [jnp.sin(ang_c)] * 2, axis=-1)
    n_rep = RET_W // HEAD_DIM
    cos_tab = jnp.concatenate([jnp.tile(cos64, (1, n_rep)), jnp.ones((ctx_len, RET_W), F32)], axis=0)
    sin_tab = jnp.concatenate([jnp.tile(sin64, (1, n_rep)), jnp.zeros((ctx_len, RET_W), F32)], axis=0)
    return cos_tab, sin_tab


def _rot_partner(width):
    l = np.arange(width)
    lo = (l % 32) < 16
    partner = np.where(lo, l + 16, l - 16)
    sign = np.where(lo, -1.0, 1.0).astype(np.float32)
    return partner, sign


def _slab_perm():
    new = np.arange(ATT_QW)
    r, rem = new // LANES, new % LANES
    g, d = rem // HEAD_DIM, rem % HEAD_DIM
    return (g * GQA_GROUP + r) * HEAD_DIM + d


def _prep_w_in(w_in):
    o_rq, o_rk, o_aq, o_ak = 0, RET_W, 4 * RET_W + 2 * SGU_W, 4 * RET_W + 2 * SGU_W + ATT_QW
    slab = _slab_perm()
    aq = w_in[..., o_aq:o_aq + ATT_QW]

    def partner(block):
        p, sg = _rot_partner(block.shape[-1])
        return block[..., p] * sg

    base = jnp.concatenate([w_in[..., :o_aq], aq[..., slab], w_in[..., o_ak:]], axis=-1)
    rot = jnp.concatenate([partner(w_in[..., o_rq:o_rq + RET_W]), partner(w_in[..., o_rk:o_rk + RET_W]),
                           partner(aq)[..., slab], partner(w_in[..., o_ak:o_ak + ATT_KVW])], axis=-1)
    return jnp.concatenate([base, rot], axis=-1).astype(BF16)


def kernel(x, c, ctx, c_ctx, w_mod, b_mod, w_in, w_out, ret_decay_fwd, ret_decay_bwd, sgu_w, sgu_b, attn_sink,
           ln_mix_g, ln_mix_b, peer_wq, peer_k1, peer_k2, peer_u, peer_v, ln_ffn_g, ln_ffn_b):
    n_batch, seq, _ = x.shape
    ctx_len = ctx.shape[1]
    depth = w_in.shape[0]
    assert seq % PEER_TOK == 0 and seq % TOK_TILE == 0 and ctx_len % CHUNK == 0
    assert (n_batch * ctx_len) % PEER_TOK == 0 and (n_batch * ctx_len) % TOK_TILE == 0
    assert n_batch + 1 <= 8
    n_lat = n_batch * seq
    kw = dict(n_batch=n_batch, seq=seq)

    x_all = jnp.concatenate([x.reshape(n_lat, D_MODEL), ctx.reshape(n_batch * ctx_len, D_MODEL)], axis=0)
    cond = jnp.concatenate([c, c_ctx[None, :], jnp.zeros((8 - n_batch - 1, D_MODEL), F32)], axis=0)
    mods = _modulations(cond, w_mod, b_mod)

    cos_tab, sin_tab = _rope_tables(seq, n_batch * ctx_len)
    w_ext = _prep_w_in(w_in)
    slab = _slab_perm()
    w_out_p = jnp.concatenate([w_out[:, :RET_W + SGU_W], w_out[:, RET_W + SGU_W:][:, slab]], axis=1).astype(BF16)
    w_cat = jnp.transpose(sgu_w, (0, 2, 1, 3)).reshape(depth, CHUNK, SGU_GROUPS * CHUNK).astype(BF16)
    b_tab = jnp.repeat(jnp.transpose(sgu_b, (0, 2, 1)), HEAD_DIM, axis=2)
    decays = jnp.stack([ret_decay_fwd, ret_decay_bwd], axis=1)
    wq = peer_wq.astype(BF16)
    k1 = peer_k1.astype(BF16)
    k2 = peer_k2.astype(BF16)
    u_bf = peer_u.astype(BF16)
    vt3 = jnp.transpose(peer_v.reshape(depth, PEER_N // PEER_EB, PEER_EB, D_MODEL), (0, 1, 3, 2)).astype(BF16)
    row = lambda a: a.reshape(depth, 1, D_MODEL)
    lmg, lmb, lfg, lfb = row(ln_mix_g), row(ln_mix_b), row(ln_ffn_g), row(ln_ffn_b)

    for l in range(depth):
        last = l == depth - 1
        rq, rk, rv, rg, su, sv, aq, ak, av = _inproj(x_all, mods[l], w_ext[l], cos_tab, sin_tab, **kw)
        o_f, o_b = _retention(decays[l], rq, rk, rv, ctx_len=ctx_len, **kw)
        ao = _attention(attn_sink[l], aq, ak, av, ctx_len=ctx_len, **kw)
        x1, h2, qp = _mixout(x_all, o_f, o_b, rg, su, sv, ao, mods[l], w_out_p[l], w_cat[l], b_tab[l],
                             lmg[l], lmb[l], wq[l], **kw)
        n_tiles = (n_lat if last else x_all.shape[0]) // PEER_TOK
        x_all = _peer(h2, qp, x1, mods[l], k1[l], k2[l], u_bf, vt3, lfg[l], lfb[l], layer=l, n_tiles=n_tiles, **kw)
    return x_all[:n_lat].reshape(n_batch, seq, D_MODEL)
```

```python
import functools

import numpy as np
import jax
import jax.numpy as jnp
from jax import lax
from jax.experimental import pallas as pl
from jax.experimental.pallas import tpu as pltpu

F32 = jnp.float32
BF16 = jnp.bfloat16

D_MODEL = 1024
DEPTH = 4
GRID_W = 64
HEAD_DIM = 64
ROPE_PAIRS = HEAD_DIM // 4
ROPE_BASE = 10000.0
RET_HEADS = 4
CHUNK = 128
RET_W = RET_HEADS * HEAD_DIM
RET_SCALE = HEAD_DIM ** -0.5
SGU_GROUPS = 4
SGU_W = SGU_GROUPS * HEAD_DIM
ATT_Q_HEADS = 8
ATT_KV_HEADS = 2
GQA_GROUP = ATT_Q_HEADS // ATT_KV_HEADS
ATT_QW = ATT_Q_HEADS * HEAD_DIM
ATT_KVW = ATT_KV_HEADS * HEAD_DIM
ATT_SCALE = HEAD_DIM ** -0.5
D_IN = 4 * RET_W + 2 * SGU_W + ATT_QW + 2 * ATT_KVW
D_ROT = 2 * RET_W + ATT_QW + ATT_KVW
PEER_HEADS = 8
PEER_NKEYS = 128
PEER_N = PEER_NKEYS * PEER_NKEYS
PEER_QDIM = 256
PEER_TOPK = 16
LN_EPS = 1e-5
DEEPNORM_ALPHA = (2 * DEPTH) ** 0.25

LANES = 128
TOK_TILE = 512
H2T_TOK = 2 * LANES
PEER_TOK = 512
PEER_EB = 512
VMEM_LIMIT = 56 * 1024 * 1024

NEG_INF = float("-inf")


def _ln(x):
    mu = jnp.mean(x, axis=-1, keepdims=True)
    xc = x - mu
    var = jnp.mean(xc * xc, axis=-1, keepdims=True)
    return xc * lax.rsqrt(var + LN_EPS)


def _gelu(x):
    h = 0.5 * x
    return h + h * jnp.tanh(x * (0.7978845608028654 + (0.7978845608028654 * 0.044715) * (x * x)))


def _silu(x):
    return x * (1.0 / (1.0 + jnp.exp(-x)))


def _dot(a, b):
    return jnp.dot(a, b, preferred_element_type=F32)


def _dot_nt(a, b):
    return lax.dot_general(a, b, (((1,), (1,)), ((), ())), preferred_element_type=F32)


def _dot_tn(a, b):
    return lax.dot_general(a, b, (((0,), (0,)), ((), ())), preferred_element_type=F32)


def _group_mean(z, avg):
    hi = z.astype(BF16)
    lo = (z - hi.astype(F32)).astype(BF16)
    return _dot(hi, avg) + _dot(lo, avg)


def _group_ln(x, avg):
    mu = _group_mean(x, avg)
    xc = x - mu
    var = _group_mean(xc * xc, avg)
    return xc * lax.rsqrt(var + LN_EPS)


def _group_avg_matrix(width):
    r = lax.broadcasted_iota(jnp.int32, (width, width), 0) // HEAD_DIM
    c = lax.broadcasted_iota(jnp.int32, (width, width), 1) // HEAD_DIM
    return jnp.where(r == c, 1.0 / HEAD_DIM, 0.0).astype(BF16)


def _head_mask_stack(n_heads, rows, width):
    r = lax.broadcasted_iota(jnp.int32, (n_heads * rows, width), 0) // rows
    c = lax.broadcasted_iota(jnp.int32, (n_heads * rows, width), 1) // HEAD_DIM
    return r == c


def _mod_kernel(c_ref, w_ref, b_ref, o_ref):
    s = _silu(c_ref[...])
    hi = s.astype(BF16)
    lo = (s - hi.astype(F32)).astype(BF16)
    w = w_ref[0]
    whi = w.astype(BF16)
    wlo = (w - whi.astype(F32)).astype(BF16)
    o_ref[0] = _dot(hi, whi) + _dot(lo, whi) + _dot(hi, wlo) + b_ref[0]


def _modulations(cond_rows, w_mod, b_mod):
    depth = w_mod.shape[0]
    n_rows = cond_rows.shape[0]
    col = 1024
    n_col = w_mod.shape[2] // col
    return pl.pallas_call(
        _mod_kernel,
        grid=(depth, n_col),
        in_specs=[
            pl.BlockSpec((n_rows, D_MODEL), lambda l, j: (0, 0)),
            pl.BlockSpec((1, D_MODEL, col), lambda l, j: (l, 0, j)),
            pl.BlockSpec((1, 1, col), lambda l, j: (l, 0, j)),
        ],
        out_specs=pl.BlockSpec((1, n_rows, col), lambda l, j: (l, 0, j)),
        out_shape=jax.ShapeDtypeStruct((depth, n_rows, w_mod.shape[2]), F32),
        name="adaln_mod",
    )(cond_rows, w_mod, b_mod.reshape(depth, 1, -1))


def _inproj_kernel(x_ref, mod_ref, w_ref, cos_ref, sin_ref,
                   rq_ref, rk_ref, rv_ref, rg_ref, su_ref, sv_ref, aq_ref, ak_ref, av_ref,
                   *, tiles_per_batch, n_batch):
    i = pl.program_id(0)
    grp = jnp.minimum(i // tiles_per_batch, n_batch)
    shift = mod_ref[pl.ds(grp, 1), 0:D_MODEL]
    scale = mod_ref[pl.ds(grp, 1), D_MODEL:2 * D_MODEL]
    h = (_ln(x_ref[...]) * (1.0 + scale) + shift).astype(BF16)
    cos = cos_ref[...]
    sin = sin_ref[...]

    def proj(c0, w):
        return _dot(h, w_ref[:, c0:c0 + w])

    rot0 = D_IN
    rq_ref[...] = (proj(0, RET_W) * cos + proj(rot0, RET_W) * sin).astype(BF16)
    rk_ref[...] = ((proj(RET_W, RET_W) * cos + proj(rot0 + RET_W, RET_W) * sin) * RET_SCALE).astype(BF16)
    rv_ref[...] = proj(2 * RET_W, RET_W).astype(BF16)
    rg_ref[...] = proj(3 * RET_W, RET_W)
    su_ref[...] = proj(4 * RET_W, SGU_W)
    sv_ref[...] = proj(4 * RET_W + SGU_W, SGU_W)
    aq0 = 4 * RET_W + 2 * SGU_W
    aqr = rot0 + 2 * RET_W
    for half in range(ATT_QW // RET_W):
        o = half * RET_W
        aq_ref[:, o:o + RET_W] = ((proj(aq0 + o, RET_W) * cos + proj(aqr + o, RET_W) * sin) * ATT_SCALE).astype(BF16)
    ak0 = aq0 + ATT_QW
    akr = aqr + ATT_QW
    ak_ref[...] = (proj(ak0, ATT_KVW) * cos[:, 0:ATT_KVW] + proj(akr, ATT_KVW) * sin[:, 0:ATT_KVW]).astype(BF16)
    av_ref[...] = proj(ak0 + ATT_KVW, ATT_KVW).astype(BF16)


def _inproj(x_all, mod_l, w_ext, cos_tab, sin_tab, *, n_batch, seq):
    n_tok = x_all.shape[0]
    n_tiles = n_tok // TOK_TILE
    tiles_per_batch = seq // TOK_TILE
    n_lat_tiles = n_batch * tiles_per_batch

    def tab_map(i):
        return (jnp.where(i < n_lat_tiles, i % tiles_per_batch, tiles_per_batch + (i - n_lat_tiles)), 0)

    tok = lambda w: pl.BlockSpec((TOK_TILE, w), lambda i: (i, 0))
    full = lambda a: pl.BlockSpec(a.shape, lambda i: (0,) * a.ndim)
    out_w = [(RET_W, BF16), (RET_W, BF16), (RET_W, BF16), (RET_W, F32), (SGU_W, F32), (SGU_W, F32),
             (ATT_QW, BF16), (ATT_KVW, BF16), (ATT_KVW, BF16)]
    return pl.pallas_call(
        functools.partial(_inproj_kernel, tiles_per_batch=tiles_per_batch, n_batch=n_batch),
        grid=(n_tiles,),
        in_specs=[tok(D_MODEL), full(mod_l), full(w_ext),
                  pl.BlockSpec((TOK_TILE, RET_W), tab_map), pl.BlockSpec((TOK_TILE, RET_W), tab_map)],
        out_specs=[tok(w) for w, _ in out_w],
        out_shape=[jax.ShapeDtypeStruct((n_tok, w), dt) for w, dt in out_w],
        compiler_params=pltpu.CompilerParams(dimension_semantics=("arbitrary",), vmem_limit_bytes=VMEM_LIMIT),
        name="inproj",
    )(x_all, mod_l, w_ext, cos_tab, sin_tab)


def _ret_kernel(dec_ref, qf_ref, kf_ref, vf_ref, qb_ref, kb_ref, vb_ref, of_ref, ob_ref,
                sf_ref, sb_ref, intra_ref, qd_ref, kd_ref, cd_ref):
    b = pl.program_id(0)
    s = pl.program_id(1)
    C = CHUNK
    W = RET_W

    @pl.when((b == 0) & (s == 0))
    def _tables():
        dec = dec_ref[...]
        lg = jnp.minimum(dec, 0.0) - jnp.log(1.0 + jnp.exp(-jnp.abs(dec)))
        lane_head = lax.broadcasted_iota(jnp.int32, (1, W), 1) // HEAD_DIM
        ii = lax.broadcasted_iota(jnp.int32, (C, C), 0)
        jj = lax.broadcasted_iota(jnp.int32, (C, C), 1)
        ri = lax.broadcasted_iota(jnp.int32, (C, W), 0).astype(F32)
        rb = lax.broadcasted_iota(jnp.int32, (W, W), 0) // HEAD_DIM
        cb = lax.broadcasted_iota(jnp.int32, (W, W), 1) // HEAD_DIM
        for d in range(2):
            lgl = jnp.zeros((1, W), F32)
            for hh in range(RET_HEADS):
                lg_h = lg[d:d + 1, hh:hh + 1]
                lgl = lgl + jnp.where(lane_head == hh, lg_h, 0.0)
                rel = (ii - jj) if d == 0 else (jj - ii)
                m = jnp.exp(jnp.maximum(rel, 0).astype(F32) * lg_h)
                intra_ref[d, hh * C:(hh + 1) * C, :] = jnp.where(rel >= 0, m, 0.0)
            if d == 0:
                qd_ref[d] = jnp.exp((ri + 1.0) * lgl)
                kd_ref[d] = jnp.exp((C - 1.0 - ri) * lgl)
            else:
                qd_ref[d] = jnp.exp((C - ri) * lgl)
                kd_ref[d] = jnp.exp(ri * lgl)
            cd_ref[d] = jnp.where(rb == cb, jnp.exp(C * lgl), 0.0)

    @pl.when(s == 0)
    def _zero():
        sf_ref[...] = jnp.zeros_like(sf_ref)
        sb_ref[...] = jnp.zeros_like(sb_ref)

    hm = _head_mask_stack(RET_HEADS, C, W)
    rb = lax.broadcasted_iota(jnp.int32, (W, W), 0) // HEAD_DIM
    cb = lax.broadcasted_iota(jnp.int32, (W, W), 1) // HEAD_DIM
    bd = rb == cb
    zero_b = jnp.zeros((), BF16)

    def direction(d, q_ref, k_ref, v_ref, st_ref, o_ref):
        q = q_ref[...]
        k = k_ref[...]
        v = v_ref[...]
        qs = jnp.where(hm, jnp.concatenate([q] * RET_HEADS, axis=0), zero_b)
        att = (_dot_nt(qs, k) * intra_ref[d]).astype(BF16)
        att = jnp.concatenate([att[hh * C:(hh + 1) * C] for hh in range(RET_HEADS)], axis=1)
        vs = jnp.where(hm, jnp.concatenate([v] * RET_HEADS, axis=0), zero_b)
        st = st_ref[...]
        o = _dot(att, vs) + _dot(q, st.astype(BF16)) * qd_ref[d]
        o_ref[...] = o
        kdec = (k.astype(F32) * kd_ref[d]).astype(BF16)
        st_ref[...] = st * cd_ref[d] + jnp.where(bd, _dot_tn(kdec, v), 0.0)

    direction(0, qf_ref, kf_ref, vf_ref, sf_ref, of_ref)
    direction(1, qb_ref, kb_ref, vb_ref, sb_ref, ob_ref)


def _retention(decays, rq, rk, rv, *, n_batch, seq, ctx_len):
    n_tok = rq.shape[0]
    nl = seq // CHUNK
    nc = ctx_len // CHUNK
    ctx0 = n_batch * nl

    def fwd_map(b, s):
        return (jnp.where(s < nc, ctx0 + b * nc + s, b * nl + (s - nc)), 0)

    def bwd_map(b, s):
        return (jnp.where(s < nc, ctx0 + b * nc + (nc - 1 - s), b * nl + (nl - 1 - (s - nc))), 0)

    fspec = pl.BlockSpec((CHUNK, RET_W), fwd_map)
    bspec = pl.BlockSpec((CHUNK, RET_W), bwd_map)
    return pl.pallas_call(
        _ret_kernel,
        grid=(n_batch, nc + nl),
        in_specs=[pl.BlockSpec(decays.shape, lambda b, s: (0, 0)), fspec, fspec, fspec, bspec, bspec, bspec],
        out_specs=[fspec, bspec],
        out_shape=[jax.ShapeDtypeStruct((n_tok, RET_W), F32)] * 2,
        scratch_shapes=[
            pltpu.VMEM((RET_W, RET_W), F32), pltpu.VMEM((RET_W, RET_W), F32),
            pltpu.VMEM((2, RET_HEADS * CHUNK, CHUNK), F32),
            pltpu.VMEM((2, CHUNK, RET_W), F32), pltpu.VMEM((2, CHUNK, RET_W), F32),
            pltpu.VMEM((2, RET_W, RET_W), F32),
        ],
        compiler_params=pltpu.CompilerParams(dimension_semantics=("arbitrary", "arbitrary")),
        name="retention",
    )(decays, rq, rk, rv, rq, rk, rv)


def _attn_kernel(sink_ref, q_ref, kp_ref, kc_ref, kn_ref, vp_ref, vc_ref, vn_ref, kx_ref, vx_ref, o_ref,
                 *, n_lat_blocks, seq):
    n = pl.program_id(1)
    W = CHUNK
    qi = lax.broadcasted_iota(jnp.int32, (W, 3 * W), 0)
    kj = lax.broadcasted_iota(jnp.int32, (W, 3 * W), 1)
    rel = kj - W - qi
    kpos = n * W - W + kj
    mask = (jnp.abs(rel) <= W) & (kpos >= 0) & (kpos < seq) & (n < n_lat_blocks)
    keys = jnp.concatenate([kp_ref[...], kc_ref[...], kn_ref[...], kx_ref[...]], axis=0)
    vals = jnp.concatenate([vp_ref[...], vc_ref[...], vn_ref[...], vx_ref[...]], axis=0)
    lane_head = lax.broadcasted_iota(jnp.int32, (1, LANES), 1) // HEAD_DIM
    zero_b = jnp.zeros((), BF16)
    slabs = [q_ref[:, r * LANES:(r + 1) * LANES] for r in range(GQA_GROUP)]
    mask4 = jnp.concatenate([mask] * GQA_GROUP, axis=0)
    outs = [jnp.zeros((W, LANES), F32) for _ in range(GQA_GROUP)]
    for g in range(ATT_KV_HEADS):
        mg = lane_head == g
        qs = jnp.concatenate([jnp.where(mg, s, zero_b) for s in slabs], axis=0)
        sink = jnp.concatenate([jnp.full((W, 1), sink_ref[g * GQA_GROUP + r], F32) for r in range(GQA_GROUP)], axis=0)
        sc = _dot_nt(qs, keys)
        s_loc = jnp.where(mask4, sc[:, 0:3 * W], NEG_INF)
        s_ctx = sc[:, 3 * W:]
        m = jnp.maximum(jnp.maximum(jnp.max(s_loc, axis=-1, keepdims=True),
                                    jnp.max(s_ctx, axis=-1, keepdims=True)), sink)
        p_loc = jnp.exp(s_loc - m)
        p_ctx = jnp.exp(s_ctx - m)
        den = (jnp.sum(p_loc, axis=-1, keepdims=True) + jnp.sum(p_ctx, axis=-1, keepdims=True)
               + jnp.exp(sink - m))
        p = jnp.concatenate([p_loc, p_ctx], axis=1).astype(BF16)
        o = _dot(p, vals) * (1.0 / den)
        for r in range(GQA_GROUP):
            outs[r] = outs[r] + jnp.where(mg, o[r * W:(r + 1) * W], 0.0)
    for r in range(GQA_GROUP):
        o_ref[:, r * LANES:(r + 1) * LANES] = outs[r].astype(BF16)


def _attention(sink, aq, ak, av, *, n_batch, seq, ctx_len):
    n_tok = aq.shape[0]
    nl = seq // CHUNK
    nc = ctx_len // CHUNK
    ctx0 = n_batch * nl

    def q_map(b, n):
        return (jnp.where(n < nl, b * nl + n, ctx0 + b * nc + (n - nl)), 0)

    def k_map(off):
        def f(b, n):
            return (b * nl + jnp.clip(n + off, 0, nl - 1), 0)
        return f

    x_map = lambda b, n: (n_batch * seq // ctx_len + b, 0)
    kv = lambda off: pl.BlockSpec((CHUNK, ATT_KVW), k_map(off))
    xspec = pl.BlockSpec((ctx_len, ATT_KVW), x_map)
    return pl.pallas_call(
        functools.partial(_attn_kernel, n_lat_blocks=nl, seq=seq),
        grid=(n_batch, nl + nc),
        in_specs=[pl.BlockSpec(memory_space=pltpu.SMEM), pl.BlockSpec((CHUNK, ATT_QW), q_map),
                  kv(-1), kv(0), kv(1), kv(-1), kv(0), kv(1), xspec, xspec],
        out_specs=pl.BlockSpec((CHUNK, ATT_QW), q_map),
        out_shape=jax.ShapeDtypeStruct((n_tok, ATT_QW), BF16),
        compiler_params=pltpu.CompilerParams(dimension_semantics=("arbitrary", "arbitrary")),
        name="window_attn",
    )(sink, aq, ak, ak, ak, av, av, av, ak, av)


def _mixout_kernel(x_ref, of_ref, ob_ref, rg_ref, su_ref, sv_ref, ao_ref, mod_ref, wout_ref, wcat_ref, bs_ref,
                   lng_ref, lnb_ref, wq_ref, x1_ref, h2t_ref, qp_ref, *, tiles_per_batch, n_batch):
    i = pl.program_id(0)
    grp = jnp.minimum(i // tiles_per_batch, n_batch)
    avg = _group_avg_matrix(RET_W)
    a = _group_ln(of_ref[...] + ob_ref[...], avg) * _silu(rg_ref[...])
    u = _gelu(su_ref[...])
    vn = _group_ln(_gelu(sv_ref[...]), avg)
    hm = _head_mask_stack(SGU_GROUPS, CHUNK, SGU_W)
    zero_b = jnp.zeros((), BF16)
    parts = []
    for c in range(TOK_TILE // CHUNK):
        vc = vn[c * CHUNK:(c + 1) * CHUNK].astype(BF16)
        vs = jnp.where(hm, jnp.concatenate([vc] * SGU_GROUPS, axis=0), zero_b)
        mix = _dot(wcat_ref[...], vs) + bs_ref[...]
        parts.append(u[c * CHUNK:(c + 1) * CHUNK] * mix)
    bmix = jnp.concatenate(parts, axis=0)
    y = (_dot(a.astype(BF16), wout_ref[0:RET_W, :])
         + _dot(bmix.astype(BF16), wout_ref[RET_W:RET_W + SGU_W, :])
         + _dot(ao_ref[...], wout_ref[RET_W + SGU_W:, :]))
    gate1 = mod_ref[pl.ds(grp, 1), 2 * D_MODEL:3 * D_MODEL]
    x1 = _ln(DEEPNORM_ALPHA * x_ref[...] + gate1 * y) * lng_ref[...] + lnb_ref[...]
    x1_ref[...] = x1
    shift2 = mod_ref[pl.ds(grp, 1), 3 * D_MODEL:4 * D_MODEL]
    scale2 = mod_ref[pl.ds(grp, 1), 4 * D_MODEL:5 * D_MODEL]
    h2f = _ln(x1) * (1.0 + scale2) + shift2
    for k in range(TOK_TILE // H2T_TOK):
        h2t_ref[k] = h2f[k * H2T_TOK:(k + 1) * H2T_TOK].T.astype(BF16)
    qp_ref[...] = _dot(h2f.astype(BF16), wq_ref[...]).astype(BF16)


def _mixout(x_all, o_f, o_b, rg, su, sv, ao, mod_l, w_out, w_cat, b_tab, ln_g, ln_b, wq, *, n_batch, seq):
    n_tok = x_all.shape[0]
    tiles_per_batch = seq // TOK_TILE
    tok = lambda w: pl.BlockSpec((TOK_TILE, w), lambda i: (i, 0))
    full = lambda a: pl.BlockSpec(a.shape, lambda i: (0,) * a.ndim)
    n_q = wq.shape[1]
    return pl.pallas_call(
        functools.partial(_mixout_kernel, tiles_per_batch=tiles_per_batch, n_batch=n_batch),
        grid=(n_tok // TOK_TILE,),
        in_specs=[tok(D_MODEL), tok(RET_W), tok(RET_W), tok(RET_W), tok(SGU_W), tok(SGU_W), tok(ATT_QW),
                  full(mod_l), full(w_out), full(w_cat), full(b_tab), full(ln_g), full(ln_b), full(wq)],
        out_specs=[tok(D_MODEL), pl.BlockSpec((TOK_TILE // H2T_TOK, D_MODEL, H2T_TOK), lambda i: (i, 0, 0)),
                   tok(n_q)],
        out_shape=[jax.ShapeDtypeStruct((n_tok, D_MODEL), F32),
                   jax.ShapeDtypeStruct((n_tok // H2T_TOK, D_MODEL, H2T_TOK), BF16),
                   jax.ShapeDtypeStruct((n_tok, n_q), BF16)],
        compiler_params=pltpu.CompilerParams(dimension_semantics=("arbitrary",), vmem_limit_bytes=VMEM_LIMIT),
        name="mix_out",
    )(x_all, o_f, o_b, rg, su, sv, ao, mod_l, w_out, w_cat, b_tab, ln_g, ln_b, wq)


_CAND_SEGS = [(a, PEER_TOPK // (a + 1)) for a in range(1, 8)]


def _batcher_pairs(n):
    pairs = []
    p = 1
    while p < n:
        k = p
        while k >= 1:
            for j in range(k % p, n - k, 2 * k):
                for i in range(min(k, n - j - k)):
                    if (i + j) // (2 * p) == (i + j + k) // (2 * p):
                        pairs.append((i + j, i + j + k))
            k //= 2
        p *= 2
    return pairs


def _top16_sorted(slabs):
    n = len(slabs)
    x = list(slabs)
    for i, j in _batcher_pairs(n):
        x[i], x[j] = jnp.maximum(x[i], x[j]), jnp.minimum(x[i], x[j])
    shift = 4
    while shift >= 1:
        y = [pltpu.roll(v, shift, 0) for v in x]
        x = [jnp.maximum(x[k], y[n - 1 - k]) for k in range(n)]
        d = n // 2
        while d >= 1:
            for i in range(n):
                if (i & d) == 0:
                    x[i], x[i + d] = jnp.maximum(x[i], x[i + d]), jnp.minimum(x[i], x[i + d])
            d //= 2
        shift //= 2
    return x


def _dup_bf16_words(x):
    hi = pltpu.bitcast(x.astype(BF16).astype(F32), jnp.uint32)
    return hi | (hi >> 16)


def _peer_kernel(h2t_ref, qp_ref, x1_ref, mod_ref, k1_ref, k2_ref, u_ref, vt_ref, lng_ref, lnb_ref, out_ref,
                 s_sc, n_sc, a_sc, rk_sc, b_sc, row_sc, act0_sc, act1_sc, hs0_sc, hs1_sc, acc_sc,
                 *, tiles_per_batch, n_batch, n_blocks):
    t = pl.program_id(0)
    e = pl.program_id(1)
    NG = PEER_TOK // LANES
    K = PEER_TOPK
    half = PEER_QDIM // 2

    @pl.when(e == 0)
    def _prologue():
        acc_sc[...] = jnp.zeros_like(acc_sc)
        for h in range(PEER_HEADS):
            for p, kref in ((0, k1_ref), (1, k2_ref)):
                c0 = h * PEER_QDIM + p * half
                sT = _dot_nt(kref[...], qp_ref[:, c0:c0 + half])
                for g in range(NG):
                    s_sc[2 * h + p, g] = sT[:, g * LANES:(g + 1) * LANES]

        row8 = lax.broadcasted_iota(jnp.int32, (8, LANES), 0)

        def stats(it, carry):
            h = it // NG
            g = it % NG
            s1 = s_sc[2 * h, g]
            s2 = s_sc[2 * h + 1, g]
            slabs1 = [s1[8 * k:8 * k + 8] for k in range(K)]
            slabs2 = [s2[8 * k:8 * k + 8] for k in range(K)]
            t1 = _top16_sorted(slabs1)
            t2 = _top16_sorted(slabs2)
            v1 = jnp.concatenate([x[0:1] for x in t1], axis=0)
            v2 = jnp.concatenate([x[0:1] for x in t2], axis=0)
            segs = [v1[0:1] + v2]
            for a, n_a in _CAND_SEGS:
                segs.append(jnp.where(row8 < n_a, v1[a:a + 1] + v2[0:8], NEG_INF))
            segs.append(v1[8:16] + v2[0:1])
            cand = jnp.concatenate(segs, axis=0)
            cur = cand
            tau = None
            for r in range(K):
                tau = jnp.max(cur, axis=0, keepdims=True)
                cur = jnp.where(cur >= tau, NEG_INF, cur)
            cmax = v1[0:1] + v2[0:1]
            z = jnp.sum(jnp.where(cand >= tau, jnp.exp(cand - cmax), 0.0), axis=0, keepdims=True)
            tau8 = jnp.broadcast_to(tau, (8, LANES))
            rz8 = jnp.broadcast_to(1.0 / z, (8, LANES))
            n_parts, a_parts, rk_parts, b_parts = [], [], [], []
            for k in range(K):
                n = jnp.zeros((8, LANES), F32)
                rk = jnp.zeros((8, LANES), F32)
                for bb in range(K):
                    n = jnp.where((slabs1[k] + t2[bb]) >= tau8, float(bb + 1), n)
                    rk = jnp.where(t2[bb] > slabs2[k], float(bb + 1), rk)
                n_parts.append(n)
                rk_parts.append(rk)
                a_parts.append(jnp.exp(slabs1[k] - t1[0]) * rz8)
                b_parts.append(jnp.exp(slabs2[k] - t2[0]))
            n_sc[h, g] = _dup_bf16_words(jnp.concatenate(n_parts, axis=0))
            a_sc[h, g] = _dup_bf16_words(jnp.concatenate(a_parts, axis=0))
            rk_sc[h, g] = jnp.concatenate(rk_parts, axis=0).astype(BF16)
            b_sc[h, g] = jnp.concatenate(b_parts, axis=0).astype(BF16)
            return carry

        lax.fori_loop(0, PEER_HEADS * NG, stats, 0)

    rows_per_blk = PEER_EB // PEER_NKEYS
    kc_rows = 16
    zero_b = jnp.zeros((), BF16)
    tile0 = pl.multiple_of((jnp.clip(e - 1, 0, n_blocks - 1) // 2) * 8, 8)

    def step(act_w, act_r, hs_w, hs_r, row_off, first=True, gates=True, second=True):
        def half(c, carry):
            if first:
                a_new = _gelu(_dot(u_ref[0], h2t_ref[c])).astype(BF16)
                act_w[2 * c] = a_new[:, 0:LANES]
                act_w[2 * c + 1] = a_new[:, LANES:2 * LANES]
            if second:
                hs_prev = jnp.concatenate([hs_r[2 * c], hs_r[2 * c + 1]], axis=1)
                acc_sc[c] += _dot(vt_ref[0, 0], hs_prev)
            if gates:
                gate_group(2 * c)
                gate_group(2 * c + 1)
            return carry

        def gate_group(g):
            for h in range(PEER_HEADS):
                n_tile = n_sc[h, g, pl.ds(tile0, 8), :]
                a_tile = a_sc[h, g, pl.ds(tile0, 8), :]
                for r in range(rows_per_blk):
                    k = row_off + r
                    row_sc[0, h, r] = pltpu.bitcast(jnp.broadcast_to(n_tile[k:k + 1, :], (8, LANES)), BF16)
                    row_sc[1, h, r] = pltpu.bitcast(jnp.broadcast_to(a_tile[k:k + 1, :], (8, LANES)), BF16)
            for kc in range(PEER_NKEYS // kc_rows):
                keys = slice(kc * kc_rows, (kc + 1) * kc_rows)
                gates = [jnp.zeros((kc_rows, LANES), BF16) for _ in range(rows_per_blk)]
                for h in range(PEER_HEADS):
                    rk = rk_sc[h, g, keys, :]
                    bb = b_sc[h, g, keys, :]
                    for r in range(rows_per_blk):
                        n_full = jnp.concatenate([row_sc[0, h, r]] * (kc_rows // 16), axis=0)
                        a_full = jnp.concatenate([row_sc[1, h, r]] * (kc_rows // 16), axis=0)
                        gates[r] = gates[r] + jnp.where(rk < n_full, a_full * bb, zero_b)
                for r in range(rows_per_blk):
                    rows = slice(r * PEER_NKEYS + kc * kc_rows, r * PEER_NKEYS + (kc + 1) * kc_rows)
                    hs_w[g, rows, :] = gates[r] * act_r[g, rows, :]

        lax.fori_loop(0, NG // 2, half, 0)

    even = (act0_sc, act1_sc, hs0_sc, hs1_sc, rows_per_blk)
    odd = (act1_sc, act0_sc, hs1_sc, hs0_sc, 0)
    steady = (e >= 2) & (e < n_blocks)

    @pl.when(e == 0)
    def _fill0():
        step(*even, gates=False, second=False)

    @pl.when(e == 1)
    def _fill1():
        step(*odd, second=False)

    @pl.when(steady & (e % 2 == 0))
    def _even():
        step(*even)

    @pl.when(steady & (e % 2 == 1))
    def _odd():
        step(*odd)

    @pl.when(e == n_blocks)
    def _drain0():
        step(*even, first=False)

    @pl.when(e == n_blocks + 1)
    def _drain1():
        step(*odd, first=False, gates=False)

    @pl.when(e == n_blocks + 1)
    def _epilogue():
        grp = jnp.minimum(t // tiles_per_batch, n_batch)
        gate2 = mod_ref[pl.ds(grp, 1), 5 * D_MODEL:6 * D_MODEL]
        f = jnp.concatenate([acc_sc[c].T for c in range(NG // 2)], axis=0)
        out_ref[...] = _ln(DEEPNORM_ALPHA * x1_ref[...] + gate2 * f) * lng_ref[...] + lnb_ref[...]


def _peer(h2, qp, x1, mod_l, k1, k2, u_bf, vt3, ln_g, ln_b, *, layer, n_batch, seq, n_tiles):
    tiles_per_batch = seq // PEER_TOK
    n_e = PEER_N // PEER_EB
    assert n_e % 2 == 0 and n_e >= 4
    NG = PEER_TOK // LANES
    tok = lambda w: pl.BlockSpec((PEER_TOK, w), lambda t, e: (t, 0))
    full = lambda a: pl.BlockSpec(a.shape, lambda t, e: (0,) * a.ndim)
    stat = lambda n, dt: pltpu.VMEM((n, NG, PEER_NKEYS, LANES), dt)
    return pl.pallas_call(
        functools.partial(_peer_kernel, tiles_per_batch=tiles_per_batch, n_batch=n_batch, n_blocks=n_e),
        grid=(n_tiles, n_e + 2),
        in_specs=[pl.BlockSpec((PEER_TOK // H2T_TOK, D_MODEL, H2T_TOK), lambda t, e: (t, 0, 0)),
                  tok(qp.shape[1]), tok(D_MODEL), full(mod_l), full(k1), full(k2),
                  pl.BlockSpec((1, PEER_EB, D_MODEL), lambda t, e: (layer, jnp.minimum(e, n_e - 1), 0)),
                  pl.BlockSpec((1, 1, D_MODEL, PEER_EB), lambda t, e: (layer, jnp.clip(e - 2, 0, n_e - 1), 0, 0)),
                  full(ln_g), full(ln_b)],
        out_specs=tok(D_MODEL),
        out_shape=jax.ShapeDtypeStruct((n_tiles * PEER_TOK, D_MODEL), F32),
        scratch_shapes=[stat(2 * PEER_HEADS, F32),
                        stat(PEER_HEADS, jnp.uint32), stat(PEER_HEADS, jnp.uint32),
                        stat(PEER_HEADS, BF16), stat(PEER_HEADS, BF16),
                        pltpu.VMEM((2, PEER_HEADS, PEER_EB // PEER_NKEYS, 16, LANES), BF16),
                        pltpu.VMEM((NG, PEER_EB, LANES), BF16), pltpu.VMEM((NG, PEER_EB, LANES), BF16),
                        pltpu.VMEM((NG, PEER_EB, LANES), BF16), pltpu.VMEM((NG, PEER_EB, LANES), BF16),
                        pltpu.VMEM((NG // 2, D_MODEL, 2 * LANES), F32)],
        compiler_params=pltpu.CompilerParams(dimension_semantics=("arbitrary", "arbitrary"),
                                             vmem_limit_bytes=VMEM_LIMIT),
        name="peer",
    )(h2, qp, x1, mod_l, k1, k2, u_bf, vt3, ln_g, ln_b)


def _rope_tables(seq, ctx_len):
    rows = seq // GRID_W
    row_id = jnp.repeat(jnp.arange(rows), GRID_W).astype(F32)
    col_id = jnp.tile(jnp.arange(GRID_W), rows).astype(F32)
    inv = jnp.power(ROPE_BASE, -jnp.arange(ROPE_PAIRS, dtype=F32) / ROPE_PAIRS)
    ang_r = row_id[:, None] * inv
    ang_c = col_id[:, None] * inv
    cos64 = jnp.concatenate([jnp.cos(ang_r)] * 2 + [jnp.cos(ang_c)] * 2, axis=-1)
    sin64 = jnp.concatenate([jnp.sin(ang_r)] * 2 + [jnp.sin(ang_c)] * 2, axis=-1)
    n_rep = RET_W // HEAD_DIM
    cos_tab = jnp.concatenate([jnp.tile(cos64, (1, n_rep)), jnp.ones((ctx_len, RET_W), F32)], axis=0)
    sin_tab = jnp.concatenate([jnp.tile(sin64, (1, n_rep)), jnp.zeros((ctx_len, RET_W), F32)], axis=0)
    return cos_tab, sin_tab


def _rot_partner(width):
    l = np.arange(width)
    lo = (l % 32) < 16
    partner = np.where(lo, l + 16, l - 16)
    sign = np.where(lo, -1.0, 1.0).astype(np.float32)
    return partner, sign


def _slab_perm():
    new = np.arange(ATT_QW)
    r, rem = new // LANES, new % LANES
    g, d = rem // HEAD_DIM, rem % HEAD_DIM
    return (g * GQA_GROUP + r) * HEAD_DIM + d


def _prep_w_in(w_in):
    o_rq, o_rk, o_aq, o_ak = 0, RET_W, 4 * RET_W + 2 * SGU_W, 4 * RET_W + 2 * SGU_W + ATT_QW
    slab = _slab_perm()
    aq = w_in[..., o_aq:o_aq + ATT_QW]

    def partner(block):
        p, sg = _rot_partner(block.shape[-1])
        return block[..., p] * sg

    base = jnp.concatenate([w_in[..., :o_aq], aq[..., slab], w_in[..., o_ak:]], axis=-1)
    rot = jnp.concatenate([partner(w_in[..., o_rq:o_rq + RET_W]), partner(w_in[..., o_rk:o_rk + RET_W]),
                           partner(aq)[..., slab], partner(w_in[..., o_ak:o_ak + ATT_KVW])], axis=-1)
    return jnp.concatenate([base, rot], axis=-1).astype(BF16)


def kernel(x, c, ctx, c_ctx, w_mod, b_mod, w_in, w_out, ret_decay_fwd, ret_decay_bwd, sgu_w, sgu_b, attn_sink,
           ln_mix_g, ln_mix_b, peer_wq, peer_k1, peer_k2, peer_u, peer_v, ln_ffn_g, ln_ffn_b):
    n_batch, seq, _ = x.shape
    ctx_len = ctx.shape[1]
    depth = w_in.shape[0]
    assert seq % PEER_TOK == 0 and seq % TOK_TILE == 0 and ctx_len % CHUNK == 0
    assert (n_batch * ctx_len) % PEER_TOK == 0 and (n_batch * ctx_len) % TOK_TILE == 0
    assert n_batch + 1 <= 8
    n_lat = n_batch * seq
    kw = dict(n_batch=n_batch, seq=seq)

    x_all = jnp.concatenate([x.reshape(n_lat, D_MODEL), ctx.reshape(n_batch * ctx_len, D_MODEL)], axis=0)
    cond = jnp.concatenate([c, c_ctx[None, :], jnp.zeros((8 - n_batch - 1, D_MODEL), F32)], axis=0)
    mods = _modulations(cond, w_mod, b_mod)

    cos_tab, sin_tab = _rope_tables(seq, n_batch * ctx_len)
    w_ext = _prep_w_in(w_in)
    slab = _slab_perm()
    w_out_p = jnp.concatenate([w_out[:, :RET_W + SGU_W], w_out[:, RET_W + SGU_W:][:, slab]], axis=1).astype(BF16)
    w_cat = jnp.transpose(sgu_w, (0, 2, 1, 3)).reshape(depth, CHUNK, SGU_GROUPS * CHUNK).astype(BF16)
    b_tab = jnp.repeat(jnp.transpose(sgu_b, (0, 2, 1)), HEAD_DIM, axis=2)
    decays = jnp.stack([ret_decay_fwd, ret_decay_bwd], axis=1)
    wq = peer_wq.astype(BF16)
    k1 = peer_k1.astype(BF16)
    k2 = peer_k2.astype(BF16)
    u_bf = peer_u.astype(BF16)
    vt3 = jnp.transpose(peer_v.reshape(depth, PEER_N // PEER_EB, PEER_EB, D_MODEL), (0, 1, 3, 2)).astype(BF16)
    row = lambda a: a.reshape(depth, 1, D_MODEL)
    lmg, lmb, lfg, lfb = row(ln_mix_g), row(ln_mix_b), row(ln_ffn_g), row(ln_ffn_b)

    for l in range(depth):
        last = l == depth - 1
        rq, rk, rv, rg, su, sv, aq, ak, av = _inproj(x_all, mods[l], w_ext[l], cos_tab, sin_tab, **kw)
        o_f, o_b = _retention(decays[l], rq, rk, rv, ctx_len=ctx_len, **kw)
        ao = _attention(attn_sink[l], aq, ak, av, ctx_len=ctx_len, **kw)
        x1, h2, qp = _mixout(x_all, o_f, o_b, rg, su, sv, ao, mods[l], w_out_p[l], w_cat[l], b_tab[l],
                             lmg[l], lmb[l], wq[l], **kw)
        n_tiles = (n_lat if last else x_all.shape[0]) // PEER_TOK
        x_all = _peer(h2, qp, x1, mods[l], k1[l], k2[l], u_bf, vt3, lfg[l], lfb[l], layer=l, n_tiles=n_tiles, **kw)
    return x_all[:n_lat].reshape(n_batch, seq, D_MODEL)
```

```python
import functools

import numpy as np
import jax
import jax.numpy as jnp
from jax import lax
from jax.experimental import pallas as pl
from jax.experimental.pallas import tpu as pltpu

F32 = jnp.float32
BF16 = jnp.bfloat16

D_MODEL = 1024
DEPTH = 4
GRID_W = 64
HEAD_DIM = 64
ROPE_PAIRS = HEAD_DIM // 4
ROPE_BASE = 10000.0
RET_HEADS = 4
CHUNK = 128
RET_W = RET_HEADS * HEAD_DIM
RET_SCALE = HEAD_DIM ** -0.5
SGU_GROUPS = 4
SGU_W = SGU_GROUPS * HEAD_DIM
ATT_Q_HEADS = 8
ATT_KV_HEADS = 2
GQA_GROUP = ATT_Q_HEADS // ATT_KV_HEADS
ATT_QW = ATT_Q_HEADS * HEAD_DIM
ATT_KVW = ATT_KV_HEADS * HEAD_DIM
ATT_SCALE = HEAD_DIM ** -0.5
D_IN = 4 * RET_W + 2 * SGU_W + ATT_QW + 2 * ATT_KVW
PEER_HEADS = 8
PEER_NKEYS = 128
PEER_N = PEER_NKEYS * PEER_NKEYS
PEER_QDIM = 256
PEER_TOPK = 16
LN_EPS = 1e-5
DEEPNORM_ALPHA = (2 * DEPTH) ** 0.25

LANES = 128
SUBLANES = 8
PACKED_ROWS = 2 * SUBLANES
TOK_TILE = 512
H2T_TOK = 2 * LANES
PEER_TOK = 512
PEER_EB = 2048
GATE_ROWS = 4
VMEM_LIMIT = 56 * 1024 * 1024

NEG_INF = float("-inf")


def _ln(x):
    mu = jnp.mean(x, axis=-1, keepdims=True)
    xc = x - mu
    var = jnp.mean(xc * xc, axis=-1, keepdims=True)
    return xc * lax.rsqrt(var + LN_EPS)


def _gelu(x):
    h = 0.5 * x
    return h + h * jnp.tanh(x * (0.7978845608028654 + (0.7978845608028654 * 0.044715) * (x * x)))


def _silu(x):
    return x * (1.0 / (1.0 + jnp.exp(-x)))


def _dot(a, b):
    return jnp.dot(a, b, preferred_element_type=F32)


def _dot_nt(a, b):
    return lax.dot_general(a, b, (((1,), (1,)), ((), ())), preferred_element_type=F32)


def _dot_tn(a, b):
    return lax.dot_general(a, b, (((0,), (0,)), ((), ())), preferred_element_type=F32)


def _group_mean(z, avg):
    hi = z.astype(BF16)
    lo = (z - hi.astype(F32)).astype(BF16)
    return _dot(hi, avg) + _dot(lo, avg)


def _group_ln(x, avg):
    mu = _group_mean(x, avg)
    xc = x - mu
    var = _group_mean(xc * xc, avg)
    return xc * lax.rsqrt(var + LN_EPS)


def _group_avg_matrix(width):
    r = lax.broadcasted_iota(jnp.int32, (width, width), 0) // HEAD_DIM
    c = lax.broadcasted_iota(jnp.int32, (width, width), 1) // HEAD_DIM
    return jnp.where(r == c, 1.0 / HEAD_DIM, 0.0).astype(BF16)


def _head_mask_stack(n_heads, rows, width):
    r = lax.broadcasted_iota(jnp.int32, (n_heads * rows, width), 0) // rows
    c = lax.broadcasted_iota(jnp.int32, (n_heads * rows, width), 1) // HEAD_DIM
    return r == c


def _mod_kernel(c_ref, w_ref, b_ref, o_ref):
    s = _silu(c_ref[...])
    hi = s.astype(BF16)
    lo = (s - hi.astype(F32)).astype(BF16)
    w = w_ref[0]
    whi = w.astype(BF16)
    wlo = (w - whi.astype(F32)).astype(BF16)
    o_ref[0] = _dot(hi, whi) + _dot(lo, whi) + _dot(hi, wlo) + b_ref[0]


def _modulations(cond_rows, w_mod, b_mod):
    depth = w_mod.shape[0]
    n_rows = cond_rows.shape[0]
    col = 1024
    n_col = w_mod.shape[2] // col
    return pl.pallas_call(
        _mod_kernel,
        grid=(depth, n_col),
        in_specs=[
            pl.BlockSpec((n_rows, D_MODEL), lambda l, j: (0, 0)),
            pl.BlockSpec((1, D_MODEL, col), lambda l, j: (l, 0, j)),
            pl.BlockSpec((1, 1, col), lambda l, j: (l, 0, j)),
        ],
        out_specs=pl.BlockSpec((1, n_rows, col), lambda l, j: (l, 0, j)),
        out_shape=jax.ShapeDtypeStruct((depth, n_rows, w_mod.shape[2]), F32),
        name="adaln_mod",
    )(cond_rows, w_mod, b_mod.reshape(depth, 1, -1))


def _inproj_kernel(x_ref, mod_ref, w_ref, cos_ref, sin_ref,
                   rq_ref, rk_ref, rv_ref, rg_ref, su_ref, sv_ref, aq_ref, ak_ref, av_ref,
                   *, tiles_per_batch, n_batch):
    i = pl.program_id(0)
    grp = jnp.minimum(i // tiles_per_batch, n_batch)
    shift = mod_ref[pl.ds(grp, 1), 0:D_MODEL]
    scale = mod_ref[pl.ds(grp, 1), D_MODEL:2 * D_MODEL]
    h = (_ln(x_ref[...]) * (1.0 + scale) + shift).astype(BF16)
    cos = cos_ref[...]
    sin = sin_ref[...]

    def proj(c0, w):
        return _dot(h, w_ref[:, c0:c0 + w])

    rot0 = D_IN
    rq_ref[...] = (proj(0, RET_W) * cos + proj(rot0, RET_W) * sin).astype(BF16)
    rk_ref[...] = ((proj(RET_W, RET_W) * cos + proj(rot0 + RET_W, RET_W) * sin) * RET_SCALE).astype(BF16)
    rv_ref[...] = proj(2 * RET_W, RET_W).astype(BF16)
    rg_ref[...] = proj(3 * RET_W, RET_W)
    su_ref[...] = proj(4 * RET_W, SGU_W)
    sv_ref[...] = proj(4 * RET_W + SGU_W, SGU_W)
    aq0 = 4 * RET_W + 2 * SGU_W
    aqr = rot0 + 2 * RET_W
    for half in range(ATT_QW // RET_W):
        o = half * RET_W
        aq_ref[:, o:o + RET_W] = ((proj(aq0 + o, RET_W) * cos + proj(aqr + o, RET_W) * sin) * ATT_SCALE).astype(BF16)
    ak0 = aq0 + ATT_QW
    akr = aqr + ATT_QW
    ak_ref[...] = (proj(ak0, ATT_KVW) * cos[:, 0:ATT_KVW] + proj(akr, ATT_KVW) * sin[:, 0:ATT_KVW]).astype(BF16)
    av_ref[...] = proj(ak0 + ATT_KVW, ATT_KVW).astype(BF16)


def _inproj(x_all, mod_l, w_ext, cos_tab, sin_tab, *, n_batch, seq):
    n_tok = x_all.shape[0]
    n_tiles = n_tok // TOK_TILE
    tiles_per_batch = seq // TOK_TILE
    n_lat_tiles = n_batch * tiles_per_batch

    def tab_map(i):
        return (jnp.where(i < n_lat_tiles, i % tiles_per_batch, tiles_per_batch + (i - n_lat_tiles)), 0)

    tok = lambda w: pl.BlockSpec((TOK_TILE, w), lambda i: (i, 0))
    full = lambda a: pl.BlockSpec(a.shape, lambda i: (0,) * a.ndim)
    out_w = [(RET_W, BF16), (RET_W, BF16), (RET_W, BF16), (RET_W, F32), (SGU_W, F32), (SGU_W, F32),
             (ATT_QW, BF16), (ATT_KVW, BF16), (ATT_KVW, BF16)]
    return pl.pallas_call(
        functools.partial(_inproj_kernel, tiles_per_batch=tiles_per_batch, n_batch=n_batch),
        grid=(n_tiles,),
        in_specs=[tok(D_MODEL), full(mod_l), full(w_ext),
                  pl.BlockSpec((TOK_TILE, RET_W), tab_map), pl.BlockSpec((TOK_TILE, RET_W), tab_map)],
        out_specs=[tok(w) for w, _ in out_w],
        out_shape=[jax.ShapeDtypeStruct((n_tok, w), dt) for w, dt in out_w],
        compiler_params=pltpu.CompilerParams(dimension_semantics=("arbitrary",), vmem_limit_bytes=VMEM_LIMIT),
        name="inproj",
    )(x_all, mod_l, w_ext, cos_tab, sin_tab)


def _ret_kernel(dec_ref, qf_ref, kf_ref, vf_ref, qb_ref, kb_ref, vb_ref, of_ref, ob_ref,
                sf_ref, sb_ref, intra_ref, qd_ref, kd_ref, cd_ref):
    b = pl.program_id(0)
    s = pl.program_id(1)
    C = CHUNK
    W = RET_W

    @pl.when((b == 0) & (s == 0))
    def _tables():
        dec = dec_ref[...]
        lg = jnp.minimum(dec, 0.0) - jnp.log(1.0 + jnp.exp(-jnp.abs(dec)))
        lane_head = lax.broadcasted_iota(jnp.int32, (1, W), 1) // HEAD_DIM
        ii = lax.broadcasted_iota(jnp.int32, (C, C), 0)
        jj = lax.broadcasted_iota(jnp.int32, (C, C), 1)
        ri = lax.broadcasted_iota(jnp.int32, (C, W), 0).astype(F32)
        rb = lax.broadcasted_iota(jnp.int32, (W, W), 0) // HEAD_DIM
        cb = lax.broadcasted_iota(jnp.int32, (W, W), 1) // HEAD_DIM
        for d in range(2):
            lgl = jnp.zeros((1, W), F32)
            for hh in range(RET_HEADS):
                lg_h = lg[d:d + 1, hh:hh + 1]
                lgl = lgl + jnp.where(lane_head == hh, lg_h, 0.0)
                rel = (ii - jj) if d == 0 else (jj - ii)
                m = jnp.exp(jnp.maximum(rel, 0).astype(F32) * lg_h)
                intra_ref[d, hh * C:(hh + 1) * C, :] = jnp.where(rel >= 0, m, 0.0)
            if d == 0:
                qd_ref[d] = jnp.exp((ri + 1.0) * lgl)
                kd_ref[d] = jnp.exp((C - 1.0 - ri) * lgl)
            else:
                qd_ref[d] = jnp.exp((C - ri) * lgl)
                kd_ref[d] = jnp.exp(ri * lgl)
            cd_ref[d] = jnp.where(rb == cb, jnp.exp(C * lgl), 0.0)

    @pl.when(s == 0)
    def _zero():
        sf_ref[...] = jnp.zeros_like(sf_ref)
        sb_ref[...] = jnp.zeros_like(sb_ref)

    hm = _head_mask_stack(RET_HEADS, C, W)
    rb = lax.broadcasted_iota(jnp.int32, (W, W), 0) // HEAD_DIM
    cb = lax.broadcasted_iota(jnp.int32, (W, W), 1) // HEAD_DIM
    bd = rb == cb
    zero_b = jnp.zeros((), BF16)

    def direction(d, q_ref, k_ref, v_ref, st_ref, o_ref):
        q = q_ref[...]
        k = k_ref[...]
        v = v_ref[...]
        qs = jnp.where(hm, jnp.concatenate([q] * RET_HEADS, axis=0), zero_b)
        att = (_dot_nt(qs, k) * intra_ref[d]).astype(BF16)
        att = jnp.concatenate([att[hh * C:(hh + 1) * C] for hh in range(RET_HEADS)], axis=1)
        vs = jnp.where(hm, jnp.concatenate([v] * RET_HEADS, axis=0), zero_b)
        st = st_ref[...]
        o = _dot(att, vs) + _dot(q, st.astype(BF16)) * qd_ref[d]
        o_ref[...] = o
        kdec = (k.astype(F32) * kd_ref[d]).astype(BF16)
        st_ref[...] = st * cd_ref[d] + jnp.where(bd, _dot_tn(kdec, v), 0.0)

    direction(0, qf_ref, kf_ref, vf_ref, sf_ref, of_ref)
    direction(1, qb_ref, kb_ref, vb_ref, sb_ref, ob_ref)


def _retention(decays, rq, rk, rv, *, n_batch, seq, ctx_len):
    n_tok = rq.shape[0]
    nl = seq // CHUNK
    nc = ctx_len // CHUNK
    ctx0 = n_batch * nl

    def fwd_map(b, s):
        return (jnp.where(s < nc, ctx0 + b * nc + s, b * nl + (s - nc)), 0)

    def bwd_map(b, s):
        return (jnp.where(s < nc, ctx0 + b * nc + (nc - 1 - s), b * nl + (nl - 1 - (s - nc))), 0)

    fspec = pl.BlockSpec((CHUNK, RET_W), fwd_map)
    bspec = pl.BlockSpec((CHUNK, RET_W), bwd_map)
    return pl.pallas_call(
        _ret_kernel,
        grid=(n_batch, nc + nl),
        in_specs=[pl.BlockSpec(decays.shape, lambda b, s: (0, 0)), fspec, fspec, fspec, bspec, bspec, bspec],
        out_specs=[fspec, bspec],
        out_shape=[jax.ShapeDtypeStruct((n_tok, RET_W), F32)] * 2,
        scratch_shapes=[
            pltpu.VMEM((RET_W, RET_W), F32), pltpu.VMEM((RET_W, RET_W), F32),
            pltpu.VMEM((2, RET_HEADS * CHUNK, CHUNK), F32),
            pltpu.VMEM((2, CHUNK, RET_W), F32), pltpu.VMEM((2, CHUNK, RET_W), F32),
            pltpu.VMEM((2, RET_W, RET_W), F32),
        ],
        compiler_params=pltpu.CompilerParams(dimension_semantics=("arbitrary", "arbitrary")),
        name="retention",
    )(decays, rq, rk, rv, rq, rk, rv)


def _attn_kernel(sink_ref, q_ref, kp_ref, kc_ref, kn_ref, vp_ref, vc_ref, vn_ref, kx_ref, vx_ref, o_ref,
                 *, n_lat_blocks, seq):
    n = pl.program_id(1)
    W = CHUNK
    qi = lax.broadcasted_iota(jnp.int32, (W, 3 * W), 0)
    kj = lax.broadcasted_iota(jnp.int32, (W, 3 * W), 1)
    rel = kj - W - qi
    kpos = n * W - W + kj
    mask = (jnp.abs(rel) <= W) & (kpos >= 0) & (kpos < seq) & (n < n_lat_blocks)
    keys = jnp.concatenate([kp_ref[...], kc_ref[...], kn_ref[...], kx_ref[...]], axis=0)
    vals = jnp.concatenate([vp_ref[...], vc_ref[...], vn_ref[...], vx_ref[...]], axis=0)
    lane_head = lax.broadcasted_iota(jnp.int32, (1, LANES), 1) // HEAD_DIM
    zero_b = jnp.zeros((), BF16)
    slabs = [q_ref[:, r * LANES:(r + 1) * LANES] for r in range(GQA_GROUP)]
    mask4 = jnp.concatenate([mask] * GQA_GROUP, axis=0)
    outs = [jnp.zeros((W, LANES), F32) for _ in range(GQA_GROUP)]
    for g in range(ATT_KV_HEADS):
        mg = lane_head == g
        qs = jnp.concatenate([jnp.where(mg, s, zero_b) for s in slabs], axis=0)
        sink = jnp.concatenate([jnp.full((W, 1), sink_ref[g * GQA_GROUP + r], F32) for r in range(GQA_GROUP)], axis=0)
        sc = _dot_nt(qs, keys)
        s_loc = jnp.where(mask4, sc[:, 0:3 * W], NEG_INF)
        s_ctx = sc[:, 3 * W:]
        m = jnp.maximum(jnp.maximum(jnp.max(s_loc, axis=-1, keepdims=True),
                                    jnp.max(s_ctx, axis=-1, keepdims=True)), sink)
        p_loc = jnp.exp(s_loc - m)
        p_ctx = jnp.exp(s_ctx - m)
        den = (jnp.sum(p_loc, axis=-1, keepdims=True) + jnp.sum(p_ctx, axis=-1, keepdims=True)
               + jnp.exp(sink - m))
        p = jnp.concatenate([p_loc, p_ctx], axis=1).astype(BF16)
        o = _dot(p, vals) * (1.0 / den)
        for r in range(GQA_GROUP):
            outs[r] = outs[r] + jnp.where(mg, o[r * W:(r + 1) * W], 0.0)
    for r in range(GQA_GROUP):
        o_ref[:, r * LANES:(r + 1) * LANES] = outs[r].astype(BF16)


def _attention(sink, aq, ak, av, *, n_batch, seq, ctx_len):
    n_tok = aq.shape[0]
    nl = seq // CHUNK
    nc = ctx_len // CHUNK
    ctx0 = n_batch * nl

    def q_map(b, n):
        return (jnp.where(n < nl, b * nl + n, ctx0 + b * nc + (n - nl)), 0)

    def k_map(off):
        def f(b, n):
            return (b * nl + jnp.clip(n + off, 0, nl - 1), 0)
        return f

    x_map = lambda b, n: (n_batch * seq // ctx_len + b, 0)
    kv = lambda off: pl.BlockSpec((CHUNK, ATT_KVW), k_map(off))
    xspec = pl.BlockSpec((ctx_len, ATT_KVW), x_map)
    return pl.pallas_call(
        functools.partial(_attn_kernel, n_lat_blocks=nl, seq=seq),
        grid=(n_batch, nl + nc),
        in_specs=[pl.BlockSpec(memory_space=pltpu.SMEM), pl.BlockSpec((CHUNK, ATT_QW), q_map),
                  kv(-1), kv(0), kv(1), kv(-1), kv(0), kv(1), xspec, xspec],
        out_specs=pl.BlockSpec((CHUNK, ATT_QW), q_map),
        out_shape=jax.ShapeDtypeStruct((n_tok, ATT_QW), BF16),
        compiler_params=pltpu.CompilerParams(dimension_semantics=("arbitrary", "arbitrary")),
        name="window_attn",
    )(sink, aq, ak, ak, ak, av, av, av, ak, av)


def _mixout_kernel(x_ref, of_ref, ob_ref, rg_ref, su_ref, sv_ref, ao_ref, mod_ref, wout_ref, wcat_ref, bs_ref,
                   lng_ref, lnb_ref, wq_ref, x1_ref, h2t_ref, qp_ref, *, tiles_per_batch, n_batch):
    i = pl.program_id(0)
    grp = jnp.minimum(i // tiles_per_batch, n_batch)
    avg = _group_avg_matrix(RET_W)
    a = _group_ln(of_ref[...] + ob_ref[...], avg) * _silu(rg_ref[...])
    u = _gelu(su_ref[...])
    vn = _group_ln(_gelu(sv_ref[...]), avg)
    hm = _head_mask_stack(SGU_GROUPS, CHUNK, SGU_W)
    zero_b = jnp.zeros((), BF16)
    parts = []
    for c in range(TOK_TILE // CHUNK):
        vc = vn[c * CHUNK:(c + 1) * CHUNK].astype(BF16)
        vs = jnp.where(hm, jnp.concatenate([vc] * SGU_GROUPS, axis=0), zero_b)
        mix = _dot(wcat_ref[...], vs) + bs_ref[...]
        parts.append(u[c * CHUNK:(c + 1) * CHUNK] * mix)
    bmix = jnp.concatenate(parts, axis=0)
    y = (_dot(a.astype(BF16), wout_ref[0:RET_W, :])
         + _dot(bmix.astype(BF16), wout_ref[RET_W:RET_W + SGU_W, :])
         + _dot(ao_ref[...], wout_ref[RET_W + SGU_W:, :]))
    gate1 = mod_ref[pl.ds(grp, 1), 2 * D_MODEL:3 * D_MODEL]
    x1 = _ln(DEEPNORM_ALPHA * x_ref[...] + gate1 * y) * lng_ref[...] + lnb_ref[...]
    x1_ref[...] = x1
    shift2 = mod_ref[pl.ds(grp, 1), 3 * D_MODEL:4 * D_MODEL]
    scale2 = mod_ref[pl.ds(grp, 1), 4 * D_MODEL:5 * D_MODEL]
    h2f = _ln(x1) * (1.0 + scale2) + shift2
    for k in range(TOK_TILE // H2T_TOK):
        h2t_ref[k] = h2f[k * H2T_TOK:(k + 1) * H2T_TOK].T.astype(BF16)
    qp_ref[...] = _dot(h2f.astype(BF16), wq_ref[...]).astype(BF16)


def _mixout(x_all, o_f, o_b, rg, su, sv, ao, mod_l, w_out, w_cat, b_tab, ln_g, ln_b, wq, *, n_batch, seq):
    n_tok = x_all.shape[0]
    tiles_per_batch = seq // TOK_TILE
    tok = lambda w: pl.BlockSpec((TOK_TILE, w), lambda i: (i, 0))
    full = lambda a: pl.BlockSpec(a.shape, lambda i: (0,) * a.ndim)
    n_q = wq.shape[1]
    return pl.pallas_call(
        functools.partial(_mixout_kernel, tiles_per_batch=tiles_per_batch, n_batch=n_batch),
        grid=(n_tok // TOK_TILE,),
        in_specs=[tok(D_MODEL), tok(RET_W), tok(RET_W), tok(RET_W), tok(SGU_W), tok(SGU_W), tok(ATT_QW),
                  full(mod_l), full(w_out), full(w_cat), full(b_tab), full(ln_g), full(ln_b), full(wq)],
        out_specs=[tok(D_MODEL), pl.BlockSpec((TOK_TILE // H2T_TOK, D_MODEL, H2T_TOK), lambda i: (i, 0, 0)),
                   tok(n_q)],
        out_shape=[jax.ShapeDtypeStruct((n_tok, D_MODEL), F32),
                   jax.ShapeDtypeStruct((n_tok // H2T_TOK, D_MODEL, H2T_TOK), BF16),
                   jax.ShapeDtypeStruct((n_tok, n_q), BF16)],
        compiler_params=pltpu.CompilerParams(dimension_semantics=("arbitrary",), vmem_limit_bytes=VMEM_LIMIT),
        name="mix_out",
    )(x_all, o_f, o_b, rg, su, sv, ao, mod_l, w_out, w_cat, b_tab, ln_g, ln_b, wq)


_CAND_SEGS = [(a, PEER_TOPK // (a + 1)) for a in range(1, SUBLANES)]


def _batcher_pairs(n):
    pairs = []
    p = 1
    while p < n:
        k = p
        while k >= 1:
            for j in range(k % p, n - k, 2 * k):
                for i in range(min(k, n - j - k)):
                    if (i + j) // (2 * p) == (i + j + k) // (2 * p):
                        pairs.append((i + j, i + j + k))
            k //= 2
        p *= 2
    return pairs


def _top16_sorted(slabs):
    n = len(slabs)
    x = list(slabs)
    for i, j in _batcher_pairs(n):
        x[i], x[j] = jnp.maximum(x[i], x[j]), jnp.minimum(x[i], x[j])
    shift = SUBLANES // 2
    while shift >= 1:
        y = [pltpu.roll(v, shift, 0) for v in x]
        x = [jnp.maximum(x[k], y[n - 1 - k]) for k in range(n)]
        d = n // 2
        while d >= 1:
            for i in range(n):
                if (i & d) == 0:
                    x[i], x[i + d] = jnp.maximum(x[i], x[i + d]), jnp.minimum(x[i], x[i + d])
            d //= 2
        shift //= 2
    return x


def _dup_bf16_words(x):
    hi = pltpu.bitcast(x.astype(BF16).astype(F32), jnp.uint32)
    return hi | (hi >> 16)


def _peer_kernel(h2t_ref, qp_ref, x1_ref, mod_ref, k1_ref, k2_ref, u_ref, vt_ref, lng_ref, lnb_ref, out_ref,
                 s_sc, n_sc, a_sc, rk_sc, b_sc, row_sc, act0_sc, act1_sc, hs0_sc, hs1_sc, acc_sc,
                 *, tiles_per_batch, n_batch, n_blocks):
    t = pl.program_id(0)
    e = pl.program_id(1)
    NG = PEER_TOK // LANES
    K = PEER_TOPK
    half = PEER_QDIM // 2

    @pl.when(e == 0)
    def _prologue():
        acc_sc[...] = jnp.zeros_like(acc_sc)
        for h in range(PEER_HEADS):
            for p, kref in ((0, k1_ref), (1, k2_ref)):
                c0 = h * PEER_QDIM + p * half
                sT = _dot_nt(kref[...], qp_ref[:, c0:c0 + half])
                for g in range(NG):
                    s_sc[2 * h + p, g] = sT[:, g * LANES:(g + 1) * LANES]

        row8 = lax.broadcasted_iota(jnp.int32, (SUBLANES, LANES), 0)

        def stats(it, carry):
            h = it // NG
            g = it % NG
            s1 = s_sc[2 * h, g]
            s2 = s_sc[2 * h + 1, g]
            slabs1 = [s1[SUBLANES * k:SUBLANES * (k + 1)] for k in range(K)]
            slabs2 = [s2[SUBLANES * k:SUBLANES * (k + 1)] for k in range(K)]
            t1 = _top16_sorted(slabs1)
            t2 = _top16_sorted(slabs2)
            v1 = jnp.concatenate([x[0:1] for x in t1], axis=0)
            v2 = jnp.concatenate([x[0:1] for x in t2], axis=0)
            segs = [v1[0:1] + v2]
            for a, n_a in _CAND_SEGS:
                segs.append(jnp.where(row8 < n_a, v1[a:a + 1] + v2[0:SUBLANES], NEG_INF))
            segs.append(v1[SUBLANES:K] + v2[0:1])
            cand = jnp.concatenate(segs, axis=0)
            cur = cand
            tau = None
            for r in range(K):
                tau = jnp.max(cur, axis=0, keepdims=True)
                cur = jnp.where(cur >= tau, NEG_INF, cur)
            cmax = v1[0:1] + v2[0:1]
            z = jnp.sum(jnp.where(cand >= tau, jnp.exp(cand - cmax), 0.0), axis=0, keepdims=True)
            tau8 = jnp.broadcast_to(tau, (SUBLANES, LANES))
            rz8 = jnp.broadcast_to(1.0 / z, (SUBLANES, LANES))
            n_parts, a_parts, rk_parts, b_parts = [], [], [], []
            for k in range(K):
                n = jnp.zeros((SUBLANES, LANES), F32)
                rk = jnp.zeros((SUBLANES, LANES), F32)
                for bb in range(K):
                    n = jnp.where((slabs1[k] + t2[bb]) >= tau8, float(bb + 1), n)
                    rk = jnp.where(t2[bb] > slabs2[k], float(bb + 1), rk)
                n_parts.append(n)
                rk_parts.append(rk)
                a_parts.append(jnp.exp(slabs1[k] - t1[0]) * rz8)
                b_parts.append(jnp.exp(slabs2[k] - t2[0]))
            n_sc[h, g] = _dup_bf16_words(jnp.concatenate(n_parts, axis=0))
            a_sc[h, g] = _dup_bf16_words(jnp.concatenate(a_parts, axis=0))
            rk_sc[h, g] = jnp.concatenate(rk_parts, axis=0).astype(BF16)
            b_sc[h, g] = jnp.concatenate(b_parts, axis=0).astype(BF16)
            return carry

        lax.fori_loop(0, PEER_HEADS * NG, stats, 0)

    rows_per_blk = PEER_EB // PEER_NKEYS
    blks_per_tile = max(1, SUBLANES // rows_per_blk)
    tiles_per_blk = max(1, rows_per_blk // SUBLANES)
    kc_rows = PACKED_ROWS
    zero_b = jnp.zeros((), BF16)
    gate_blk = jnp.clip(e - 1, 0, n_blocks - 1)
    tile0 = pl.multiple_of((gate_blk // blks_per_tile) * (SUBLANES * tiles_per_blk), SUBLANES)

    def step(act_w, act_r, hs_w, hs_r, row_off, first=True, gates=True, second=True):
        def half(c, carry):
            if first:
                a_new = _gelu(_dot(u_ref[0], h2t_ref[c])).astype(BF16)
                act_w[2 * c] = a_new[:, 0:LANES]
                act_w[2 * c + 1] = a_new[:, LANES:2 * LANES]
            if second:
                hs_prev = jnp.concatenate([hs_r[2 * c], hs_r[2 * c + 1]], axis=1)
                acc_sc[c] += _dot(vt_ref[0, 0], hs_prev)
            if gates:
                gate_group(2 * c)
                gate_group(2 * c + 1)
            return carry

        def gate_group(g):
            for h in range(PEER_HEADS):
                for tt in range(tiles_per_blk):
                    n_tile = n_sc[h, g, pl.ds(tile0 + SUBLANES * tt, SUBLANES), :]
                    a_tile = a_sc[h, g, pl.ds(tile0 + SUBLANES * tt, SUBLANES), :]
                    for rr in range(min(SUBLANES, rows_per_blk)):
                        r, k = SUBLANES * tt + rr, row_off + rr
                        row_sc[0, h, r] = pltpu.bitcast(jnp.broadcast_to(n_tile[k:k + 1, :], (SUBLANES, LANES)), BF16)
                        row_sc[1, h, r] = pltpu.bitcast(jnp.broadcast_to(a_tile[k:k + 1, :], (SUBLANES, LANES)), BF16)
            for kc in range(PEER_NKEYS // kc_rows):
                keys = slice(kc * kc_rows, (kc + 1) * kc_rows)
                for r0 in range(0, rows_per_blk, GATE_ROWS):
                    rr = range(r0, min(r0 + GATE_ROWS, rows_per_blk))
                    gates = {r: jnp.zeros((kc_rows, LANES), BF16) for r in rr}
                    for h in range(PEER_HEADS):
                        rk = rk_sc[h, g, keys, :]
                        bb = b_sc[h, g, keys, :]
                        for r in rr:
                            n_full = jnp.concatenate([row_sc[0, h, r]] * (kc_rows // PACKED_ROWS), axis=0)
                            a_full = jnp.concatenate([row_sc[1, h, r]] * (kc_rows // PACKED_ROWS), axis=0)
                            gates[r] = gates[r] + jnp.where(rk < n_full, a_full * bb, zero_b)
                    for r in rr:
                        rows = slice(r * PEER_NKEYS + kc * kc_rows, r * PEER_NKEYS + (kc + 1) * kc_rows)
                        hs_w[g, rows, :] = gates[r] * act_r[g, rows, :]

        lax.fori_loop(0, NG // 2, half, 0)

    even = (act0_sc, act1_sc, hs0_sc, hs1_sc, (1 % blks_per_tile) * rows_per_blk)
    odd = (act1_sc, act0_sc, hs1_sc, hs0_sc, 0)
    steady = (e >= 2) & (e < n_blocks)

    @pl.when(e == 0)
    def _fill0():
        step(*even, gates=False, second=False)

    @pl.when(e == 1)
    def _fill1():
        step(*odd, second=False)

    @pl.when(steady & (e % 2 == 0))
    def _even():
        step(*even)

    @pl.when(steady & (e % 2 == 1))
    def _odd():
        step(*odd)

    @pl.when(e == n_blocks)
    def _drain0():
        step(*even, first=False)

    @pl.when(e == n_blocks + 1)
    def _drain1():
        step(*odd, first=False, gates=False)

    @pl.when(e == n_blocks + 1)
    def _epilogue():
        grp = jnp.minimum(t // tiles_per_batch, n_batch)
        gate2 = mod_ref[pl.ds(grp, 1), 5 * D_MODEL:6 * D_MODEL]
        f = jnp.concatenate([acc_sc[c].T for c in range(NG // 2)], axis=0)
        out_ref[...] = _ln(DEEPNORM_ALPHA * x1_ref[...] + gate2 * f) * lng_ref[...] + lnb_ref[...]


def _peer(h2, qp, x1, mod_l, k1, k2, u_bf, vt3, ln_g, ln_b, *, layer, n_batch, seq, n_tiles):
    tiles_per_batch = seq // PEER_TOK
    n_e = PEER_N // PEER_EB
    assert n_e % 2 == 0 and n_e >= 4
    NG = PEER_TOK // LANES
    tok = lambda w: pl.BlockSpec((PEER_TOK, w), lambda t, e: (t, 0))
    full = lambda a: pl.BlockSpec(a.shape, lambda t, e: (0,) * a.ndim)
    stat = lambda n, dt: pltpu.VMEM((n, NG, PEER_NKEYS, LANES), dt)
    return pl.pallas_call(
        functools.partial(_peer_kernel, tiles_per_batch=tiles_per_batch, n_batch=n_batch, n_blocks=n_e),
        grid=(n_tiles, n_e + 2),
        in_specs=[pl.BlockSpec((PEER_TOK // H2T_TOK, D_MODEL, H2T_TOK), lambda t, e: (t, 0, 0)),
                  tok(qp.shape[1]), tok(D_MODEL), full(mod_l), full(k1), full(k2),
                  pl.BlockSpec((1, PEER_EB, D_MODEL), lambda t, e: (layer, jnp.minimum(e, n_e - 1), 0)),
                  pl.BlockSpec((1, 1, D_MODEL, PEER_EB), lambda t, e: (layer, jnp.clip(e - 2, 0, n_e - 1), 0, 0)),
                  full(ln_g), full(ln_b)],
        out_specs=tok(D_MODEL),
        out_shape=jax.ShapeDtypeStruct((n_tiles * PEER_TOK, D_MODEL), F32),
        scratch_shapes=[stat(2 * PEER_HEADS, F32),
                        stat(PEER_HEADS, jnp.uint32), stat(PEER_HEADS, jnp.uint32),
                        stat(PEER_HEADS, BF16), stat(PEER_HEADS, BF16),
                        pltpu.VMEM((2, PEER_HEADS, PEER_EB // PEER_NKEYS, PACKED_ROWS, LANES), BF16),
                        pltpu.VMEM((NG, PEER_EB, LANES), BF16), pltpu.VMEM((NG, PEER_EB, LANES), BF16),
                        pltpu.VMEM((NG, PEER_EB, LANES), BF16), pltpu.VMEM((NG, PEER_EB, LANES), BF16),
                        pltpu.VMEM((NG // 2, D_MODEL, 2 * LANES), F32)],
        compiler_params=pltpu.CompilerParams(dimension_semantics=("arbitrary", "arbitrary"),
                                             vmem_limit_bytes=VMEM_LIMIT),
        name="peer",
    )(h2, qp, x1, mod_l, k1, k2, u_bf, vt3, ln_g, ln_b)


def _rope_tables(seq, ctx_len):
    rows = seq // GRID_W
    row_id = jnp.repeat(jnp.arange(rows), GRID_W).astype(F32)
    col_id = jnp.tile(jnp.arange(GRID_W), rows).astype(F32)
    inv = jnp.power(ROPE_BASE, -jnp.arange(ROPE_PAIRS, dtype=F32) / ROPE_PAIRS)
    ang_r = row_id[:, None] * inv
    ang_c = col_id[:, None] * inv
    cos64 = jnp.concatenate([jnp.cos(ang_r)] * 2 + [jnp.cos(ang_c)] * 2, axis=-1)
    sin64 = jnp.concatenate([jnp.sin(ang_r)] * 2 + [jnp.sin(ang_c)] * 2, axis=-1)
    n_rep = RET_W // HEAD_DIM
    cos_tab = jnp.concatenate([jnp.tile(cos64, (1, n_rep)), jnp.ones((ctx_len, RET_W), F32)], axis=0)
    sin_tab = jnp.concatenate([jnp.tile(sin64, (1, n_rep)), jnp.zeros((ctx_len, RET_W), F32)], axis=0)
    return cos_tab, sin_tab


def _rot_partner(width):
    l = np.arange(width)
    lo = (l % (2 * ROPE_PAIRS)) < ROPE_PAIRS
    partner = np.where(lo, l + ROPE_PAIRS, l - ROPE_PAIRS)
    sign = np.where(lo, -1.0, 1.0).astype(np.float32)
    return partner, sign


def _slab_perm():
    new = np.arange(ATT_QW)
    r, rem = new // LANES, new % LANES
    g, d = rem // HEAD_DIM, rem % HEAD_DIM
    return (g * GQA_GROUP + r) * HEAD_DIM + d


def _prep_w_in(w_in):
    o_rq, o_rk, o_aq, o_ak = 0, RET_W, 4 * RET_W + 2 * SGU_W, 4 * RET_W + 2 * SGU_W + ATT_QW
    slab = _slab_perm()
    aq = w_in[..., o_aq:o_aq + ATT_QW]

    def partner(block):
        p, sg = _rot_partner(block.shape[-1])
        return block[..., p] * sg

    base = jnp.concatenate([w_in[..., :o_aq], aq[..., slab], w_in[..., o_ak:]], axis=-1)
    rot = jnp.concatenate([partner(w_in[..., o_rq:o_rq + RET_W]), partner(w_in[..., o_rk:o_rk + RET_W]),
                           partner(aq)[..., slab], partner(w_in[..., o_ak:o_ak + ATT_KVW])], axis=-1)
    return jnp.concatenate([base, rot], axis=-1).astype(BF16)


def kernel(x, c, ctx, c_ctx, w_mod, b_mod, w_in, w_out, ret_decay_fwd, ret_decay_bwd, sgu_w, sgu_b, attn_sink,
           ln_mix_g, ln_mix_b, peer_wq, peer_k1, peer_k2, peer_u, peer_v, ln_ffn_g, ln_ffn_b):
    n_batch, seq, _ = x.shape
    ctx_len = ctx.shape[1]
    depth = w_in.shape[0]
    assert seq % PEER_TOK == 0 and seq % TOK_TILE == 0 and ctx_len % CHUNK == 0
    assert (n_batch * ctx_len) % PEER_TOK == 0 and (n_batch * ctx_len) % TOK_TILE == 0
    assert (n_batch * seq) % ctx_len == 0
    assert n_batch + 1 <= SUBLANES
    n_lat = n_batch * seq
    kw = dict(n_batch=n_batch, seq=seq)

    x_all = jnp.concatenate([x.reshape(n_lat, D_MODEL), ctx.reshape(n_batch * ctx_len, D_MODEL)], axis=0)
    cond = jnp.concatenate([c, c_ctx[None, :], jnp.zeros((SUBLANES - n_batch - 1, D_MODEL), F32)], axis=0)
    mods = _modulations(cond, w_mod, b_mod)

    cos_tab, sin_tab = _rope_tables(seq, n_batch * ctx_len)
    w_ext = _prep_w_in(w_in)
    slab = _slab_perm()
    w_out_p = jnp.concatenate([w_out[:, :RET_W + SGU_W], w_out[:, RET_W + SGU_W:][:, slab]], axis=1).astype(BF16)
    w_cat = jnp.transpose(sgu_w, (0, 2, 1, 3)).reshape(depth, CHUNK, SGU_GROUPS * CHUNK).astype(BF16)
    b_tab = jnp.repeat(jnp.transpose(sgu_b, (0, 2, 1)), HEAD_DIM, axis=2)
    decays = jnp.stack([ret_decay_fwd, ret_decay_bwd], axis=1)
    wq = peer_wq.astype(BF16)
    k1 = peer_k1.astype(BF16)
    k2 = peer_k2.astype(BF16)
    u_bf = peer_u.astype(BF16)
    vt3 = jnp.transpose(peer_v.reshape(depth, PEER_N // PEER_EB, PEER_EB, D_MODEL), (0, 1, 3, 2)).astype(BF16)
    row = lambda a: a.reshape(depth, 1, D_MODEL)
    lmg, lmb, lfg, lfb = row(ln_mix_g), row(ln_mix_b), row(ln_ffn_g), row(ln_ffn_b)

    for l in range(depth):
        last = l == depth - 1
        rq, rk, rv, rg, su, sv, aq, ak, av = _inproj(x_all, mods[l], w_ext[l], cos_tab, sin_tab, **kw)
        o_f, o_b = _retention(decays[l], rq, rk, rv, ctx_len=ctx_len, **kw)
        ao = _attention(attn_sink[l], aq, ak, av, ctx_len=ctx_len, **kw)
        x1, h2, qp = _mixout(x_all, o_f, o_b, rg, su, sv, ao, mods[l], w_out_p[l], w_cat[l], b_tab[l],
                             lmg[l], lmb[l], wq[l], **kw)
        n_tiles = (n_lat if last else x_all.shape[0]) // PEER_TOK
        x_all = _peer(h2, qp, x1, mods[l], k1[l], k2[l], u_bf, vt3, lfg[l], lfb[l], layer=l, n_tiles=n_tiles, **kw)
    return x_all[:n_lat].reshape(n_batch, seq, D_MODEL)
```

```python
import functools

import numpy as np
import jax
import jax.numpy as jnp
from jax import lax
from jax.experimental import pallas as pl
from jax.experimental.pallas import tpu as pltpu

F32 = jnp.float32
BF16 = jnp.bfloat16

D_MODEL = 1024
DEPTH = 4
GRID_W = 64
HEAD_DIM = 64
ROPE_PAIRS = HEAD_DIM // 4
ROPE_BASE = 10000.0
RET_HEADS = 4
CHUNK = 128
RET_W = RET_HEADS * HEAD_DIM
RET_SCALE = HEAD_DIM ** -0.5
SGU_GROUPS = 4
SGU_W = SGU_GROUPS * HEAD_DIM
ATT_Q_HEADS = 8
ATT_KV_HEADS = 2
GQA_GROUP = ATT_Q_HEADS // ATT_KV_HEADS
ATT_QW = ATT_Q_HEADS * HEAD_DIM
ATT_KVW = ATT_KV_HEADS * HEAD_DIM
ATT_SCALE = HEAD_DIM ** -0.5
D_IN = 4 * RET_W + 2 * SGU_W + ATT_QW + 2 * ATT_KVW
PEER_HEADS = 8
PEER_NKEYS = 128
PEER_N = PEER_NKEYS * PEER_NKEYS
PEER_QDIM = 256
PEER_TOPK = 16
LN_EPS = 1e-5
DEEPNORM_ALPHA = (2 * DEPTH) ** 0.25

LANES = 128
SUBLANES = 8
PACKED_ROWS = 2 * SUBLANES
TOK_TILE = 512
H2T_TOK = 2 * LANES
PEER_TOK = 512
PEER_EB = 2048
GATE_ROWS = 4
VMEM_LIMIT = 56 * 1024 * 1024

NEG_INF = float("-inf")


def _ln(x):
    mu = jnp.mean(x, axis=-1, keepdims=True)
    xc = x - mu
    var = jnp.mean(xc * xc, axis=-1, keepdims=True)
    return xc * lax.rsqrt(var + LN_EPS)


def _gelu(x):
    h = 0.5 * x
    return h + h * jnp.tanh(x * (0.7978845608028654 + (0.7978845608028654 * 0.044715) * (x * x)))


def _silu(x):
    return x * (1.0 / (1.0 + jnp.exp(-x)))


def _dot(a, b):
    return jnp.dot(a, b, preferred_element_type=F32)


def _dot_nt(a, b):
    return lax.dot_general(a, b, (((1,), (1,)), ((), ())), preferred_element_type=F32)


def _dot_tn(a, b):
    return lax.dot_general(a, b, (((0,), (0,)), ((), ())), preferred_element_type=F32)


def _group_mean(z, avg):
    hi = z.astype(BF16)
    lo = (z - hi.astype(F32)).astype(BF16)
    return _dot(hi, avg) + _dot(lo, avg)


def _group_ln(x, avg):
    mu = _group_mean(x, avg)
    xc = x - mu
    var = _group_mean(xc * xc, avg)
    return xc * lax.rsqrt(var + LN_EPS)


def _group_avg_matrix(width):
    r = lax.broadcasted_iota(jnp.int32, (width, width), 0) // HEAD_DIM
    c = lax.broadcasted_iota(jnp.int32, (width, width), 1) // HEAD_DIM
    return jnp.where(r == c, 1.0 / HEAD_DIM, 0.0).astype(BF16)


def _head_mask_stack(n_heads, rows, width):
    r = lax.broadcasted_iota(jnp.int32, (n_heads * rows, width), 0) // rows
    c = lax.broadcasted_iota(jnp.int32, (n_heads * rows, width), 1) // HEAD_DIM
    return r == c


def _mod_kernel(c_ref, w_ref, b_ref, o_ref):
    s = _silu(c_ref[...])
    hi = s.astype(BF16)
    lo = (s - hi.astype(F32)).astype(BF16)
    w = w_ref[0]
    whi = w.astype(BF16)
    wlo = (w - whi.astype(F32)).astype(BF16)
    o_ref[0] = _dot(hi, whi) + _dot(lo, whi) + _dot(hi, wlo) + b_ref[0]


def _modulations(cond_rows, w_mod, b_mod):
    depth = w_mod.shape[0]
    n_rows = cond_rows.shape[0]
    col = 1024
    n_col = w_mod.shape[2] // col
    return pl.pallas_call(
        _mod_kernel,
        grid=(depth, n_col),
        in_specs=[
            pl.BlockSpec((n_rows, D_MODEL), lambda l, j: (0, 0)),
            pl.BlockSpec((1, D_MODEL, col), lambda l, j: (l, 0, j)),
            pl.BlockSpec((1, 1, col), lambda l, j: (l, 0, j)),
        ],
        out_specs=pl.BlockSpec((1, n_rows, col), lambda l, j: (l, 0, j)),
        out_shape=jax.ShapeDtypeStruct((depth, n_rows, w_mod.shape[2]), F32),
        name="adaln_mod",
    )(cond_rows, w_mod, b_mod.reshape(depth, 1, -1))


def _inproj_kernel(x_ref, mod_ref, w_ref, cos_ref, sin_ref,
                   rq_ref, rk_ref, rv_ref, rg_ref, su_ref, sv_ref, aq_ref, ak_ref, av_ref,
                   *, tiles_per_batch, n_batch):
    i = pl.program_id(0)
    grp = jnp.minimum(i // tiles_per_batch, n_batch)
    shift = mod_ref[pl.ds(grp, 1), 0:D_MODEL]
    scale = mod_ref[pl.ds(grp, 1), D_MODEL:2 * D_MODEL]
    h = (_ln(x_ref[...]) * (1.0 + scale) + shift).astype(BF16)
    cos = cos_ref[...]
    sin = sin_ref[...]

    def proj(c0, w):
        return _dot(h, w_ref[:, c0:c0 + w])

    rot0 = D_IN
    rq_ref[...] = (proj(0, RET_W) * cos + proj(rot0, RET_W) * sin).astype(BF16)
    rk_ref[...] = ((proj(RET_W, RET_W) * cos + proj(rot0 + RET_W, RET_W) * sin) * RET_SCALE).astype(BF16)
    rv_ref[...] = proj(2 * RET_W, RET_W).astype(BF16)
    rg_ref[...] = proj(3 * RET_W, RET_W)
    su_ref[...] = proj(4 * RET_W, SGU_W)
    sv_ref[...] = proj(4 * RET_W + SGU_W, SGU_W)
    aq0 = 4 * RET_W + 2 * SGU_W
    aqr = rot0 + 2 * RET_W
    for half in range(ATT_QW // RET_W):
        o = half * RET_W
        aq_ref[:, o:o + RET_W] = ((proj(aq0 + o, RET_W) * cos + proj(aqr + o, RET_W) * sin) * ATT_SCALE).astype(BF16)
    ak0 = aq0 + ATT_QW
    akr = aqr + ATT_QW
    ak_ref[...] = (proj(ak0, ATT_KVW) * cos[:, 0:ATT_KVW] + proj(akr, ATT_KVW) * sin[:, 0:ATT_KVW]).astype(BF16)
    av_ref[...] = proj(ak0 + ATT_KVW, ATT_KVW).astype(BF16)


def _inproj(x_all, mod_l, w_ext, cos_tab, sin_tab, *, n_batch, seq):
    n_tok = x_all.shape[0]
    n_tiles = n_tok // TOK_TILE
    tiles_per_batch = seq // TOK_TILE
    n_lat_tiles = n_batch * tiles_per_batch

    def tab_map(i):
        return (jnp.where(i < n_lat_tiles, i % tiles_per_batch, tiles_per_batch + (i - n_lat_tiles)), 0)

    tok = lambda w: pl.BlockSpec((TOK_TILE, w), lambda i: (i, 0))
    full = lambda a: pl.BlockSpec(a.shape, lambda i: (0,) * a.ndim)
    out_w = [(RET_W, BF16), (RET_W, BF16), (RET_W, BF16), (RET_W, F32), (SGU_W, F32), (SGU_W, F32),
             (ATT_QW, BF16), (ATT_KVW, BF16), (ATT_KVW, BF16)]
    return pl.pallas_call(
        functools.partial(_inproj_kernel, tiles_per_batch=tiles_per_batch, n_batch=n_batch),
        grid=(n_tiles,),
        in_specs=[tok(D_MODEL), full(mod_l), full(w_ext),
                  pl.BlockSpec((TOK_TILE, RET_W), tab_map), pl.BlockSpec((TOK_TILE, RET_W), tab_map)],
        out_specs=[tok(w) for w, _ in out_w],
        out_shape=[jax.ShapeDtypeStruct((n_tok, w), dt) for w, dt in out_w],
        compiler_params=pltpu.CompilerParams(dimension_semantics=("arbitrary",), vmem_limit_bytes=VMEM_LIMIT),
        name="inproj",
    )(x_all, mod_l, w_ext, cos_tab, sin_tab)


def _ret_kernel(dec_ref, qf_ref, kf_ref, vf_ref, qb_ref, kb_ref, vb_ref, of_ref, ob_ref,
                sf_ref, sb_ref, intra_ref, qd_ref, kd_ref, cd_ref):
    b = pl.program_id(0)
    s = pl.program_id(1)
    C = CHUNK
    W = RET_W

    @pl.when((b == 0) & (s == 0))
    def _tables():
        dec = dec_ref[...]
        lg = jnp.minimum(dec, 0.0) - jnp.log(1.0 + jnp.exp(-jnp.abs(dec)))
        lane_head = lax.broadcasted_iota(jnp.int32, (1, W), 1) // HEAD_DIM
        ii = lax.broadcasted_iota(jnp.int32, (C, C), 0)
        jj = lax.broadcasted_iota(jnp.int32, (C, C), 1)
        ri = lax.broadcasted_iota(jnp.int32, (C, W), 0).astype(F32)
        rb = lax.broadcasted_iota(jnp.int32, (W, W), 0) // HEAD_DIM
        cb = lax.broadcasted_iota(jnp.int32, (W, W), 1) // HEAD_DIM
        for d in range(2):
            lgl = jnp.zeros((1, W), F32)
            for hh in range(RET_HEADS):
                lg_h = lg[d:d + 1, hh:hh + 1]
                lgl = lgl + jnp.where(lane_head == hh, lg_h, 0.0)
                rel = (ii - jj) if d == 0 else (jj - ii)
                m = jnp.exp(jnp.maximum(rel, 0).astype(F32) * lg_h)
                intra_ref[d, hh * C:(hh + 1) * C, :] = jnp.where(rel >= 0, m, 0.0)
            if d == 0:
                qd_ref[d] = jnp.exp((ri + 1.0) * lgl)
                kd_ref[d] = jnp.exp((C - 1.0 - ri) * lgl)
            else:
                qd_ref[d] = jnp.exp((C - ri) * lgl)
                kd_ref[d] = jnp.exp(ri * lgl)
            cd_ref[d] = jnp.where(rb == cb, jnp.exp(C * lgl), 0.0)

    @pl.when(s == 0)
    def _zero():
        sf_ref[...] = jnp.zeros_like(sf_ref)
        sb_ref[...] = jnp.zeros_like(sb_ref)

    hm = _head_mask_stack(RET_HEADS, C, W)
    rb = lax.broadcasted_iota(jnp.int32, (W, W), 0) // HEAD_DIM
    cb = lax.broadcasted_iota(jnp.int32, (W, W), 1) // HEAD_DIM
    bd = rb == cb
    zero_b = jnp.zeros((), BF16)

    def direction(d, q_ref, k_ref, v_ref, st_ref, o_ref):
        q = q_ref[...]
        k = k_ref[...]
        v = v_ref[...]
        qs = jnp.where(hm, jnp.concatenate([q] * RET_HEADS, axis=0), zero_b)
        att = (_dot_nt(qs, k) * intra_ref[d]).astype(BF16)
        att = jnp.concatenate([att[hh * C:(hh + 1) * C] for hh in range(RET_HEADS)], axis=1)
        vs = jnp.where(hm, jnp.concatenate([v] * RET_HEADS, axis=0), zero_b)
        st = st_ref[...]
        o = _dot(att, vs) + _dot(q, st.astype(BF16)) * qd_ref[d]
        o_ref[...] = o
        kdec = (k.astype(F32) * kd_ref[d]).astype(BF16)
        st_ref[...] = st * cd_ref[d] + jnp.where(bd, _dot_tn(kdec, v), 0.0)

    direction(0, qf_ref, kf_ref, vf_ref, sf_ref, of_ref)
    direction(1, qb_ref, kb_ref, vb_ref, sb_ref, ob_ref)


def _retention(decays, rq, rk, rv, *, n_batch, seq, ctx_len):
    n_tok = rq.shape[0]
    nl = seq // CHUNK
    nc = ctx_len // CHUNK
    ctx0 = n_batch * nl

    def fwd_map(b, s):
        return (jnp.where(s < nc, ctx0 + b * nc + s, b * nl + (s - nc)), 0)

    def bwd_map(b, s):
        return (jnp.where(s < nc, ctx0 + b * nc + (nc - 1 - s), b * nl + (nl - 1 - (s - nc))), 0)

    fspec = pl.BlockSpec((CHUNK, RET_W), fwd_map)
    bspec = pl.BlockSpec((CHUNK, RET_W), bwd_map)
    return pl.pallas_call(
        _ret_kernel,
        grid=(n_batch, nc + nl),
        in_specs=[pl.BlockSpec(decays.shape, lambda b, s: (0, 0)), fspec, fspec, fspec, bspec, bspec, bspec],
        out_specs=[fspec, bspec],
        out_shape=[jax.ShapeDtypeStruct((n_tok, RET_W), F32)] * 2,
        scratch_shapes=[
            pltpu.VMEM((RET_W, RET_W), F32), pltpu.VMEM((RET_W, RET_W), F32),
            pltpu.VMEM((2, RET_HEADS * CHUNK, CHUNK), F32),
            pltpu.VMEM((2, CHUNK, RET_W), F32), pltpu.VMEM((2, CHUNK, RET_W), F32),
            pltpu.VMEM((2, RET_W, RET_W), F32),
        ],
        compiler_params=pltpu.CompilerParams(dimension_semantics=("arbitrary", "arbitrary")),
        name="retention",
    )(decays, rq, rk, rv, rq, rk, rv)


def _attn_kernel(sink_ref, q_ref, kp_ref, kc_ref, kn_ref, vp_ref, vc_ref, vn_ref, kx_ref, vx_ref, o_ref,
                 *, n_lat_blocks, seq):
    n = pl.program_id(1)
    W = CHUNK
    qi = lax.broadcasted_iota(jnp.int32, (W, 3 * W), 0)
    kj = lax.broadcasted_iota(jnp.int32, (W, 3 * W), 1)
    rel = kj - W - qi
    kpos = n * W - W + kj
    mask = (jnp.abs(rel) <= W) & (kpos >= 0) & (kpos < seq) & (n < n_lat_blocks)
    keys = jnp.concatenate([kp_ref[...], kc_ref[...], kn_ref[...], kx_ref[...]], axis=0)
    vals = jnp.concatenate([vp_ref[...], vc_ref[...], vn_ref[...], vx_ref[...]], axis=0)
    lane_head = lax.broadcasted_iota(jnp.int32, (1, LANES), 1) // HEAD_DIM
    zero_b = jnp.zeros((), BF16)
    slabs = [q_ref[:, r * LANES:(r + 1) * LANES] for r in range(GQA_GROUP)]
    mask4 = jnp.concatenate([mask] * GQA_GROUP, axis=0)
    outs = [jnp.zeros((W, LANES), F32) for _ in range(GQA_GROUP)]
    for g in range(ATT_KV_HEADS):
        mg = lane_head == g
        qs = jnp.concatenate([jnp.where(mg, s, zero_b) for s in slabs], axis=0)
        sink = jnp.concatenate([jnp.full((W, 1), sink_ref[g * GQA_GROUP + r], F32) for r in range(GQA_GROUP)], axis=0)
        sc = _dot_nt(qs, keys)
        s_loc = jnp.where(mask4, sc[:, 0:3 * W], NEG_INF)
        s_ctx = sc[:, 3 * W:]
        m = jnp.maximum(jnp.maximum(jnp.max(s_loc, axis=-1, keepdims=True),
                                    jnp.max(s_ctx, axis=-1, keepdims=True)), sink)
        p_loc = jnp.exp(s_loc - m)
        p_ctx = jnp.exp(s_ctx - m)
        den = (jnp.sum(p_loc, axis=-1, keepdims=True) + jnp.sum(p_ctx, axis=-1, keepdims=True)
               + jnp.exp(sink - m))
        p = jnp.concatenate([p_loc, p_ctx], axis=1).astype(BF16)
        o = _dot(p, vals) * (1.0 / den)
        for r in range(GQA_GROUP):
            outs[r] = outs[r] + jnp.where(mg, o[r * W:(r + 1) * W], 0.0)
    for r in range(GQA_GROUP):
        o_ref[:, r * LANES:(r + 1) * LANES] = outs[r].astype(BF16)


def _attention(sink, aq, ak, av, *, n_batch, seq, ctx_len):
    n_tok = aq.shape[0]
    nl = seq // CHUNK
    nc = ctx_len // CHUNK
    ctx0 = n_batch * nl

    def q_map(b, n):
        return (jnp.where(n < nl, b * nl + n, ctx0 + b * nc + (n - nl)), 0)

    def k_map(off):
        def f(b, n):
            return (b * nl + jnp.clip(n + off, 0, nl - 1), 0)
        return f

    x_map = lambda b, n: (n_batch * seq // ctx_len + b, 0)
    kv = lambda off: pl.BlockSpec((CHUNK, ATT_KVW), k_map(off))
    xspec = pl.BlockSpec((ctx_len, ATT_KVW), x_map)
    return pl.pallas_call(
        functools.partial(_attn_kernel, n_lat_blocks=nl, seq=seq),
        grid=(n_batch, nl + nc),
        in_specs=[pl.BlockSpec(memory_space=pltpu.SMEM), pl.BlockSpec((CHUNK, ATT_QW), q_map),
                  kv(-1), kv(0), kv(1), kv(-1), kv(0), kv(1), xspec, xspec],
        out_specs=pl.BlockSpec((CHUNK, ATT_QW), q_map),
        out_shape=jax.ShapeDtypeStruct((n_tok, ATT_QW), BF16),
        compiler_params=pltpu.CompilerParams(dimension_semantics=("arbitrary", "arbitrary")),
        name="window_attn",
    )(sink, aq, ak, ak, ak, av, av, av, ak, av)


def _mixout_kernel(x_ref, of_ref, ob_ref, rg_ref, su_ref, sv_ref, ao_ref, mod_ref, wout_ref, wcat_ref, bs_ref,
                   lng_ref, lnb_ref, wq_ref, x1_ref, h2t_ref, qp_ref, *, tiles_per_batch, n_batch):
    i = pl.program_id(0)
    grp = jnp.minimum(i // tiles_per_batch, n_batch)
    avg = _group_avg_matrix(RET_W)
    a = _group_ln(of_ref[...] + ob_ref[...], avg) * _silu(rg_ref[...])
    u = _gelu(su_ref[...])
    vn = _group_ln(_gelu(sv_ref[...]), avg)
    hm = _head_mask_stack(SGU_GROUPS, CHUNK, SGU_W)
    zero_b = jnp.zeros((), BF16)
    parts = []
    for c in range(TOK_TILE // CHUNK):
        vc = vn[c * CHUNK:(c + 1) * CHUNK].astype(BF16)
        vs = jnp.where(hm, jnp.concatenate([vc] * SGU_GROUPS, axis=0), zero_b)
        mix = _dot(wcat_ref[...], vs) + bs_ref[...]
        parts.append(u[c * CHUNK:(c + 1) * CHUNK] * mix)
    bmix = jnp.concatenate(parts, axis=0)
    y = (_dot(a.astype(BF16), wout_ref[0:RET_W, :])
         + _dot(bmix.astype(BF16), wout_ref[RET_W:RET_W + SGU_W, :])
         + _dot(ao_ref[...], wout_ref[RET_W + SGU_W:, :]))
    gate1 = mod_ref[pl.ds(grp, 1), 2 * D_MODEL:3 * D_MODEL]
    x1 = _ln(DEEPNORM_ALPHA * x_ref[...] + gate1 * y) * lng_ref[...] + lnb_ref[...]
    x1_ref[...] = x1
    shift2 = mod_ref[pl.ds(grp, 1), 3 * D_MODEL:4 * D_MODEL]
    scale2 = mod_ref[pl.ds(grp, 1), 4 * D_MODEL:5 * D_MODEL]
    h2f = _ln(x1) * (1.0 + scale2) + shift2
    for k in range(TOK_TILE // H2T_TOK):
        h2t_ref[k] = h2f[k * H2T_TOK:(k + 1) * H2T_TOK].T.astype(BF16)
    qp_ref[...] = _dot(h2f.astype(BF16), wq_ref[...]).astype(BF16)


def _mixout(x_all, o_f, o_b, rg, su, sv, ao, mod_l, w_out, w_cat, b_tab, ln_g, ln_b, wq, *, n_batch, seq):
    n_tok = x_all.shape[0]
    tiles_per_batch = seq // TOK_TILE
    tok = lambda w: pl.BlockSpec((TOK_TILE, w), lambda i: (i, 0))
    full = lambda a: pl.BlockSpec(a.shape, lambda i: (0,) * a.ndim)
    n_q = wq.shape[1]
    return pl.pallas_call(
        functools.partial(_mixout_kernel, tiles_per_batch=tiles_per_batch, n_batch=n_batch),
        grid=(n_tok // TOK_TILE,),
        in_specs=[tok(D_MODEL), tok(RET_W), tok(RET_W), tok(RET_W), tok(SGU_W), tok(SGU_W), tok(ATT_QW),
                  full(mod_l), full(w_out), full(w_cat), full(b_tab), full(ln_g), full(ln_b), full(wq)],
        out_specs=[tok(D_MODEL), pl.BlockSpec((TOK_TILE // H2T_TOK, D_MODEL, H2T_TOK), lambda i: (i, 0, 0)),
                   tok(n_q)],
        out_shape=[jax.ShapeDtypeStruct((n_tok, D_MODEL), F32),
                   jax.ShapeDtypeStruct((n_tok // H2T_TOK, D_MODEL, H2T_TOK), BF16),
                   jax.ShapeDtypeStruct((n_tok, n_q), BF16)],
        compiler_params=pltpu.CompilerParams(dimension_semantics=("arbitrary",), vmem_limit_bytes=VMEM_LIMIT),
        name="mix_out",
    )(x_all, o_f, o_b, rg, su, sv, ao, mod_l, w_out, w_cat, b_tab, ln_g, ln_b, wq)


_CAND_SEGS = [(a, PEER_TOPK // (a + 1)) for a in range(1, SUBLANES)]


def _batcher_pairs(n):
    pairs = []
    p = 1
    while p < n:
        k = p
        while k >= 1:
            for j in range(k % p, n - k, 2 * k):
                for i in range(min(k, n - j - k)):
                    if (i + j) // (2 * p) == (i + j + k) // (2 * p):
                        pairs.append((i + j, i + j + k))
            k //= 2
        p *= 2
    return pairs


def _top16_sorted(slabs):
    n = len(slabs)
    x = list(slabs)
    for i, j in _batcher_pairs(n):
        x[i], x[j] = jnp.maximum(x[i], x[j]), jnp.minimum(x[i], x[j])
    shift = SUBLANES // 2
    while shift >= 1:
        y = [pltpu.roll(v, shift, 0) for v in x]
        x = [jnp.maximum(x[k], y[n - 1 - k]) for k in range(n)]
        d = n // 2
        while d >= 1:
            for i in range(n):
                if (i & d) == 0:
                    x[i], x[i + d] = jnp.maximum(x[i], x[i + d]), jnp.minimum(x[i], x[i + d])
            d //= 2
        shift //= 2
    return x


def _dup_bf16_words(x):
    hi = pltpu.bitcast(x.astype(BF16).astype(F32), jnp.uint32)
    return hi | (hi >> 16)


def _peer_kernel(h2t_ref, qp_ref, x1_ref, mod_ref, k1_ref, k2_ref, u_ref, vt_ref, lng_ref, lnb_ref, out_ref,
                 s_sc, n_sc, a_sc, rk_sc, b_sc, row_sc, act0_sc, act1_sc, hs0_sc, hs1_sc, acc_sc,
                 *, tiles_per_batch, n_batch, n_blocks):
    t = pl.program_id(0)
    e = pl.program_id(1)
    NG = PEER_TOK // LANES
    K = PEER_TOPK
    half = PEER_QDIM // 2

    @pl.when(e == 0)
    def _prologue():
        acc_sc[...] = jnp.zeros_like(acc_sc)
        for h in range(PEER_HEADS):
            for p, kref in ((0, k1_ref), (1, k2_ref)):
                c0 = h * PEER_QDIM + p * half
                sT = _dot_nt(kref[...], qp_ref[:, c0:c0 + half])
                for g in range(NG):
                    s_sc[2 * h + p, g] = sT[:, g * LANES:(g + 1) * LANES]

        row8 = lax.broadcasted_iota(jnp.int32, (SUBLANES, LANES), 0)

        def stats(it, carry):
            h = it // NG
            g = it % NG
            s1 = s_sc[2 * h, g]
            s2 = s_sc[2 * h + 1, g]
            slabs1 = [s1[SUBLANES * k:SUBLANES * (k + 1)] for k in range(K)]
            slabs2 = [s2[SUBLANES * k:SUBLANES * (k + 1)] for k in range(K)]
            t1 = _top16_sorted(slabs1)
            t2 = _top16_sorted(slabs2)
            v1 = jnp.concatenate([x[0:1] for x in t1], axis=0)
            v2 = jnp.concatenate([x[0:1] for x in t2], axis=0)
            segs = [v1[0:1] + v2]
            for a, n_a in _CAND_SEGS:
                segs.append(jnp.where(row8 < n_a, v1[a:a + 1] + v2[0:SUBLANES], NEG_INF))
            segs.append(v1[SUBLANES:K] + v2[0:1])
            cand = jnp.concatenate(segs, axis=0)
            cur = cand
            tau = None
            for r in range(K):
                tau = jnp.max(cur, axis=0, keepdims=True)
                cur = jnp.where(cur >= tau, NEG_INF, cur)
            cmax = v1[0:1] + v2[0:1]
            z = jnp.sum(jnp.where(cand >= tau, jnp.exp(cand - cmax), 0.0), axis=0, keepdims=True)
            tau8 = jnp.broadcast_to(tau, (SUBLANES, LANES))
            rz8 = jnp.broadcast_to(1.0 / z, (SUBLANES, LANES))
            n_parts, a_parts, rk_parts, b_parts = [], [], [], []
            for k in range(K):
                n = jnp.zeros((SUBLANES, LANES), F32)
                rk = jnp.zeros((SUBLANES, LANES), F32)
                for bb in range(K):
                    n = jnp.where((slabs1[k] + t2[bb]) >= tau8, float(bb + 1), n)
                    rk = jnp.where(t2[bb] > slabs2[k], float(bb + 1), rk)
                n_parts.append(n)
                rk_parts.append(rk)
                a_parts.append(jnp.exp(slabs1[k] - t1[0]) * rz8)
                b_parts.append(jnp.exp(slabs2[k] - t2[0]))
            n_sc[h, g] = _dup_bf16_words(jnp.concatenate(n_parts, axis=0))
            a_sc[h, g] = _dup_bf16_words(jnp.concatenate(a_parts, axis=0))
            rk_sc[h, g] = jnp.concatenate(rk_parts, axis=0).astype(BF16)
            b_sc[h, g] = jnp.concatenate(b_parts, axis=0).astype(BF16)
            return carry

        lax.fori_loop(0, PEER_HEADS * NG, stats, 0)

    rows_per_blk = PEER_EB // PEER_NKEYS
    blks_per_tile = max(1, SUBLANES // rows_per_blk)
    tiles_per_blk = max(1, rows_per_blk // SUBLANES)
    kc_rows = PACKED_ROWS
    zero_b = jnp.zeros((), BF16)
    gate_blk = jnp.clip(e - 1, 0, n_blocks - 1)
    tile0 = pl.multiple_of((gate_blk // blks_per_tile) * (SUBLANES * tiles_per_blk), SUBLANES)

    def step(act_w, act_r, hs_w, hs_r, row_off, first=True, gates=True, second=True):
        def half(c, carry):
            if first:
                a_new = _gelu(_dot(u_ref[0], h2t_ref[c]).astype(BF16))
                act_w[2 * c] = a_new[:, 0:LANES]
                act_w[2 * c + 1] = a_new[:, LANES:2 * LANES]
            if second:
                hs_prev = jnp.concatenate([hs_r[2 * c], hs_r[2 * c + 1]], axis=1)
                acc_sc[c] += _dot(vt_ref[0, 0], hs_prev)
            if gates:
                gate_group(2 * c)
                gate_group(2 * c + 1)
            return carry

        def gate_group(g):
            for h in range(PEER_HEADS):
                for tt in range(tiles_per_blk):
                    n_tile = n_sc[h, g, pl.ds(tile0 + SUBLANES * tt, SUBLANES), :]
                    a_tile = a_sc[h, g, pl.ds(tile0 + SUBLANES * tt, SUBLANES), :]
                    for rr in range(min(SUBLANES, rows_per_blk)):
                        r, k = SUBLANES * tt + rr, row_off + rr
                        row_sc[0, h, r] = pltpu.bitcast(jnp.broadcast_to(n_tile[k:k + 1, :], (SUBLANES, LANES)), BF16)
                        row_sc[1, h, r] = pltpu.bitcast(jnp.broadcast_to(a_tile[k:k + 1, :], (SUBLANES, LANES)), BF16)
            for kc in range(PEER_NKEYS // kc_rows):
                keys = slice(kc * kc_rows, (kc + 1) * kc_rows)
                for r0 in range(0, rows_per_blk, GATE_ROWS):
                    rr = range(r0, min(r0 + GATE_ROWS, rows_per_blk))
                    gates = {r: jnp.zeros((kc_rows, LANES), BF16) for r in rr}
                    for h in range(PEER_HEADS):
                        rk = rk_sc[h, g, keys, :]
                        bb = b_sc[h, g, keys, :]
                        for r in rr:
                            n_full = jnp.concatenate([row_sc[0, h, r]] * (kc_rows // PACKED_ROWS), axis=0)
                            a_full = jnp.concatenate([row_sc[1, h, r]] * (kc_rows // PACKED_ROWS), axis=0)
                            gates[r] = gates[r] + jnp.where(rk < n_full, a_full * bb, zero_b)
                    for r in rr:
                        rows = slice(r * PEER_NKEYS + kc * kc_rows, r * PEER_NKEYS + (kc + 1) * kc_rows)
                        hs_w[g, rows, :] = gates[r] * act_r[g, rows, :]

        lax.fori_loop(0, NG // 2, half, 0)

    even = (act0_sc, act1_sc, hs0_sc, hs1_sc, (1 % blks_per_tile) * rows_per_blk)
    odd = (act1_sc, act0_sc, hs1_sc, hs0_sc, 0)
    steady = (e >= 2) & (e < n_blocks)

    @pl.when(e == 0)
    def _fill0():
        step(*even, gates=False, second=False)

    @pl.when(e == 1)
    def _fill1():
        step(*odd, second=False)

    @pl.when(steady & (e % 2 == 0))
    def _even():
        step(*even)

    @pl.when(steady & (e % 2 == 1))
    def _odd():
        step(*odd)

    @pl.when(e == n_blocks)
    def _drain0():
        step(*even, first=False)

    @pl.when(e == n_blocks + 1)
    def _drain1():
        step(*odd, first=False, gates=False)

    @pl.when(e == n_blocks + 1)
    def _epilogue():
        grp = jnp.minimum(t // tiles_per_batch, n_batch)
        gate2 = mod_ref[pl.ds(grp, 1), 5 * D_MODEL:6 * D_MODEL]
        f = jnp.concatenate([acc_sc[c].T for c in range(NG // 2)], axis=0)
        out_ref[...] = _ln(DEEPNORM_ALPHA * x1_ref[...] + gate2 * f) * lng_ref[...] + lnb_ref[...]


def _peer(h2, qp, x1, mod_l, k1, k2, u_bf, vt3, ln_g, ln_b, *, layer, n_batch, seq, n_tiles):
    tiles_per_batch = seq // PEER_TOK
    n_e = PEER_N // PEER_EB
    assert n_e % 2 == 0 and n_e >= 4
    NG = PEER_TOK // LANES
    tok = lambda w: pl.BlockSpec((PEER_TOK, w), lambda t, e: (t, 0))
    full = lambda a: pl.BlockSpec(a.shape, lambda t, e: (0,) * a.ndim)
    stat = lambda n, dt: pltpu.VMEM((n, NG, PEER_NKEYS, LANES), dt)
    return pl.pallas_call(
        functools.partial(_peer_kernel, tiles_per_batch=tiles_per_batch, n_batch=n_batch, n_blocks=n_e),
        grid=(n_tiles, n_e + 2),
        in_specs=[pl.BlockSpec((PEER_TOK // H2T_TOK, D_MODEL, H2T_TOK), lambda t, e: (t, 0, 0)),
                  tok(qp.shape[1]), tok(D_MODEL), full(mod_l), full(k1), full(k2),
                  pl.BlockSpec((1, PEER_EB, D_MODEL), lambda t, e: (layer, jnp.minimum(e, n_e - 1), 0)),
                  pl.BlockSpec((1, 1, D_MODEL, PEER_EB), lambda t, e: (layer, jnp.clip(e - 2, 0, n_e - 1), 0, 0)),
                  full(ln_g), full(ln_b)],
        out_specs=tok(D_MODEL),
        out_shape=jax.ShapeDtypeStruct((n_tiles * PEER_TOK, D_MODEL), F32),
        scratch_shapes=[stat(2 * PEER_HEADS, F32),
                        stat(PEER_HEADS, jnp.uint32), stat(PEER_HEADS, jnp.uint32),
                        stat(PEER_HEADS, BF16), stat(PEER_HEADS, BF16),
                        pltpu.VMEM((2, PEER_HEADS, PEER_EB // PEER_NKEYS, PACKED_ROWS, LANES), BF16),
                        pltpu.VMEM((NG, PEER_EB, LANES), BF16), pltpu.VMEM((NG, PEER_EB, LANES), BF16),
                        pltpu.VMEM((NG, PEER_EB, LANES), BF16), pltpu.VMEM((NG, PEER_EB, LANES), BF16),
                        pltpu.VMEM((NG // 2, D_MODEL, 2 * LANES), F32)],
        compiler_params=pltpu.CompilerParams(dimension_semantics=("arbitrary", "arbitrary"),
                                             vmem_limit_bytes=VMEM_LIMIT),
        name="peer",
    )(h2, qp, x1, mod_l, k1, k2, u_bf, vt3, ln_g, ln_b)


def _rope_tables(seq, ctx_len):
    rows = seq // GRID_W
    row_id = jnp.repeat(jnp.arange(rows), GRID_W).astype(F32)
    col_id = jnp.tile(jnp.arange(GRID_W), rows).astype(F32)
    inv = jnp.power(ROPE_BASE, -jnp.arange(ROPE_PAIRS, dtype=F32) / ROPE_PAIRS)
    ang_r = row_id[:, None] * inv
    ang_c = col_id[:, None] * inv
    cos64 = jnp.concatenate([jnp.cos(ang_r)] * 2 + [jnp.cos(ang_c)] * 2, axis=-1)
    sin64 = jnp.concatenate([jnp.sin(ang_r)] * 2 + [jnp.sin(ang_c)] * 2, axis=-1)
    n_rep = RET_W // HEAD_DIM
    cos_tab = jnp.concatenate([jnp.tile(cos64, (1, n_rep)), jnp.ones((ctx_len, RET_W), F32)], axis=0)
    sin_tab = jnp.concatenate([jnp.tile(sin64, (1, n_rep)), jnp.zeros((ctx_len, RET_W), F32)], axis=0)
    return cos_tab, sin_tab


def _rot_partner(width):
    l = np.arange(width)
    lo = (l % (2 * ROPE_PAIRS)) < ROPE_PAIRS
    partner = np.where(lo, l + ROPE_PAIRS, l - ROPE_PAIRS)
    sign = np.where(lo, -1.0, 1.0).astype(np.float32)
    return partner, sign


def _slab_perm():
    new = np.arange(ATT_QW)
    r, rem = new // LANES, new % LANES
    g, d = rem // HEAD_DIM, rem % HEAD_DIM
    return (g * GQA_GROUP + r) * HEAD_DIM + d


def _prep_w_in(w_in):
    o_rq, o_rk, o_aq, o_ak = 0, RET_W, 4 * RET_W + 2 * SGU_W, 4 * RET_W + 2 * SGU_W + ATT_QW
    slab = _slab_perm()
    aq = w_in[..., o_aq:o_aq + ATT_QW]

    def partner(block):
        p, sg = _rot_partner(block.shape[-1])
        return block[..., p] * sg

    base = jnp.concatenate([w_in[..., :o_aq], aq[..., slab], w_in[..., o_ak:]], axis=-1)
    rot = jnp.concatenate([partner(w_in[..., o_rq:o_rq + RET_W]), partner(w_in[..., o_rk:o_rk + RET_W]),
                           partner(aq)[..., slab], partner(w_in[..., o_ak:o_ak + ATT_KVW])], axis=-1)
    return jnp.concatenate([base, rot], axis=-1).astype(BF16)


def kernel(x, c, ctx, c_ctx, w_mod, b_mod, w_in, w_out, ret_decay_fwd, ret_decay_bwd, sgu_w, sgu_b, attn_sink,
           ln_mix_g, ln_mix_b, peer_wq, peer_k1, peer_k2, peer_u, peer_v, ln_ffn_g, ln_ffn_b):
    n_batch, seq, _ = x.shape
    ctx_len = ctx.shape[1]
    depth = w_in.shape[0]
    assert seq % PEER_TOK == 0 and seq % TOK_TILE == 0 and ctx_len % CHUNK == 0
    assert (n_batch * ctx_len) % PEER_TOK == 0 and (n_batch * ctx_len) % TOK_TILE == 0
    assert (n_batch * seq) % ctx_len == 0
    assert n_batch + 1 <= SUBLANES
    n_lat = n_batch * seq
    kw = dict(n_batch=n_batch, seq=seq)

    x_all = jnp.concatenate([x.reshape(n_lat, D_MODEL), ctx.reshape(n_batch * ctx_len, D_MODEL)], axis=0)
    cond = jnp.concatenate([c, c_ctx[None, :], jnp.zeros((SUBLANES - n_batch - 1, D_MODEL), F32)], axis=0)
    mods = _modulations(cond, w_mod, b_mod)

    cos_tab, sin_tab = _rope_tables(seq, n_batch * ctx_len)
    w_ext = _prep_w_in(w_in)
    slab = _slab_perm()
    w_out_p = jnp.concatenate([w_out[:, :RET_W + SGU_W], w_out[:, RET_W + SGU_W:][:, slab]], axis=1).astype(BF16)
    w_cat = jnp.transpose(sgu_w, (0, 2, 1, 3)).reshape(depth, CHUNK, SGU_GROUPS * CHUNK).astype(BF16)
    b_tab = jnp.repeat(jnp.transpose(sgu_b, (0, 2, 1)), HEAD_DIM, axis=2)
    decays = jnp.stack([ret_decay_fwd, ret_decay_bwd], axis=1)
    wq = peer_wq.astype(BF16)
    k1 = peer_k1.astype(BF16)
    k2 = peer_k2.astype(BF16)
    u_bf = peer_u.astype(BF16)
    vt3 = jnp.transpose(peer_v.reshape(depth, PEER_N // PEER_EB, PEER_EB, D_MODEL), (0, 1, 3, 2)).astype(BF16)
    row = lambda a: a.reshape(depth, 1, D_MODEL)
    lmg, lmb, lfg, lfb = row(ln_mix_g), row(ln_mix_b), row(ln_ffn_g), row(ln_ffn_b)

    for l in range(depth):
        last = l == depth - 1
        rq, rk, rv, rg, su, sv, aq, ak, av = _inproj(x_all, mods[l], w_ext[l], cos_tab, sin_tab, **kw)
        o_f, o_b = _retention(decays[l], rq, rk, rv, ctx_len=ctx_len, **kw)
        ao = _attention(attn_sink[l], aq, ak, av, ctx_len=ctx_len, **kw)
        x1, h2, qp = _mixout(x_all, o_f, o_b, rg, su, sv, ao, mods[l], w_out_p[l], w_cat[l], b_tab[l],
                             lmg[l], lmb[l], wq[l], **kw)
        n_tiles = (n_lat if last else x_all.shape[0]) // PEER_TOK
        x_all = _peer(h2, qp, x1, mods[l], k1[l], k2[l], u_bf, vt3, lfg[l], lfb[l], layer=l, n_tiles=n_tiles, **kw)
    return x_all[:n_lat].reshape(n_batch, seq, D_MODEL)
```
